```python
import jax, jax.numpy as jnp
from jax import lax
import numpy as np

D_MODEL = 1024
BATCH = 16
SEQ = 4096
DEPTH = 2
DEC_BATCH = 8
DEC_SEQ = 16
PAST_LEN = 4096

CHUNK = 64
N_HEADS = 8
HEAD_DIM = 64
KV_HEADS = 2
GROUP = N_HEADS // KV_HEADS
IDX_HEADS = 8
IDX_DIM = 64
DSA_TOPK = 256
QBLOCK = 128
ATTN_SCALE = HEAD_DIM ** -0.5
RW_HEADS = 8
RW_HEAD_DIM = 64
W_LORA = 64
A_LORA = 64
V_LORA = 32
G_LORA = 160
N_KEYS = 128
N_EXPERTS = N_KEYS * N_KEYS
P_HEADS = 8
P_QDIM = 256
P_HALF = P_QDIM // 2
P_TOPK = 16
P_BLOCK = 256
EPS = 1e-6
GN_EPS = 64e-5
DSA_Q = N_HEADS * HEAD_DIM
DSA_KV = KV_HEADS * HEAD_DIM
IDX_Q = IDX_HEADS * IDX_DIM
RW_WIDTH = RW_HEADS * RW_HEAD_DIM
RW_COLS = 3 * RW_WIDTH + W_LORA + A_LORA + G_LORA
IN_SPLITS = (DSA_Q, DSA_KV, DSA_KV, IDX_Q, IDX_DIM, IDX_HEADS, RW_COLS, 2 * D_MODEL)
IN_COLS = DSA_Q + 2 * DSA_KV + IDX_Q + IDX_DIM + IDX_HEADS + RW_COLS + 2 * D_MODEL

kernel_name = 'hybrid_dsa_rwkv7_peer_stream_step'


def _split(a, sizes):
    parts, off = [], 0
    for s in sizes:
        parts.append(a[..., off:off + s])
        off += s
    return parts


def rmsnorm(x, g):
    xf = x.astype(jnp.float32)
    y = xf * lax.rsqrt(jnp.mean(xf * xf, axis=-1, keepdims=True) + EPS)
    return y.astype(x.dtype) * g


def layernorm(x, g, b, eps):
    xf = x.astype(jnp.float32)
    xc = xf - jnp.mean(xf, axis=-1, keepdims=True)
    y = xc * lax.rsqrt(jnp.mean(xc * xc, axis=-1, keepdims=True) + eps)
    return y.astype(x.dtype) * g + b


def dsa_attention(q, k, v, iq, ik, iw, q_pos, k_pos, topk):
    B, T = q.shape[:2]
    qb = QBLOCK if T % QBLOCK == 0 else T
    nb = T // qb
    k_chunk = k_pos // CHUNK

    def blocks(a):
        return jnp.moveaxis(a.reshape((B, nb, qb) + a.shape[2:]), 1, 0)

    def one_block(args):
        q_b, iq_b, iw_b, pos_b = args
        q_chunk = pos_b // CHUNK
        logits = jnp.einsum('bqhd,bsd->bqhs', iq_b, ik)
        score = jnp.einsum('bqh,bqhs->bqs', iw_b, jax.nn.relu(logits)).astype(jnp.float32)
        admissible = k_chunk[None, :] <= q_chunk[:, None]
        score = jnp.where(admissible[None], score, -jnp.inf)
        _, sel = lax.top_k(score, topk)
        valid = k_chunk[sel] <= q_chunk[None, :, None]
        k_sel = jax.vmap(lambda t, i: t[i])(k, sel)
        v_sel = jax.vmap(lambda t, i: t[i])(v, sel)
        qg = q_b.reshape(B, qb, KV_HEADS, GROUP, HEAD_DIM)
        s = jnp.einsum('bqgrd,bqkgd->bqgrk', qg, k_sel).astype(jnp.float32) * ATTN_SCALE
        s = jnp.where(valid[:, :, None, None, :], s, -jnp.inf)
        p = jax.nn.softmax(s, axis=-1).astype(v.dtype)
        o = jnp.einsum('bqgrk,bqkgd->bqgrd', p, v_sel)
        return o.reshape(B, qb, N_HEADS * HEAD_DIM)

    out = lax.map(one_block, (blocks(q), blocks(iq), blocks(iw), q_pos.reshape(nb, qb)))
    return jnp.moveaxis(out, 0, 1).reshape(B, T, N_HEADS * HEAD_DIM)


def wkv7_scan(S0, r, w, k, v, kk, a):
    f32 = jnp.float32

    def step(S, inp):
        r_t, w_t, k_t, v_t, kk_t, a_t = inp
        s_kk = jnp.einsum('bhij,bhj->bhi', S, kk_t)
        S = (S * w_t[:, :, None, :] - s_kk[..., None] * (kk_t * a_t)[:, :, None, :]
             + v_t[..., None] * k_t[:, :, None, :])
        return S, jnp.einsum('bhij,bhj->bhi', S, r_t)

    xs = tuple(jnp.moveaxis(t.astype(f32), 1, 0) for t in (r, w, k, v, kk, a))
    S, o = lax.scan(step, S0.astype(f32), xs)
    return S.astype(S0.dtype), jnp.moveaxis(o, 0, 1).astype(r.dtype)


def peer(h, w_q, b_q, sub_keys, exp_u, exp_v):
    B, T, D = h.shape
    n = B * T
    nb = -(-n // P_BLOCK)
    flat = jnp.pad(h.reshape(n, D), ((0, nb * P_BLOCK - n), (0, 0)))
    ncand = P_TOPK * P_TOPK

    def one_block(xb):
        q = (xb @ w_q + b_q).reshape(P_BLOCK, P_HEADS, 2, P_HALF)
        s = jnp.einsum('nhcd,hckd->nhck', q, sub_keys).astype(jnp.float32)
        top_s, top_i = lax.top_k(s, P_TOPK)
        cand_s = (top_s[:, :, 0, :, None] + top_s[:, :, 1, None, :]).reshape(P_BLOCK, P_HEADS, ncand)
        cand_i = (top_i[:, :, 0, :, None] * N_KEYS + top_i[:, :, 1, None, :]).reshape(P_BLOCK, P_HEADS, ncand)
        best_s, best_pos = lax.top_k(cand_s, P_TOPK)
        idx = jnp.take_along_axis(cand_i, best_pos, axis=-1)
        gate = jax.nn.softmax(best_s, axis=-1)
        u = exp_u[idx]
        vv = exp_v[idx]
        act = jax.nn.gelu(jnp.einsum('nhkd,nd->nhk', u, xb).astype(jnp.float32), approximate=False)
        coef = (gate * act).astype(xb.dtype)
        return jnp.einsum('nhk,nhkd->nd', coef, vv)

    out = lax.map(one_block, flat.reshape(nb, P_BLOCK, D))
    return out.reshape(nb * P_BLOCK, D)[:n].reshape(B, T, D)


def trunk_layer(x, c, lp, v_first, past, q_offset):
    B, T, _ = x.shape
    mod = jax.nn.silu(c) @ lp['w_ada'] + lp['b_ada']
    sh_t, sc_t, gt_t, sh_c, sc_c, gt_c = jnp.split(mod[:, None, :], 6, axis=-1)
    h = rmsnorm(x, lp['g_norm1']) * (1 + sc_t) + sh_t
    q, k, v, iq, ik, iw, rw, gates = _split(h @ lp['w_in'], IN_SPLITS)

    q = q.reshape(B, T, N_HEADS, HEAD_DIM)
    k = k.reshape(B, T, KV_HEADS, HEAD_DIM)
    v = v.reshape(B, T, KV_HEADS, HEAD_DIM)
    iq = iq.reshape(B, T, IDX_HEADS, IDX_DIM)
    ik = layernorm(ik, lp['idx_k_g'], lp['idx_k_b'], EPS)
    if past is None:
        k_all, v_all, ik_all = k, v, ik
        S0 = jnp.zeros((B, RW_HEADS, RW_HEAD_DIM, RW_HEAD_DIM), x.dtype)
        rw_prev = jnp.zeros((B, 1, RW_COLS), x.dtype)
    else:
        k_past, v_past, ik_past, S0, rw_prev = past
        k_all = jnp.concatenate([k_past.astype(k.dtype), k], axis=1)
        v_all = jnp.concatenate([v_past.astype(v.dtype), v], axis=1)
        ik_all = jnp.concatenate([ik_past.astype(ik.dtype), ik], axis=1)
    L = k_all.shape[1]
    topk = min(DSA_TOPK, L // 4)
    q_pos = q_offset + jnp.arange(T)
    k_pos = jnp.arange(L)
    o_a = dsa_attention(q, k_all, v_all, iq, ik_all, iw, q_pos, k_pos, topk)

    rw_shifted = jnp.concatenate([rw_prev.astype(rw.dtype), rw[:, :-1]], axis=1)
    rw_mix = rw + lp['mu'] * (rw_shifted - rw)
    r, kr, vr, wd, ad, gd = _split(rw_mix, (RW_WIDTH, RW_WIDTH, RW_WIDTH, W_LORA, A_LORA, G_LORA))
    decay_log = -jax.nn.softplus(-(lp['w0'] + jnp.tanh(wd) @ lp['w_up'])) - 0.5
    w = jnp.exp(-jnp.exp(decay_log.astype(jnp.float32)))
    a = jax.nn.sigmoid(lp['a0'] + ad @ lp['a_up'])
    g = jax.nn.sigmoid(gd) @ lp['g_up']
    if v_first is None:
        v_first = vr
    else:
        vr = vr + (v_first - vr) * jax.nn.sigmoid(lp['v0'] + (vr @ lp['v_down']) @ lp['v_up'])

    def heads(t):
        return t.reshape(B, T, RW_HEADS, RW_HEAD_DIM)

    kk = heads(kr * lp['k_k']).astype(jnp.float32)
    kk = (kk / jnp.maximum(jnp.sqrt(jnp.sum(kk * kk, axis=-1, keepdims=True)), 1e-12)).astype(kr.dtype)
    kr = kr * (1 + (a - 1) * lp['k_a'])
    rh, kh, vh = heads(r), heads(kr), heads(vr)
    S_new, o_b = wkv7_scan(S0, rh, heads(w), kh, vh, kk, heads(a))
    o_b = layernorm(o_b, lp['lnx_g'].reshape(RW_HEADS, RW_HEAD_DIM),
                    lp['lnx_b'].reshape(RW_HEADS, RW_HEAD_DIM), GN_EPS)
    o_b = o_b + jnp.sum(rh * kh * lp['r_k'], axis=-1, keepdims=True) * vh
    o_b = o_b.reshape(B, T, RW_WIDTH) * g

    g_a, g_b = jnp.split(gates, 2, axis=-1)
    merged = jax.nn.sigmoid(g_a) * (o_a @ lp['w_oa']) + jax.nn.sigmoid(g_b) * (o_b @ lp['w_ob'])
    x = x + gt_t * (merged @ lp['w_out'])

    h2 = rmsnorm(x, lp['g_norm2']) * (1 + sc_c) + sh_c
    x = x + gt_c * peer(h2, lp['p_wq'], lp['p_bq'], lp['p_keys'], lp['p_u'], lp['p_v'])
    return x, v_first, (k, v, ik, S_new, rw[:, -1:])


def setup_inputs(seed: int = 0) -> dict:
    key = jax.random.key(seed)
    ks = iter(jax.random.split(key, 48))
    D = D_MODEL

    def nrm(shape, scale):
        return jax.random.normal(next(ks), shape, jnp.float32) * scale

    return {
        'x_prompt': nrm((BATCH, SEQ, D), 1.0),
        'x_sample': nrm((DEC_BATCH, DEC_SEQ, D), 1.0),
        'cache_k': nrm((DEPTH, DEC_BATCH, PAST_LEN, KV_HEADS, HEAD_DIM), 1.0),
        'cache_v': nrm((DEPTH, DEC_BATCH, PAST_LEN, KV_HEADS, HEAD_DIM), 1.0),
        'cache_kidx': nrm((DEPTH, DEC_BATCH, PAST_LEN, IDX_DIM), 1.0),
        'state_wkv': nrm((DEPTH, DEC_BATCH, RW_HEADS, RW_HEAD_DIM, RW_HEAD_DIM), 0.3),
        'state_shift': nrm((DEPTH, DEC_BATCH, 1, RW_COLS), 1.0),
        'c_prompt': nrm((BATCH, D), 1.0),
        'c_sample': nrm((DEC_BATCH, D), 1.0),
        'w_ada': nrm((DEPTH, D, 6 * D), 0.5 * D ** -0.5),
        'b_ada': nrm((DEPTH, 6 * D), 0.02),
        'g_norm1': 1.0 + nrm((DEPTH, D), 0.02),
        'w_in': nrm((DEPTH, D, IN_COLS), D ** -0.5),
        'idx_k_g': 1.0 + nrm((DEPTH, IDX_DIM), 0.02),
        'idx_k_b': nrm((DEPTH, IDX_DIM), 0.02),
        'rw_mu': jax.random.uniform(next(ks), (DEPTH, RW_COLS), jnp.float32),
        'rw_w0': nrm((DEPTH, RW_WIDTH), 0.5),
        'rw_w_up': nrm((DEPTH, W_LORA, RW_WIDTH), 0.5 * W_LORA ** -0.5),
        'rw_a0': nrm((DEPTH, RW_WIDTH), 0.5),
        'rw_a_up': nrm((DEPTH, A_LORA, RW_WIDTH), 0.5 * A_LORA ** -0.5),
        'rw_g_up': nrm((DEPTH, G_LORA, RW_WIDTH), G_LORA ** -0.5),
        'rw_k_k': 0.85 + nrm((DEPTH, RW_WIDTH), 0.05),
        'rw_k_a': 1.0 + nrm((DEPTH, RW_WIDTH), 0.05),
        'rw_r_k': nrm((DEPTH, RW_HEADS, RW_HEAD_DIM), 0.1),
        'rw_lnx_g': 1.0 + nrm((DEPTH, RW_WIDTH), 0.02),
        'rw_lnx_b': nrm((DEPTH, RW_WIDTH), 0.02),
        'rw_v0': nrm((DEPTH - 1, RW_WIDTH), 0.5),
        'rw_v_down': nrm((DEPTH - 1, RW_WIDTH, V_LORA), RW_WIDTH ** -0.5),
        'rw_v_up': nrm((DEPTH - 1, V_LORA, RW_WIDTH), 0.5 * V_LORA ** -0.5),
        'w_oa': nrm((DEPTH, DSA_Q, D), DSA_Q ** -0.5),
        'w_ob': nrm((DEPTH, RW_WIDTH, D), RW_WIDTH ** -0.5),
        'w_out': nrm((DEPTH, D, D), D ** -0.5),
        'g_norm2': 1.0 + nrm((DEPTH, D), 0.02),
        'peer_wq': nrm((DEPTH, D, P_HEADS * P_QDIM), D ** -0.5),
        'peer_bq': nrm((DEPTH, P_HEADS * P_QDIM), 0.02),
        'peer_sub_keys': nrm((DEPTH, P_HEADS, 2, N_KEYS, P_HALF), P_HALF ** -0.5),
        'peer_u': nrm((DEPTH, N_EXPERTS, D), D ** -0.5),
        'peer_v': nrm((DEPTH, N_EXPERTS, D), P_HEADS ** -0.5),
        'g_final': 1.0 + nrm((D,), 0.02),
    }


def reference(x_prompt, x_sample, cache_k, cache_v, cache_kidx, state_wkv, state_shift,
              c_prompt, c_sample, w_ada, b_ada, g_norm1, w_in, idx_k_g, idx_k_b,
              rw_mu, rw_w0, rw_w_up, rw_a0, rw_a_up, rw_g_up, rw_k_k, rw_k_a, rw_r_k,
              rw_lnx_g, rw_lnx_b, rw_v0, rw_v_down, rw_v_up, w_oa, w_ob, w_out,
              g_norm2, peer_wq, peer_bq, peer_sub_keys, peer_u, peer_v, g_final):
    xp, xs = x_prompt, x_sample
    vf_p, vf_s = None, None
    past_len = cache_k.shape[2]
    new_p, new_s = [], []
    for l in range(DEPTH):
        lp = {'w_ada': w_ada[l], 'b_ada': b_ada[l], 'g_norm1': g_norm1[l], 'w_in': w_in[l],
              'idx_k_g': idx_k_g[l], 'idx_k_b': idx_k_b[l], 'mu': rw_mu[l], 'w0': rw_w0[l],
              'w_up': rw_w_up[l], 'a0': rw_a0[l], 'a_up': rw_a_up[l], 'g_up': rw_g_up[l],
              'k_k': rw_k_k[l], 'k_a': rw_k_a[l], 'r_k': rw_r_k[l], 'lnx_g': rw_lnx_g[l],
              'lnx_b': rw_lnx_b[l], 'w_oa': w_oa[l], 'w_ob': w_ob[l], 'w_out': w_out[l],
              'g_norm2': g_norm2[l], 'p_wq': peer_wq[l], 'p_bq': peer_bq[l],
              'p_keys': peer_sub_keys[l], 'p_u': peer_u[l], 'p_v': peer_v[l]}
        if l > 0:
            lp['v0'] = rw_v0[l - 1]
            lp['v_down'] = rw_v_down[l - 1]
            lp['v_up'] = rw_v_up[l - 1]
        xp, vf_p, st_p = trunk_layer(xp, c_prompt, lp, vf_p, None, 0)
        past = (cache_k[l], cache_v[l], cache_kidx[l], state_wkv[l], state_shift[l])
        xs, vf_s, st_s = trunk_layer(xs, c_sample, lp, vf_s, past, past_len)
        new_p.append(st_p)
        new_s.append(st_s)
    y_prompt = rmsnorm(xp, g_final)
    y_sample = rmsnorm(xs, g_final)

    def stk(lst, i):
        return jnp.stack([e[i] for e in lst], axis=0)

    return (y_prompt, y_sample,
            stk(new_p, 0), stk(new_p, 1), stk(new_p, 2), stk(new_p, 3), stk(new_p, 4),
            stk(new_s, 0), stk(new_s, 1), stk(new_s, 2), stk(new_s, 3), stk(new_s, 4))
```

```python
import functools

import numpy as np
import jax
import jax.numpy as jnp
from jax import lax
from jax.experimental import pallas as pl
from jax.experimental.pallas import tpu as pltpu

F32 = jnp.float32
BF16 = jnp.bfloat16
I32 = jnp.int32
HIGHEST = lax.Precision.HIGHEST

CHUNK = 64
N_HEADS = 8
HEAD_DIM = 64
KV_HEADS = 2
GROUP = N_HEADS // KV_HEADS
IDX_HEADS = 8
IDX_DIM = 64
DSA_TOPK = 256
ATTN_SCALE = HEAD_DIM ** -0.5
RW_HEADS = 8
RW_HEAD_DIM = 64
RW_WIDTH = RW_HEADS * RW_HEAD_DIM
W_LORA = 64
A_LORA = 64
V_LORA = 32
G_LORA = 160
RW_COLS = 3 * RW_WIDTH + W_LORA + A_LORA + G_LORA
N_KEYS = 128
P_HEADS = 8
P_HALF = 128
P_TOPK = 16
EPS = 1e-6
GN_EPS = 64e-5

LANES = 128
SUBLANES = 8
VMEM_LIMIT = 56 * 1024 * 1024

DSA_Q = N_HEADS * HEAD_DIM
IDX_Q = IDX_HEADS * IDX_DIM
KV_W = KV_HEADS * HEAD_DIM
QI_W = DSA_Q + IDX_Q
SMALL_W = 4 * LANES
RW_PAD = 15 * LANES
NEG = -1e30


def _params(*sem):
    return pltpu.CompilerParams(dimension_semantics=sem, vmem_limit_bytes=VMEM_LIMIT)


def _bdot(a, b):
    return jnp.dot(a.astype(BF16), b.astype(BF16), preferred_element_type=F32)


def _hdot(a, b):
    return jnp.dot(a, b, precision=HIGHEST, preferred_element_type=F32)


def _dot_nt(a, b, precision=None):
    return lax.dot_general(a, b, (((1,), (1,)), ((), ())), precision=precision,
                           preferred_element_type=F32)


def _dot_tn(a, b, precision=None):
    return lax.dot_general(a, b, (((0,), (0,)), ((), ())), precision=precision,
                           preferred_element_type=F32)


def _sigmoid(x):
    return 1.0 / (1.0 + jnp.exp(-x))


def _rms(x):
    return x * lax.rsqrt(jnp.mean(x * x, axis=-1, keepdims=True) + EPS)


def _mod_kernel(c_ref, w_ref, b_ref, o_ref):
    c = c_ref[...]
    o_ref[...] = _bdot(c * _sigmoid(c), w_ref[...]) + b_ref[...]


def _mod_call(c_all, w_ada, b_ada):
    nb, d = c_all.shape
    ncol = w_ada.shape[1] // d
    return pl.pallas_call(
        _mod_kernel,
        grid=(ncol,),
        in_specs=[pl.BlockSpec((nb, d), lambda j: (0, 0)),
                  pl.BlockSpec((d, d), lambda j: (0, j)),
                  pl.BlockSpec((1, d), lambda j: (0, j))],
        out_specs=pl.BlockSpec((nb, d), lambda j: (0, j)),
        out_shape=jax.ShapeDtypeStruct((nb, ncol * d), F32),
        compiler_params=_params("arbitrary"),
        name="adaln_mod",
    )(c_all, w_ada, b_ada.reshape(1, -1))


def _inproj_kernel(x_ref, sc_ref, sh_ref, g_ref, w_ref, ikg_ref, ikb_ref,
                   qi_ref, k_ref, v_ref, ik_ref, iw_ref, rw_ref, gates_ref):
    x = x_ref[0]
    h = (_rms(x) * g_ref[...]) * (1.0 + sc_ref[0]) + sh_ref[0]
    hb = h.astype(BF16)
    o0 = QI_W
    o1 = o0 + SMALL_W
    o2 = o1 + RW_PAD
    qi_ref[0] = jnp.dot(hb, w_ref[:, 0:o0], preferred_element_type=F32)
    small = jnp.dot(hb, w_ref[:, o0:o1], preferred_element_type=F32)
    k_ref[0] = small[:, 0:LANES]
    v_ref[0] = small[:, LANES:2 * LANES]
    ik = small[:, 2 * LANES:2 * LANES + IDX_DIM]
    ikc = ik - jnp.mean(ik, axis=-1, keepdims=True)
    ikn = ikc * lax.rsqrt(jnp.mean(ikc * ikc, axis=-1, keepdims=True) + EPS)
    ik_ref[0] = ikn * ikg_ref[...] + ikb_ref[...]
    iw_ref[0] = small[:, 3 * LANES:3 * LANES + IDX_HEADS]
    rw_ref[0] = jnp.dot(hb, w_ref[:, o1:o2], preferred_element_type=F32)
    gates_ref[0] = jnp.dot(hb, w_ref[:, o2:], preferred_element_type=F32)


def _pack_w_in(w_in):
    d = w_in.shape[0]
    offs = np.cumsum([0, DSA_Q, KV_W, KV_W, IDX_Q, IDX_DIM, IDX_HEADS, RW_COLS, 2 * d])
    q, k, v, iq, ik, iw, rw, gates = (w_in[:, offs[i]:offs[i + 1]] for i in range(8))
    z = lambda n: jnp.zeros((d, n), w_in.dtype)
    packed = jnp.concatenate(
        [q, iq, k, v, ik, z(LANES - IDX_DIM), iw, z(LANES - IDX_HEADS),
         rw, z(RW_PAD - RW_COLS), gates], axis=1)
    return packed.astype(BF16)


def _inproj_call(x, sc, sh, g1, w_packed, ikg, ikb, tm):
    b, t, d = x.shape
    nw = w_packed.shape[1]
    tok = lambda n: pl.BlockSpec((1, tm, n), lambda bi, i: (bi, i, 0))
    row = lambda n: pl.BlockSpec((1, n), lambda bi, i: (0, 0))
    per_b = pl.BlockSpec((1, 1, d), lambda bi, i: (bi, 0, 0))
    widths = (QI_W, LANES, LANES, IDX_DIM, IDX_HEADS, RW_PAD, 2 * d)
    return pl.pallas_call(
        _inproj_kernel,
        grid=(b, t // tm),
        in_specs=[tok(d), per_b, per_b, row(d),
                  pl.BlockSpec((d, nw), lambda bi, i: (0, 0)),
                  row(IDX_DIM), row(IDX_DIM)],
        out_specs=[tok(n) for n in widths],
        out_shape=[jax.ShapeDtypeStruct((b, t, n), F32) for n in widths],
        compiler_params=_params("arbitrary", "arbitrary"),
        name="norm_inproj",
    )(x, sc, sh, g1, w_packed, ikg, ikb)


def _dsa_kernel(qi_ref, iwt_ref, k_ref, vt_ref, ik_ref, o_ref,
                key_scr, bias_scr, iq_scr, qg_scr, *, tq, tk, l_valid, q_offset, topk):
    qb = pl.program_id(1)
    int_min = jnp.int32(-2 ** 31)
    q0 = q_offset + qb * tq
    last_chunk = (q0 + tq - 1) // CHUNK
    n_adm = jnp.minimum((last_chunk + 1) * CHUNK, l_valid)
    n_kt = (n_adm + tk - 1) // tk

    x = qi_ref[0]
    for h in range(IDX_HEADS):
        iq_scr[h] = x[:, DSA_Q + IDX_DIM * h:DSA_Q + IDX_DIM * (h + 1)].astype(BF16)
    for g in range(KV_HEADS):
        for r in range(GROUP):
            h = GROUP * g + r
            qg_scr[g, r * tq:(r + 1) * tq, :] = (
                x[:, HEAD_DIM * h:HEAD_DIM * (h + 1)] * ATTN_SCALE).astype(BF16)
    iwt = iwt_ref[0]
    q_chunk = (q0 + lax.broadcasted_iota(I32, (1, tq), 1)) // CHUNK
    row_iota = lax.broadcasted_iota(I32, (tk, tq), 0)

    def tile_base(kt):
        return pl.multiple_of(kt * tk, tk)

    def score_body(kt, carry):
        base = tile_base(kt)
        ikt = ik_ref[0, pl.ds(base, tk), :].astype(BF16)
        acc = jnp.zeros((tk, tq), F32)
        for h in range(IDX_HEADS):
            acc = acc + iwt[h:h + 1, :] * jnp.maximum(_dot_nt(ikt, iq_scr[h]), 0.0)
        acc = jnp.where(acc == 0.0, 0.0, acc)
        bits = lax.bitcast_convert_type(acc, I32)
        key = jnp.where(bits < 0, bits ^ jnp.int32(0x7FFFFFFF), bits)
        kpos = base + row_iota
        adm = (kpos < l_valid) & ((kpos // CHUNK) <= q_chunk)
        key_scr[pl.ds(base, tk), :] = jnp.where(adm, key, int_min)
        return carry

    lax.fori_loop(0, n_kt, score_body, 0)

    def count(pred_fn):
        def body(kt, c):
            base = tile_base(kt)
            m = jnp.where(pred_fn(key_scr[pl.ds(base, tk), :], base + row_iota), 1, 0)
            return c + jnp.sum(m.reshape(tk // SUBLANES, SUBLANES, tq), axis=0)
        c = lax.fori_loop(0, n_kt, body, jnp.zeros((SUBLANES, tq), I32))
        return jnp.sum(c, axis=0, keepdims=True)

    def bit_body(i, tb):
        cand_b = tb | lax.shift_left(jnp.int32(1), 31 - i)
        cand = cand_b ^ int_min
        cnt = count(lambda kk, idx: kk >= cand)
        return jnp.where(cnt >= topk, cand_b, tb)

    tau = lax.fori_loop(0, 32, bit_body, jnp.zeros((1, tq), I32)) ^ int_min
    cnt_ge = count(lambda kk, idx: kk >= tau)
    cnt_gt = count(lambda kk, idx: kk > tau)
    need = topk - cnt_gt
    excess = (tau > int_min) & (cnt_ge - cnt_gt > need)
    any_excess = jnp.max(jnp.where(excess, 1, 0)) > 0

    idx_bits = 13
    def tie_limit():
        def jbody(i, j):
            cand_j = j | lax.shift_left(jnp.int32(1), idx_bits - 1 - i)
            f = count(lambda kk, idx: (kk == tau) & (idx < cand_j))
            return jnp.where(f <= need, cand_j, j)
        return lax.fori_loop(0, idx_bits, jbody, jnp.zeros((1, tq), I32))

    j_lim = lax.cond(any_excess, tie_limit,
                     lambda: jnp.full((1, tq), 2 ** idx_bits - 1, I32))

    def bias_body(kt, carry):
        base = tile_base(kt)
        kk = key_scr[pl.ds(base, tk), :]
        sel = (kk > tau) | ((kk == tau) & ((base + row_iota) < j_lim))
        sel = sel & (kk != int_min)
        bias_scr[pl.ds(base, tk), :] = jnp.where(sel, 0.0, NEG)
        return carry

    lax.fori_loop(0, n_kt, bias_body, 0)

    def attn_body(kt, carry):
        base = tile_base(kt)
        bias = bias_scr[pl.ds(base, tk), :]
        bias_g = jnp.concatenate([bias] * GROUP, axis=1)
        k_all = k_ref[0, pl.ds(base, tk), :]
        vt_all = vt_ref[0, kt]
        new = []
        for g in range(KV_HEADS):
            m, l, acc = carry[g]
            kg = k_all[:, HEAD_DIM * g:HEAD_DIM * (g + 1)].astype(BF16)
            s = _dot_nt(kg, qg_scr[g]) + bias_g
            m_new = jnp.maximum(m, jnp.max(s, axis=0, keepdims=True))
            alpha = jnp.exp(m - m_new)
            p = jnp.exp(s - m_new)
            l = l * alpha + jnp.sum(p, axis=0, keepdims=True)
            pv = _bdot(vt_all[HEAD_DIM * g:HEAD_DIM * (g + 1), :], p)
            new.append((m_new, l, acc * alpha + pv))
        return tuple(new)

    init = tuple((jnp.full((1, GROUP * tq), NEG, F32), jnp.zeros((1, GROUP * tq), F32),
                  jnp.zeros((HEAD_DIM, GROUP * tq), F32)) for _ in range(KV_HEADS))
    fin = lax.fori_loop(0, n_kt, attn_body, init)
    outs = []
    for g in range(KV_HEADS):
        _, l, acc = fin[g]
        og = acc / l
        for r in range(GROUP):
            outs.append(og[:, r * tq:(r + 1) * tq])
    o_ref[0] = jnp.concatenate(outs, axis=0).T


def _dsa_call(qi, iw, k_all, v_all, ik_all, *, q_offset, tq, tk):
    b, t, _ = qi.shape
    l_valid = k_all.shape[1]
    topk = min(DSA_TOPK, l_valid // 4)
    assert topk <= tk and t % tq == 0
    l_pad = -(-l_valid // tk) * tk
    assert l_pad < 2 ** 13 - 1
    pad = ((0, 0), (0, l_pad - l_valid), (0, 0))
    k_p, v_p, ik_p = (jnp.pad(a, pad) for a in (k_all, v_all, ik_all))
    nkt = l_pad // tk
    vt = jnp.swapaxes(v_p.reshape(b, nkt, tk, KV_W), 2, 3)
    iwt = jnp.swapaxes(iw, 1, 2)
    kern = functools.partial(_dsa_kernel, tq=tq, tk=tk, l_valid=l_valid,
                             q_offset=q_offset, topk=topk)
    return pl.pallas_call(
        kern,
        grid=(b, t // tq),
        in_specs=[pl.BlockSpec((1, tq, QI_W), lambda bi, i: (bi, i, 0)),
                  pl.BlockSpec((1, IDX_HEADS, tq), lambda bi, i: (bi, 0, i)),
                  pl.BlockSpec((1, l_pad, KV_W), lambda bi, i: (bi, 0, 0)),
                  pl.BlockSpec((1, nkt, KV_W, tk), lambda bi, i: (bi, 0, 0, 0)),
                  pl.BlockSpec((1, l_pad, IDX_DIM), lambda bi, i: (bi, 0, 0))],
        out_specs=pl.BlockSpec((1, tq, DSA_Q), lambda bi, i: (bi, i, 0)),
        out_shape=jax.ShapeDtypeStruct((b, t, DSA_Q), F32),
        scratch_shapes=[pltpu.VMEM((l_pad, tq), I32), pltpu.VMEM((l_pad, tq), F32),
                        pltpu.VMEM((IDX_HEADS, tq, IDX_DIM), BF16),
                        pltpu.VMEM((KV_HEADS, GROUP * tq, HEAD_DIM), BF16)],
        compiler_params=_params("arbitrary", "arbitrary"),
        name="dsa_attention",
    )(qi, iwt, k_p, vt, ik_p)


def _rwprep_kernel(*refs, has_vfirst):
    (rw_ref, prev8_ref, shift0_ref, mu_ref, w0_ref, a0_ref, wup_ref, aup_ref, gup_ref,
     kk_ref, ka_ref, rk_ref, bd_ref) = refs[:13]
    if has_vfirst:
        vfirst_ref, v0_ref, vdown_ref, vup_ref = refs[13:17]
        outs = refs[17:]
    else:
        outs = refs[13:]
    r_o, lw_o, k_o, v_o, kkn_o, b_o, g_o, bonus_o = outs
    i = pl.program_id(1)
    rw = rw_ref[0]
    prev = jnp.where(i == 0, shift0_ref[0], prev8_ref[0][SUBLANES - 1:SUBLANES, :])
    row = lax.broadcasted_iota(I32, rw.shape, 0)
    shifted = jnp.where(row == 0, prev, pltpu.roll(rw, 1, 0))
    mix = rw + mu_ref[...] * (shifted - rw)
    w3 = RW_WIDTH
    r = mix[:, 0:w3]
    kr = mix[:, w3:2 * w3]
    vr = mix[:, 2 * w3:3 * w3]
    wa = mix[:, 3 * w3:3 * w3 + LANES]
    gd = mix[:, 3 * w3 + LANES:]
    z = w0_ref[...] + _bdot(jnp.tanh(wa), wup_ref[...])
    nz = -z
    softplus = jnp.maximum(nz, 0.0) + jnp.log(1.0 + jnp.exp(-jnp.abs(nz)))
    lw = -jnp.exp(-softplus - 0.5)
    a = _sigmoid(a0_ref[...] + _bdot(wa, aup_ref[...]))
    g = _bdot(_sigmoid(gd), gup_ref[...])
    if has_vfirst:
        lora = _bdot(_bdot(vr, vdown_ref[...]), vup_ref[...])
        vr = vr + (vfirst_ref[0] - vr) * _sigmoid(v0_ref[...] + lora)
    bd = bd_ref[...]
    kkr = kr * kk_ref[...]
    kkn = kkr / jnp.maximum(jnp.sqrt(_hdot(kkr * kkr, bd)), 1e-12)
    k2 = kr * (1.0 + (a - 1.0) * ka_ref[...])
    r_o[0] = r
    lw_o[0] = lw
    k_o[0] = k2
    v_o[0] = vr
    kkn_o[0] = kkn
    b_o[0] = kkn * a
    g_o[0] = g
    bonus_o[0] = _hdot(r * k2 * rk_ref[...], bd) * vr


def _head_block_diag():
    h = np.arange(RW_WIDTH) // RW_HEAD_DIM
    return jnp.asarray((h[:, None] == h[None, :]).astype(np.float32))


def _pad_rows(w, lo, total):
    return jnp.pad(w, ((lo, total - lo - w.shape[0]), (0, 0))).astype(BF16)


def _rwprep_call(rw, shift0, lp, vfirst, tm):
    b, t, _ = rw.shape
    has_vfirst = vfirst is not None
    tok = lambda n: pl.BlockSpec((1, tm, n), lambda bi, i: (bi, i, 0))
    full = lambda a: pl.BlockSpec(a.shape, lambda bi, i: (0,) * a.ndim)
    consts = [lp["mu"], lp["w0"], lp["a0"], lp["w_up"], lp["a_up"], lp["g_up"],
              lp["k_k"], lp["k_a"], lp["r_k"], lp["bd"]]
    args = [rw, rw, shift0] + consts
    in_specs = [tok(RW_PAD),
                pl.BlockSpec((1, SUBLANES, RW_PAD),
                             lambda bi, i: (bi, jnp.maximum(i * (tm // SUBLANES) - 1, 0), 0)),
                pl.BlockSpec((1, 1, RW_PAD), lambda bi, i: (bi, 0, 0))]
    in_specs += [full(a) for a in consts]
    if has_vfirst:
        extra = [lp["v0"], lp["v_down"], lp["v_up"]]
        args += [vfirst] + extra
        in_specs += [tok(RW_WIDTH)] + [full(a) for a in extra]
    return pl.pallas_call(
        functools.partial(_rwprep_kernel, has_vfirst=has_vfirst),
        grid=(b, t // tm),
        in_specs=in_specs,
        out_specs=[tok(RW_WIDTH)] * 8,
        out_shape=[jax.ShapeDtypeStruct((b, t, RW_WIDTH), F32)] * 8,
        compiler_params=_params("arbitrary", "arbitrary"),
        name="rwkv_prep",
    )(*args)


def _scan_kernel(r_ref, lw_ref, k_ref, v_ref, kk_ref, b_ref, m0_ref, o_ref, mout_ref, m_scr,
                 *, c):
    ci = pl.program_id(1)

    @pl.when(ci == 0)
    def _():
        m_scr[...] = m0_ref[0]

    lw = lw_ref[0]
    row = lax.broadcasted_iota(I32, (c, c), 0)
    col = lax.broadcasted_iota(I32, (c, c), 1)
    incl = row >= col
    strict = row > col
    eye_c = jnp.where(row == col, 1.0, 0.0)
    cum = _hdot(jnp.where(incl, 1.0, 0.0), lw)
    total = cum[c - 1:c, :]
    g_inv = jnp.exp(-cum)
    g_rem = jnp.exp(total - cum)
    kk = kk_ref[0]
    b = b_ref[0]
    k = k_ref[0]
    alpha = kk * jnp.exp(cum - lw)
    beta = b * g_inv
    kappa = k * g_inv
    rho = r_ref[0] * jnp.exp(cum)
    khat = k * g_rem
    bhat = b * g_rem
    g_tot = jnp.exp(total)
    v = v_ref[0]
    n = RW_HEAD_DIM
    rn = lax.broadcasted_iota(I32, (n, n), 0)
    cn = lax.broadcasted_iota(I32, (n, n), 1)
    nlev = int(np.log2(c))
    outs = []
    for h in range(RW_HEADS):
        sl = slice(n * h, n * (h + 1))
        a_h, be_h, ka_h, rh_h, v_h = alpha[:, sl], beta[:, sl], kappa[:, sl], rho[:, sl], v[:, sl]
        l_ab = jnp.where(strict, _dot_nt(a_h, be_h, HIGHEST), 0.0)
        l_ak = jnp.where(strict, _dot_nt(a_h, ka_h, HIGHEST), 0.0)
        l_rk = jnp.where(incl, _dot_nt(rh_h, ka_h, HIGHEST), 0.0)
        l_rb = jnp.where(incl, _dot_nt(rh_h, be_h, HIGHEST), 0.0)
        p = -l_ab
        tinv = eye_c + p
        for _ in range(nlev - 1):
            p = _hdot(p, p)
            tinv = tinv + _hdot(tinv, p)
        m0 = m_scr[h]
        u = _hdot(tinv, _hdot(a_h, m0) + _hdot(l_ak, v_h))
        outs.append(_hdot(rh_h, m0) + _hdot(l_rk, v_h) - _hdot(l_rb, u))
        dg = jnp.where(rn == cn, jnp.broadcast_to(g_tot[:, sl], (n, n)), 0.0)
        m_scr[h] = (_hdot(dg, m0) + _dot_tn(khat[:, sl], v_h, HIGHEST)
                    - _dot_tn(bhat[:, sl], u, HIGHEST))
    o_ref[0] = jnp.concatenate(outs, axis=1)

    @pl.when(ci == pl.num_programs(1) - 1)
    def _():
        mout_ref[0] = m_scr[...]


def _scan_call(r, lw, k, v, kk, bb, m0, c):
    b, t, w = r.shape
    tok = pl.BlockSpec((1, c, w), lambda bi, i: (bi, i, 0))
    st = pl.BlockSpec((1, RW_HEADS, RW_HEAD_DIM, RW_HEAD_DIM), lambda bi, i: (bi, 0, 0, 0))
    return pl.pallas_call(
        functools.partial(_scan_kernel, c=c),
        grid=(b, t // c),
        in_specs=[tok] * 6 + [st],
        out_specs=[tok, st],
        out_shape=[jax.ShapeDtypeStruct((b, t, w), F32),
                   jax.ShapeDtypeStruct(m0.shape, F32)],
        scratch_shapes=[pltpu.VMEM((RW_HEADS, RW_HEAD_DIM, RW_HEAD_DIM), F32)],
        compiler_params=_params("arbitrary", "arbitrary"),
        name="rwkv_scan",
    )(r, lw, k, v, kk, bb, m0)


def _merge_kernel(x_ref, oa_ref, os_ref, bonus_ref, g_ref, gates_ref, gt_ref, sc_ref, sh_ref,
                  woa_ref, wob_ref, wout_ref, lng_ref, lnb_ref, gn2_ref, bd_ref,
                  x1_ref, h2_ref):
    bd = bd_ref[...]
    inv_n = 1.0 / RW_HEAD_DIM
    o = os_ref[0]
    oc = o - _hdot(o, bd) * inv_n
    var = _hdot(oc * oc, bd) * inv_n
    y = oc * lax.rsqrt(var + GN_EPS) * lng_ref[...] + lnb_ref[...]
    ob = (y + bonus_ref[0]) * g_ref[0]
    gates = gates_ref[0]
    d = x_ref.shape[-1]
    merged = (_sigmoid(gates[:, :d]) * _bdot(oa_ref[0], woa_ref[...])
              + _sigmoid(gates[:, d:]) * _bdot(ob, wob_ref[...]))
    x1 = x_ref[0] + gt_ref[0] * _bdot(merged, wout_ref[...])
    x1_ref[0] = x1
    h2 = (_rms(x1) * gn2_ref[...]) * (1.0 + sc_ref[0]) + sh_ref[0]
    h2_ref[0] = h2.astype(BF16)


def _merge_call(x, oa, o_scan, bonus, g, gates, gt, sc, sh, lp, tm):
    b, t, d = x.shape
    tok = lambda n: pl.BlockSpec((1, tm, n), lambda bi, i: (bi, i, 0))
    per_b = pl.BlockSpec((1, 1, d), lambda bi, i: (bi, 0, 0))
    full = lambda a: pl.BlockSpec(a.shape, lambda bi, i: (0,) * a.ndim)
    consts = [lp["w_oa"], lp["w_ob"], lp["w_out"], lp["lnx_g"], lp["lnx_b"], lp["g_norm2"],
              lp["bd"]]
    return pl.pallas_call(
        _merge_kernel,
        grid=(b, t // tm),
        in_specs=[tok(d), tok(DSA_Q), tok(RW_WIDTH), tok(RW_WIDTH), tok(RW_WIDTH), tok(2 * d),
                  per_b, per_b, per_b] + [full(a) for a in consts],
        out_specs=[tok(d), tok(d)],
        out_shape=[jax.ShapeDtypeStruct((b, t, d), F32), jax.ShapeDtypeStruct((b, t, d), BF16)],
        compiler_params=_params("arbitrary", "arbitrary"),
        name="merge_out",
    )(x, oa, o_scan, bonus, g, gates, gt, sc, sh, *consts)


def _kth_largest_rows(x, kth):
    work = x
    cnt = jnp.zeros((1, x.shape[1]), F32)
    tau = jnp.full((1, x.shape[1]), -jnp.inf, F32)
    tops = []
    for _ in range(kth):
        mx = jnp.max(work, axis=0, keepdims=True)
        eq = work == mx
        tau = jnp.where(cnt < kth, mx, tau)
        cnt = cnt + jnp.sum(jnp.where(eq, 1.0, 0.0), axis=0, keepdims=True)
        tops.append(mx)
        work = jnp.where(eq, -jnp.inf, work)
    return tau, tops


def _gelu(x):
    return 0.5 * x * (1.0 + lax.erf(x * (2.0 ** -0.5)))


def _peer_kernel(h2_ref, x1_ref, gt_ref, wq_ref, bq_ref, keys_ref, u_ref, vt_ref, gf_ref,
                 out_ref, s1_scr, s2_scr, e1_scr, e2_scr, tau_scr, acc_scr,
                 *, tn, eb, rep, final):
    e = pl.program_id(1)
    hb = h2_ref[...]

    @pl.when(e == 0)
    def _():
        q = (jnp.dot(hb, wq_ref[...], preferred_element_type=F32) + bq_ref[...]).astype(BF16)
        for h in range(P_HEADS):
            halves = []
            for c in range(2):
                j = 2 * h + c
                s = _dot_nt(keys_ref[j], q[:, P_HALF * j:P_HALF * (j + 1)])
                tau_c, tops = _kth_largest_rows(s, P_TOPK)
                halves.append((s, tau_c, tops))
            (s1, t1, m1), (s2, t2, m2) = halves
            m1s = jnp.concatenate(m1, axis=0)
            cand = jnp.concatenate([m1s + m2[r2] for r2 in range(P_TOPK)], axis=0)
            tau, _ = _kth_largest_rows(cand, P_TOPK)
            z = jnp.sum(jnp.where(cand >= tau, jnp.exp(cand - (m1[0] + m2[0])), 0.0),
                        axis=0, keepdims=True)
            s1_scr[h] = jnp.where(s1 >= t1, s1, -jnp.inf)
            s2_scr[h] = jnp.where(s2 >= t2, s2, -jnp.inf)
            e1_scr[h] = jnp.exp(s1 - m1[0])
            e2_scr[h] = jnp.exp(s2 - m2[0]) / z
            tau_scr[h] = tau
        acc_scr[...] = jnp.zeros_like(acc_scr)

    act = _gelu(_dot_nt(u_ref[...], hb))
    n_i1 = eb // N_KEYS
    coefs = []
    for c in range(n_i1):
        i1 = e * n_i1 + c
        gate = jnp.zeros((N_KEYS, tn), F32)
        for h in range(P_HEADS):
            pair = s1_scr[h, pl.ds(i1, 1), :] + s2_scr[h]
            gate = gate + jnp.where(pair >= tau_scr[h],
                                    e1_scr[h, pl.ds(i1, 1), :] * e2_scr[h], 0.0)
        coefs.append((gate * act[N_KEYS * c:N_KEYS * (c + 1), :]).astype(BF16))
    coef = jnp.concatenate(coefs, axis=0)
    acc_scr[...] += jnp.dot(vt_ref[...], coef, preferred_element_type=F32)

    @pl.when(e == pl.num_programs(1) - 1)
    def _():
        d = acc_scr.shape[0]
        gt = gt_ref[...]
        gt = jnp.broadcast_to(gt, (gt.shape[0], rep, d)).reshape(tn, d)
        x2 = x1_ref[...] + gt * acc_scr[...].T
        if final:
            x2 = _rms(x2) * gf_ref[...]
        out_ref[...] = x2


def _peer_call(h2, x1, gt, lp, g_final, *, tn, eb, final):
    b, t, d = x1.shape
    n = b * t
    assert n % tn == 0 and (t % tn == 0 or tn % t == 0)
    nbt = max(1, tn // t)
    tiles_per_b = max(1, t // tn)
    n_exp = lp["p_u"].shape[0]
    kern = functools.partial(_peer_kernel, tn=tn, eb=eb, rep=tn // nbt, final=final)
    full = lambda a: pl.BlockSpec(a.shape, lambda ti, e: (0,) * a.ndim)
    tok = pl.BlockSpec((tn, d), lambda ti, e: (ti, 0))
    sel = pltpu.VMEM((P_HEADS, N_KEYS, tn), F32)
    out = pl.pallas_call(
        kern,
        grid=(n // tn, n_exp // eb),
        in_specs=[tok, tok,
                  pl.BlockSpec((nbt, 1, d), lambda ti, e: (ti // tiles_per_b, 0, 0)),
                  full(lp["p_wq"]), full(lp["p_bq"]), full(lp["p_keys"]),
                  pl.BlockSpec((eb, d), lambda ti, e: (e, 0)),
                  pl.BlockSpec((d, eb), lambda ti, e: (0, e)),
                  full(g_final)],
        out_specs=tok,
        out_shape=jax.ShapeDtypeStruct((n, d), F32),
        scratch_shapes=[sel, sel, sel, sel, pltpu.VMEM((P_HEADS, 1, tn), F32),
                        pltpu.VMEM((d, tn), F32)],
        compiler_params=_params("arbitrary", "arbitrary"),
        name="peer",
    )(h2.reshape(n, d), x1.reshape(n, d), gt, lp["p_wq"], lp["p_bq"], lp["p_keys"],
      lp["p_u"], lp["p_vt"], g_final)
    return out.reshape(b, t, d)


def _layer(x, mod, lp, vfirst, past, q_offset, g_final, final, tiles):
    b, t, d = x.shape
    sh_t, sc_t, gt_t, sh_c, sc_c, gt_c = (m[:, None, :] for m in jnp.split(mod, 6, axis=-1))
    qi, k, v, ik, iw, rw, gates = _inproj_call(
        x, sc_t, sh_t, lp["g_norm1"], lp["w_in"], lp["idx_k_g"], lp["idx_k_b"], tiles["tm"])

    if past is None:
        k_all, v_all, ik_all = k, v, ik
        m0 = jnp.zeros((b, RW_HEADS, RW_HEAD_DIM, RW_HEAD_DIM), F32)
        shift0 = jnp.zeros((b, 1, RW_PAD), F32)
    else:
        k_past, v_past, ik_past, s0, rw_prev = past
        pl_ = k_past.shape[1]
        k_all = jnp.concatenate([k_past.reshape(b, pl_, KV_W), k], axis=1)
        v_all = jnp.concatenate([v_past.reshape(b, pl_, KV_W), v], axis=1)
        ik_all = jnp.concatenate([ik_past, ik], axis=1)
        m0 = jnp.swapaxes(s0, -1, -2)
        shift0 = jnp.pad(rw_prev, ((0, 0), (0, 0), (0, RW_PAD - RW_COLS)))
    tq = tiles["tq"]
    t_pad = -(-t // tq) * tq
    qpad = ((0, 0), (0, t_pad - t), (0, 0))
    o_a = _dsa_call(jnp.pad(qi, qpad), jnp.pad(iw, qpad), k_all, v_all, ik_all,
                    q_offset=q_offset, tq=tq, tk=tiles["tk"])[:, :t]

    r, lw, k2, v2, kkn, bb, g, bonus = _rwprep_call(rw, shift0, lp, vfirst, tiles["tm"])
    if vfirst is None:
        vfirst = v2
    o_scan, m_new = _scan_call(r, lw, k2, v2, kkn, bb, m0, tiles["c"])

    x1, h2 = _merge_call(x, o_a, o_scan, bonus, g, gates, gt_t, sc_c, sh_c, lp, tiles["tm"])
    x2 = _peer_call(h2, x1, gt_c, lp, g_final, tn=tiles["tn"], eb=tiles["eb"], final=final)
    state = (k.reshape(b, t, KV_HEADS, HEAD_DIM), v.reshape(b, t, KV_HEADS, HEAD_DIM), ik,
             jnp.swapaxes(m_new, -1, -2), rw[:, -1:, :RW_COLS])
    return x2, vfirst, state


def _tiles(t):
    return {"tm": min(t, 256), "tq": LANES, "tk": 256, "c": min(t, CHUNK),
            "tn": 512 if t >= 512 else LANES, "eb": 512}


def kernel(x_prompt, x_sample, cache_k, cache_v, cache_kidx, state_wkv, state_shift, c_prompt, c_sample, w_ada, b_ada, g_norm1, w_in, idx_k_g, idx_k_b, rw_mu, rw_w0, rw_w_up, rw_a0, rw_a_up, rw_g_up, rw_k_k, rw_k_a, rw_r_k, rw_lnx_g, rw_lnx_b, rw_v0, rw_v_down, rw_v_up, w_oa, w_ob, w_out, g_norm2, peer_wq, peer_bq, peer_sub_keys, peer_u, peer_v, g_final):
    depth = w_in.shape[0]
    nbp = x_prompt.shape[0]
    past_len = cache_k.shape[2]
    bd = _head_block_diag()
    row = lambda a: a.reshape(1, -1)
    xp, xs = x_prompt, x_sample
    vf_p, vf_s = None, None
    new_p, new_s = [], []
    c_all = jnp.concatenate([c_prompt, c_sample], axis=0)
    gf = row(g_final)
    for l in range(depth):
        lp = {
            "g_norm1": row(g_norm1[l]), "w_in": _pack_w_in(w_in[l]),
            "idx_k_g": row(idx_k_g[l]), "idx_k_b": row(idx_k_b[l]),
            "mu": jnp.pad(row(rw_mu[l]), ((0, 0), (0, RW_PAD - RW_COLS))),
            "w0": row(rw_w0[l]), "a0": row(rw_a0[l]),
            "w_up": _pad_rows(rw_w_up[l], 0, LANES),
            "a_up": _pad_rows(rw_a_up[l], W_LORA, LANES),
            "g_up": _pad_rows(rw_g_up[l], 0, RW_PAD - 3 * RW_WIDTH - LANES),
            "k_k": row(rw_k_k[l]), "k_a": row(rw_k_a[l]), "r_k": row(rw_r_k[l]),
            "lnx_g": row(rw_lnx_g[l]), "lnx_b": row(rw_lnx_b[l]), "bd": bd,
            "w_oa": w_oa[l].astype(BF16), "w_ob": w_ob[l].astype(BF16),
            "w_out": w_out[l].astype(BF16), "g_norm2": row(g_norm2[l]),
            "p_wq": peer_wq[l].astype(BF16), "p_bq": row(peer_bq[l]),
            "p_keys": peer_sub_keys[l].reshape(2 * P_HEADS, N_KEYS, P_HALF).astype(BF16),
            "p_u": peer_u[l].astype(BF16), "p_vt": peer_v[l].T.astype(BF16),
        }
        if l > 0:
            lp["v0"] = row(rw_v0[l - 1])
            lp["v_down"] = jnp.pad(rw_v_down[l - 1], ((0, 0), (0, LANES - V_LORA))).astype(BF16)
            lp["v_up"] = _pad_rows(rw_v_up[l - 1], 0, LANES)
        mod = _mod_call(c_all, w_ada[l], b_ada[l])
        final = l == depth - 1
        xp, vf_p, st_p = _layer(xp, mod[:nbp], lp, vf_p, None, 0, gf, final,
                                _tiles(xp.shape[1]))
        past = (cache_k[l], cache_v[l], cache_kidx[l], state_wkv[l], state_shift[l])
        xs, vf_s, st_s = _layer(xs, mod[nbp:], lp, vf_s, past, past_len, gf, final,
                                _tiles(xs.shape[1]))
        new_p.append(st_p)
        new_s.append(st_s)

    def stk(lst, i):
        return jnp.stack([e[i] for e in lst], axis=0)

    return (xp, xs,
            stk(new_p, 0), stk(new_p, 1), stk(new_p, 2), stk(new_p, 3), stk(new_p, 4),
            stk(new_s, 0), stk(new_s, 1), stk(new_s, 2), stk(new_s, 3), stk(new_s, 4))
```

```python
import functools

import numpy as np
import jax
import jax.numpy as jnp
from jax import lax
from jax.experimental import pallas as pl
from jax.experimental.pallas import tpu as pltpu

F32 = jnp.float32
BF16 = jnp.bfloat16
I32 = jnp.int32
HIGHEST = lax.Precision.HIGHEST

CHUNK = 64
N_HEADS = 8
HEAD_DIM = 64
KV_HEADS = 2
GROUP = N_HEADS // KV_HEADS
IDX_HEADS = 8
IDX_DIM = 64
DSA_TOPK = 256
ATTN_SCALE = HEAD_DIM ** -0.5
RW_HEADS = 8
RW_HEAD_DIM = 64
RW_WIDTH = RW_HEADS * RW_HEAD_DIM
W_LORA = 64
A_LORA = 64
V_LORA = 32
G_LORA = 160
RW_COLS = 3 * RW_WIDTH + W_LORA + A_LORA + G_LORA
N_KEYS = 128
P_HEADS = 8
P_HALF = 128
P_TOPK = 16
EPS = 1e-6
GN_EPS = 64e-5

LANES = 128
SUBLANES = 8
VMEM_LIMIT = 56 * 1024 * 1024

DSA_Q = N_HEADS * HEAD_DIM
IDX_Q = IDX_HEADS * IDX_DIM
KV_W = KV_HEADS * HEAD_DIM
QI_W = DSA_Q + IDX_Q
SMALL_W = 4 * LANES
RW_PAD = 15 * LANES
NEG = -1e30


def _params(*sem):
    return pltpu.CompilerParams(dimension_semantics=sem, vmem_limit_bytes=VMEM_LIMIT)


def _bdot(a, b):
    return jnp.dot(a.astype(BF16), b.astype(BF16), preferred_element_type=F32)


def _hdot(a, b):
    return jnp.dot(a, b, precision=HIGHEST, preferred_element_type=F32)


def _dot_nt(a, b, precision=None):
    return lax.dot_general(a, b, (((1,), (1,)), ((), ())), precision=precision,
                           preferred_element_type=F32)


def _dot_tn(a, b, precision=None):
    return lax.dot_general(a, b, (((0,), (0,)), ((), ())), precision=precision,
                           preferred_element_type=F32)


def _sigmoid(x):
    return 1.0 / (1.0 + jnp.exp(-x))


def _rms(x):
    return x * lax.rsqrt(jnp.mean(x * x, axis=-1, keepdims=True) + EPS)


def _mod_kernel(c_ref, w_ref, b_ref, o_ref):
    c = c_ref[...]
    o_ref[...] = _bdot(c * _sigmoid(c), w_ref[...]) + b_ref[...]


def _mod_call(c_all, w_ada, b_ada):
    nb, d = c_all.shape
    ncol = w_ada.shape[1] // d
    return pl.pallas_call(
        _mod_kernel,
        grid=(ncol,),
        in_specs=[pl.BlockSpec((nb, d), lambda j: (0, 0)),
                  pl.BlockSpec((d, d), lambda j: (0, j)),
                  pl.BlockSpec((1, d), lambda j: (0, j))],
        out_specs=pl.BlockSpec((nb, d), lambda j: (0, j)),
        out_shape=jax.ShapeDtypeStruct((nb, ncol * d), F32),
        compiler_params=_params("arbitrary"),
        name="adaln_mod",
    )(c_all, w_ada, b_ada.reshape(1, -1))


def _inproj_kernel(x_ref, sc_ref, sh_ref, g_ref, w_ref, ikg_ref, ikb_ref,
                   qi_ref, k_ref, v_ref, ik_ref, iw_ref, rw_ref, gates_ref):
    x = x_ref[0]
    h = (_rms(x) * g_ref[...]) * (1.0 + sc_ref[0]) + sh_ref[0]
    hb = h.astype(BF16)
    o0 = QI_W
    o1 = o0 + SMALL_W
    o2 = o1 + RW_PAD
    qi_ref[0] = jnp.dot(hb, w_ref[:, 0:o0], preferred_element_type=F32)
    small = jnp.dot(hb, w_ref[:, o0:o1], preferred_element_type=F32)
    k_ref[0] = small[:, 0:LANES]
    v_ref[0] = small[:, LANES:2 * LANES]
    ik = small[:, 2 * LANES:2 * LANES + IDX_DIM]
    ikc = ik - jnp.mean(ik, axis=-1, keepdims=True)
    ikn = ikc * lax.rsqrt(jnp.mean(ikc * ikc, axis=-1, keepdims=True) + EPS)
    ik_ref[0] = ikn * ikg_ref[...] + ikb_ref[...]
    iw_ref[0] = small[:, 3 * LANES:3 * LANES + IDX_HEADS]
    rw_ref[0] = jnp.dot(hb, w_ref[:, o1:o2], preferred_element_type=F32)
    gates_ref[0] = jnp.dot(hb, w_ref[:, o2:], preferred_element_type=F32)


def _pack_w_in(w_in):
    d = w_in.shape[0]
    offs = np.cumsum([0, DSA_Q, KV_W, KV_W, IDX_Q, IDX_DIM, IDX_HEADS, RW_COLS, 2 * d])
    q, k, v, iq, ik, iw, rw, gates = (w_in[:, offs[i]:offs[i + 1]] for i in range(8))
    z = lambda n: jnp.zeros((d, n), w_in.dtype)
    packed = jnp.concatenate(
        [q, iq, k, v, ik, z(LANES - IDX_DIM), iw, z(LANES - IDX_HEADS),
         rw, z(RW_PAD - RW_COLS), gates], axis=1)
    return packed.astype(BF16)


def _inproj_call(x, sc, sh, g1, w_packed, ikg, ikb, tm):
    b, t, d = x.shape
    nw = w_packed.shape[1]
    tok = lambda n: pl.BlockSpec((1, tm, n), lambda bi, i: (bi, i, 0))
    row = lambda n: pl.BlockSpec((1, n), lambda bi, i: (0, 0))
    per_b = pl.BlockSpec((1, 1, d), lambda bi, i: (bi, 0, 0))
    widths = (QI_W, LANES, LANES, IDX_DIM, IDX_HEADS, RW_PAD, 2 * d)
    return pl.pallas_call(
        _inproj_kernel,
        grid=(b, t // tm),
        in_specs=[tok(d), per_b, per_b, row(d),
                  pl.BlockSpec((d, nw), lambda bi, i: (0, 0)),
                  row(IDX_DIM), row(IDX_DIM)],
        out_specs=[tok(n) for n in widths],
        out_shape=[jax.ShapeDtypeStruct((b, t, n), F32) for n in widths],
        compiler_params=_params("arbitrary", "arbitrary"),
        name="norm_inproj",
    )(x, sc, sh, g1, w_packed, ikg, ikb)


def _dsa_kernel(qi_ref, iwt_ref, k_ref, vt_ref, ik_ref, o_ref,
                key_scr, bias_scr, iq_scr, qg_scr, *, tq, tk, l_valid, q_offset, topk):
    qb = pl.program_id(1)
    int_min = jnp.int32(-2 ** 31)
    q0 = q_offset + qb * tq
    last_chunk = (q0 + tq - 1) // CHUNK
    n_adm = jnp.minimum((last_chunk + 1) * CHUNK, l_valid)
    n_kt = (n_adm + tk - 1) // tk

    x = qi_ref[0]
    for h in range(IDX_HEADS):
        iq_scr[h] = x[:, DSA_Q + IDX_DIM * h:DSA_Q + IDX_DIM * (h + 1)].astype(BF16)
    for g in range(KV_HEADS):
        for r in range(GROUP):
            h = GROUP * g + r
            qg_scr[g, r * tq:(r + 1) * tq, :] = (
                x[:, HEAD_DIM * h:HEAD_DIM * (h + 1)] * ATTN_SCALE).astype(BF16)
    iwt = iwt_ref[0]
    q_chunk = (q0 + lax.broadcasted_iota(I32, (1, tq), 1)) // CHUNK
    row_iota = lax.broadcasted_iota(I32, (tk, tq), 0)

    def tile_base(kt):
        return pl.multiple_of(kt * tk, tk)

    def score_body(kt, carry):
        base = tile_base(kt)
        ikt = ik_ref[0, pl.ds(base, tk), :].astype(BF16)
        acc = jnp.zeros((tk, tq), F32)
        for h in range(IDX_HEADS):
            acc = acc + iwt[h:h + 1, :] * jnp.maximum(_dot_nt(ikt, iq_scr[h]), 0.0)
        acc = jnp.where(acc == 0.0, 0.0, acc)
        bits = lax.bitcast_convert_type(acc, I32)
        key = jnp.where(bits < 0, bits ^ jnp.int32(0x7FFFFFFF), bits)
        kpos = base + row_iota
        adm = (kpos < l_valid) & ((kpos // CHUNK) <= q_chunk)
        key_scr[pl.ds(base, tk), :] = jnp.where(adm, key, int_min)
        return carry

    lax.fori_loop(0, n_kt, score_body, 0)

    acc_rows = 4 * SUBLANES

    def count(pred_fn):
        def body(kt, c):
            base = tile_base(kt)
            m = jnp.where(pred_fn(key_scr[pl.ds(base, tk), :], base + row_iota), 1, 0)
            return c + jnp.sum(m.reshape(tk // acc_rows, acc_rows, tq), axis=0)
        c = lax.fori_loop(0, n_kt, body, jnp.zeros((acc_rows, tq), I32))
        return jnp.sum(c, axis=0, keepdims=True)

    def bit_body(i, tb):
        cand_b = tb | lax.shift_left(jnp.int32(1), 31 - i)
        cand = cand_b ^ int_min
        cnt = count(lambda kk, idx: kk >= cand)
        return jnp.where(cnt >= topk, cand_b, tb)

    tau = lax.fori_loop(0, 32, bit_body, jnp.zeros((1, tq), I32)) ^ int_min
    cnt_ge = count(lambda kk, idx: kk >= tau)
    cnt_gt = count(lambda kk, idx: kk > tau)
    need = topk - cnt_gt
    excess = (tau > int_min) & (cnt_ge - cnt_gt > need)
    any_excess = jnp.max(jnp.where(excess, 1, 0)) > 0

    idx_bits = 13
    def tie_limit():
        def jbody(i, j):
            cand_j = j | lax.shift_left(jnp.int32(1), idx_bits - 1 - i)
            f = count(lambda kk, idx: (kk == tau) & (idx < cand_j))
            return jnp.where(f <= need, cand_j, j)
        return lax.fori_loop(0, idx_bits, jbody, jnp.zeros((1, tq), I32))

    j_lim = lax.cond(any_excess, tie_limit,
                     lambda: jnp.full((1, tq), 2 ** idx_bits - 1, I32))

    def bias_body(kt, carry):
        base = tile_base(kt)
        kk = key_scr[pl.ds(base, tk), :]
        sel = (kk > tau) | ((kk == tau) & ((base + row_iota) < j_lim))
        sel = sel & (kk != int_min)
        bias_scr[pl.ds(base, tk), :] = jnp.where(sel, 0.0, NEG)
        return carry

    lax.fori_loop(0, n_kt, bias_body, 0)

    def attn_body(kt, carry):
        base = tile_base(kt)
        bias = bias_scr[pl.ds(base, tk), :]
        k_all = k_ref[0, pl.ds(base, tk), :]
        vt_all = vt_ref[0, kt]
        new = []
        s_groups = [_dot_nt(k_all[:, HEAD_DIM * g:HEAD_DIM * (g + 1)].astype(BF16), qg_scr[g])
                    for g in range(KV_HEADS)]
        for g in range(KV_HEADS):
            s_all = s_groups[g]
            ps, stats = [], []
            for r in range(GROUP):
                m, l, acc = carry[GROUP * g + r]
                s = s_all[:, r * tq:(r + 1) * tq] + bias
                m_new = jnp.maximum(m, jnp.max(s, axis=0, keepdims=True))
                alpha = jnp.exp(m - m_new)
                p = jnp.exp(s - m_new)
                ps.append(p.astype(BF16))
                stats.append((m_new, l * alpha + jnp.sum(p, axis=0, keepdims=True), alpha, acc))
            pv = jnp.dot(vt_all[HEAD_DIM * g:HEAD_DIM * (g + 1), :].astype(BF16),
                         jnp.concatenate(ps, axis=1), preferred_element_type=F32)
            for r, (m_new, l_new, alpha, acc) in enumerate(stats):
                new.append((m_new, l_new, acc * alpha + pv[:, r * tq:(r + 1) * tq]))
        return tuple(new)

    init = tuple((jnp.full((1, tq), NEG, F32), jnp.zeros((1, tq), F32),
                  jnp.zeros((HEAD_DIM, tq), F32)) for _ in range(N_HEADS))
    fin = lax.fori_loop(0, n_kt, attn_body, init)
    o_ref[0] = jnp.concatenate([acc / l for _, l, acc in fin], axis=0).T


def _dsa_call(qi, iw, k_all, v_all, ik_all, *, q_offset, tq, tk):
    b, t, _ = qi.shape
    l_valid = k_all.shape[1]
    topk = min(DSA_TOPK, l_valid // 4)
    assert topk <= tk and t % tq == 0
    l_pad = -(-l_valid // tk) * tk
    assert l_pad < 2 ** 13 - 1
    pad = ((0, 0), (0, l_pad - l_valid), (0, 0))
    k_p, v_p, ik_p = (jnp.pad(a, pad) for a in (k_all, v_all, ik_all))
    nkt = l_pad // tk
    vt = jnp.swapaxes(v_p.reshape(b, nkt, tk, KV_W), 2, 3)
    iwt = jnp.swapaxes(iw, 1, 2)
    kern = functools.partial(_dsa_kernel, tq=tq, tk=tk, l_valid=l_valid,
                             q_offset=q_offset, topk=topk)
    return pl.pallas_call(
        kern,
        grid=(b, t // tq),
        in_specs=[pl.BlockSpec((1, tq, QI_W), lambda bi, i: (bi, i, 0)),
                  pl.BlockSpec((1, IDX_HEADS, tq), lambda bi, i: (bi, 0, i)),
                  pl.BlockSpec((1, l_pad, KV_W), lambda bi, i: (bi, 0, 0)),
                  pl.BlockSpec((1, nkt, KV_W, tk), lambda bi, i: (bi, 0, 0, 0)),
                  pl.BlockSpec((1, l_pad, IDX_DIM), lambda bi, i: (bi, 0, 0))],
        out_specs=pl.BlockSpec((1, tq, DSA_Q), lambda bi, i: (bi, i, 0)),
        out_shape=jax.ShapeDtypeStruct((b, t, DSA_Q), F32),
        scratch_shapes=[pltpu.VMEM((l_pad, tq), I32), pltpu.VMEM((l_pad, tq), F32),
                        pltpu.VMEM((IDX_HEADS, tq, IDX_DIM), BF16),
                        pltpu.VMEM((KV_HEADS, GROUP * tq, HEAD_DIM), BF16)],
        compiler_params=_params("arbitrary", "arbitrary"),
        name="dsa_attention",
    )(qi, iwt, k_p, vt, ik_p)


def _rwprep_kernel(*refs, has_vfirst):
    (rw_ref, prev8_ref, shift0_ref, mu_ref, w0_ref, a0_ref, wup_ref, aup_ref, gup_ref,
     kk_ref, ka_ref, rk_ref, bd_ref) = refs[:13]
    if has_vfirst:
        vfirst_ref, v0_ref, vdown_ref, vup_ref = refs[13:17]
        outs = refs[17:]
    else:
        outs = refs[13:]
    r_o, lw_o, k_o, v_o, kkn_o, b_o, g_o, bonus_o = outs
    i = pl.program_id(1)
    rw = rw_ref[0]
    prev = jnp.where(i == 0, shift0_ref[0], prev8_ref[0][SUBLANES - 1:SUBLANES, :])
    row = lax.broadcasted_iota(I32, rw.shape, 0)
    shifted = jnp.where(row == 0, prev, pltpu.roll(rw, 1, 0))
    mix = rw + mu_ref[...] * (shifted - rw)
    w3 = RW_WIDTH
    r = mix[:, 0:w3]
    kr = mix[:, w3:2 * w3]
    vr = mix[:, 2 * w3:3 * w3]
    wa = mix[:, 3 * w3:3 * w3 + LANES]
    gd = mix[:, 3 * w3 + LANES:]
    z = w0_ref[...] + _bdot(jnp.tanh(wa), wup_ref[...])
    nz = -z
    softplus = jnp.maximum(nz, 0.0) + jnp.log(1.0 + jnp.exp(-jnp.abs(nz)))
    lw = -jnp.exp(-softplus - 0.5)
    a = _sigmoid(a0_ref[...] + _bdot(wa, aup_ref[...]))
    g = _bdot(_sigmoid(gd), gup_ref[...])
    if has_vfirst:
        lora = _bdot(_bdot(vr, vdown_ref[...]), vup_ref[...])
        vr = vr + (vfirst_ref[0] - vr) * _sigmoid(v0_ref[...] + lora)
    bd = bd_ref[...]
    kkr = kr * kk_ref[...]
    kkn = kkr / jnp.maximum(jnp.sqrt(_hdot(kkr * kkr, bd)), 1e-12)
    k2 = kr * (1.0 + (a - 1.0) * ka_ref[...])
    r_o[0] = r
    lw_o[0] = lw
    k_o[0] = k2
    v_o[0] = vr
    kkn_o[0] = kkn
    b_o[0] = kkn * a
    g_o[0] = g
    bonus_o[0] = _hdot(r * k2 * rk_ref[...], bd) * vr


def _head_block_diag():
    h = np.arange(RW_WIDTH) // RW_HEAD_DIM
    return jnp.asarray((h[:, None] == h[None, :]).astype(np.float32))


def _pad_rows(w, lo, total):
    return jnp.pad(w, ((lo, total - lo - w.shape[0]), (0, 0))).astype(BF16)


def _rwprep_call(rw, shift0, lp, vfirst, tm):
    b, t, _ = rw.shape
    has_vfirst = vfirst is not None
    tok = lambda n: pl.BlockSpec((1, tm, n), lambda bi, i: (bi, i, 0))
    full = lambda a: pl.BlockSpec(a.shape, lambda bi, i: (0,) * a.ndim)
    consts = [lp["mu"], lp["w0"], lp["a0"], lp["w_up"], lp["a_up"], lp["g_up"],
              lp["k_k"], lp["k_a"], lp["r_k"], lp["bd"]]
    args = [rw, rw, shift0] + consts
    in_specs = [tok(RW_PAD),
                pl.BlockSpec((1, SUBLANES, RW_PAD),
                             lambda bi, i: (bi, jnp.maximum(i * (tm // SUBLANES) - 1, 0), 0)),
                pl.BlockSpec((1, 1, RW_PAD), lambda bi, i: (bi, 0, 0))]
    in_specs += [full(a) for a in consts]
    if has_vfirst:
        extra = [lp["v0"], lp["v_down"], lp["v_up"]]
        args += [vfirst] + extra
        in_specs += [tok(RW_WIDTH)] + [full(a) for a in extra]
    return pl.pallas_call(
        functools.partial(_rwprep_kernel, has_vfirst=has_vfirst),
        grid=(b, t // tm),
        in_specs=in_specs,
        out_specs=[tok(RW_WIDTH)] * 8,
        out_shape=[jax.ShapeDtypeStruct((b, t, RW_WIDTH), F32)] * 8,
        compiler_params=_params("arbitrary", "arbitrary"),
        name="rwkv_prep",
    )(*args)


def _scan_intra_kernel(r_ref, lw_ref, k_ref, v_ref, kk_ref, b_ref,
                       r2_ref, oi_ref, a_ref, d_ref, *, c, nc):
    row = lax.broadcasted_iota(I32, (c, c), 0)
    col = lax.broadcasted_iota(I32, (c, c), 1)
    incl = row >= col
    strict = row > col
    eye_c = jnp.where(row == col, 1.0, 0.0)
    ones_incl = jnp.where(incl, 1.0, 0.0)
    n = RW_HEAD_DIM
    rn = lax.broadcasted_iota(I32, (n, n), 0)
    cn = lax.broadcasted_iota(I32, (n, n), 1)
    nlev = int(np.log2(c))

    prep = []
    for ci in range(nc):
        rows = slice(ci * c, (ci + 1) * c)
        lw, k, b = lw_ref[0, rows, :], k_ref[0, rows, :], b_ref[0, rows, :]
        cum = _hdot(ones_incl, lw)
        total = cum[c - 1:c, :]
        g_inv = jnp.exp(-cum)
        g_rem = jnp.exp(total - cum)
        prep.append(dict(
            alpha=kk_ref[0, rows, :] * jnp.exp(cum - lw), beta=b * g_inv, kappa=k * g_inv,
            rho=r_ref[0, rows, :] * jnp.exp(cum), khat=k * g_rem, bhat=b * g_rem,
            g_tot=jnp.exp(total), v=v_ref[0, rows, :]))
    units = [(ci, h) for ci in range(nc) for h in range(RW_HEADS)]

    def head(ci, h, name):
        return prep[ci][name][:, n * h:n * (h + 1)]

    grams = [_dot_nt(
        jnp.concatenate([head(ci, h, "alpha"), head(ci, h, "rho")], axis=0).astype(BF16),
        jnp.concatenate([head(ci, h, "beta"), head(ci, h, "kappa")], axis=0).astype(BF16))
        for ci, h in units]
    l_ak = [jnp.where(strict, g[:c, c:], 0.0) for g in grams]
    l_rb = [jnp.where(incl, g[c:, :c], 0.0) for g in grams]
    l_rk = [jnp.where(incl, g[c:, c:], 0.0) for g in grams]
    ps = [-jnp.where(strict, g[:c, :c], 0.0) for g in grams]
    tinvs = [eye_c + p for p in ps]
    for _ in range(nlev - 1):
        ps = [_bdot(p, p) for p in ps]
        tinvs = [t + _bdot(t, p) for t, p in zip(tinvs, ps)]
    lvs = [_bdot(jnp.concatenate([ak, rk], axis=0), head(ci, h, "v"))
           for ak, rk, (ci, h) in zip(l_ak, l_rk, units)]
    wys = [_hdot(t, jnp.concatenate([head(ci, h, "alpha"), lv[:c]], axis=1))
           for t, lv, (ci, h) in zip(tinvs, lvs, units)]
    rbs = [_bdot(rb, wy) for rb, wy in zip(l_rb, wys)]
    bws = [_dot_tn(head(ci, h, "bhat"), wy, HIGHEST) for wy, (ci, h) in zip(wys, units)]
    kvs = [_dot_tn(head(ci, h, "khat"), head(ci, h, "v"), HIGHEST) for ci, h in units]
    for ci in range(nc):
        rows = slice(ci * c, (ci + 1) * c)
        mine = [u for u, (cj, _) in enumerate(units) if cj == ci]
        r2_ref[0, rows, :] = jnp.concatenate(
            [head(ci, h, "rho") - rbs[u][:, :n] for h, u in enumerate(mine)], axis=1)
        oi_ref[0, rows, :] = jnp.concatenate(
            [lvs[u][c:] - rbs[u][:, n:] for u in mine], axis=1)
        for h, u in enumerate(mine):
            dg = jnp.where(rn == cn,
                           jnp.broadcast_to(prep[ci]["g_tot"][:, n * h:n * (h + 1)], (n, n)), 0.0)
            a_ref[0, ci, h] = dg - bws[u][:, :n]
            d_ref[0, ci, h] = kvs[u] - bws[u][:, n:]


def _scan_inter_kernel(r2_ref, oi_ref, a_ref, d_ref, m0_ref, o_ref, mout_ref, m_scr, *, nb):
    ci = pl.program_id(1)

    @pl.when(ci == 0)
    def _():
        m_scr[...] = m0_ref[...]

    n = RW_HEAD_DIM
    for bi in range(nb):
        r2 = r2_ref[bi]
        outs = []
        for h in range(RW_HEADS):
            m0 = m_scr[bi, h]
            outs.append(_hdot(r2[:, n * h:n * (h + 1)], m0))
            m_scr[bi, h] = _hdot(a_ref[bi, 0, h], m0) + d_ref[bi, 0, h]
        o_ref[bi] = jnp.concatenate(outs, axis=1) + oi_ref[bi]

    @pl.when(ci == pl.num_programs(1) - 1)
    def _():
        mout_ref[...] = m_scr[...]


def _scan_call(r, lw, k, v, kk, bb, m0, c):
    b, t, w = r.shape
    nch = t // c
    nc = 2 if nch % 2 == 0 else 1
    nb = next(n for n in (4, 2, 1) if b % n == 0)
    hd = RW_HEAD_DIM
    tok = pl.BlockSpec((1, nc * c, w), lambda bi, i: (bi, i, 0))
    mats = pl.BlockSpec((1, nc, RW_HEADS, hd, hd), lambda bi, i: (bi, i, 0, 0, 0))
    r2, oi, a_mat, d_mat = pl.pallas_call(
        functools.partial(_scan_intra_kernel, c=c, nc=nc),
        grid=(b, nch // nc),
        in_specs=[tok] * 6,
        out_specs=[tok, tok, mats, mats],
        out_shape=[jax.ShapeDtypeStruct((b, t, w), F32)] * 2
        + [jax.ShapeDtypeStruct((b, nch, RW_HEADS, hd, hd), F32)] * 2,
        compiler_params=_params("arbitrary", "arbitrary"),
        name="rwkv_chunk_terms",
    )(r, lw, k, v, kk, bb)
    tok_b = pl.BlockSpec((nb, c, w), lambda bi, i: (bi, i, 0))
    mat_b = pl.BlockSpec((nb, 1, RW_HEADS, hd, hd), lambda bi, i: (bi, i, 0, 0, 0))
    st = pl.BlockSpec((nb, RW_HEADS, hd, hd), lambda bi, i: (bi, 0, 0, 0))
    return pl.pallas_call(
        functools.partial(_scan_inter_kernel, nb=nb),
        grid=(b // nb, nch),
        in_specs=[tok_b, tok_b, mat_b, mat_b, st],
        out_specs=[tok_b, st],
        out_shape=[jax.ShapeDtypeStruct((b, t, w), F32),
                   jax.ShapeDtypeStruct(m0.shape, F32)],
        scratch_shapes=[pltpu.VMEM((nb, RW_HEADS, hd, hd), F32)],
        compiler_params=_params("arbitrary", "arbitrary"),
        name="rwkv_scan",
    )(r2, oi, a_mat, d_mat, m0)


def _merge_kernel(x_ref, oa_ref, os_ref, bonus_ref, g_ref, gates_ref, gt_ref, sc_ref, sh_ref,
                  woa_ref, wob_ref, wout_ref, lng_ref, lnb_ref, gn2_ref, bd_ref,
                  x1_ref, h2_ref):
    bd = bd_ref[...]
    inv_n = 1.0 / RW_HEAD_DIM
    o = os_ref[0]
    oc = o - _hdot(o, bd) * inv_n
    var = _hdot(oc * oc, bd) * inv_n
    y = oc * lax.rsqrt(var + GN_EPS) * lng_ref[...] + lnb_ref[...]
    ob = (y + bonus_ref[0]) * g_ref[0]
    gates = gates_ref[0]
    d = x_ref.shape[-1]
    merged = (_sigmoid(gates[:, :d]) * _bdot(oa_ref[0], woa_ref[...])
              + _sigmoid(gates[:, d:]) * _bdot(ob, wob_ref[...]))
    x1 = x_ref[0] + gt_ref[0] * _bdot(merged, wout_ref[...])
    x1_ref[0] = x1
    h2 = (_rms(x1) * gn2_ref[...]) * (1.0 + sc_ref[0]) + sh_ref[0]
    h2_ref[0] = h2.astype(BF16)


def _merge_call(x, oa, o_scan, bonus, g, gates, gt, sc, sh, lp, tm):
    b, t, d = x.shape
    tok = lambda n: pl.BlockSpec((1, tm, n), lambda bi, i: (bi, i, 0))
    per_b = pl.BlockSpec((1, 1, d), lambda bi, i: (bi, 0, 0))
    full = lambda a: pl.BlockSpec(a.shape, lambda bi, i: (0,) * a.ndim)
    consts = [lp["w_oa"], lp["w_ob"], lp["w_out"], lp["lnx_g"], lp["lnx_b"], lp["g_norm2"],
              lp["bd"]]
    return pl.pallas_call(
        _merge_kernel,
        grid=(b, t // tm),
        in_specs=[tok(d), tok(DSA_Q), tok(RW_WIDTH), tok(RW_WIDTH), tok(RW_WIDTH), tok(2 * d),
                  per_b, per_b, per_b] + [full(a) for a in consts],
        out_specs=[tok(d), tok(d)],
        out_shape=[jax.ShapeDtypeStruct((b, t, d), F32), jax.ShapeDtypeStruct((b, t, d), BF16)],
        compiler_params=_params("arbitrary", "arbitrary"),
        name="merge_out",
    )(x, oa, o_scan, bonus, g, gates, gt, sc, sh, *consts)


def _kth_largest_rows(x, kth):
    work = x
    cnt = jnp.zeros((1, x.shape[1]), F32)
    tau = jnp.full((1, x.shape[1]), -jnp.inf, F32)
    tops = []
    for _ in range(kth):
        mx = jnp.max(work, axis=0, keepdims=True)
        eq = work == mx
        tau = jnp.where(cnt < kth, mx, tau)
        cnt = cnt + jnp.sum(jnp.where(eq, 1.0, 0.0), axis=0, keepdims=True)
        tops.append(mx)
        work = jnp.where(eq, -jnp.inf, work)
    return tau, tops


def _gelu(x):
    return 0.5 * x * (1.0 + lax.erf(x * (2.0 ** -0.5)))


def _peer_kernel(h2_ref, x1_ref, gt_ref, wq_ref, bq_ref, keys_ref, u0_ref, uy_ref, uxn_ref,
                 vt_ref, gf_ref, out_ref, s1_scr, s2_scr, e1_scr, e2_scr, tau_scr,
                 sx_scr, sy_scr, acc_scr, *, tn, eb, rep, final):
    j = pl.program_id(1)
    hb = h2_ref[...]

    @pl.when(j == 0)
    def _():
        sx_scr[...] = _dot_nt(u0_ref[...], hb)
        q = (jnp.dot(hb, wq_ref[...], preferred_element_type=F32) + bq_ref[...]).astype(BF16)
        for h in range(P_HEADS):
            halves = []
            for c in range(2):
                hc = 2 * h + c
                s = _dot_nt(keys_ref[hc], q[:, P_HALF * hc:P_HALF * (hc + 1)])
                tau_c, tops = _kth_largest_rows(s, P_TOPK)
                halves.append((s, tau_c, tops))
            (s1, t1, m1), (s2, t2, m2) = halves
            m1s = jnp.concatenate(m1, axis=0)
            m1lo = m1s[:P_TOPK // 2]
            cand = jnp.concatenate(
                [m1s + m2[0]] + [m1lo + m2[r2] for r2 in range(1, P_TOPK)], axis=0)
            tau, _ = _kth_largest_rows(cand, P_TOPK)
            z = jnp.sum(jnp.where(cand >= tau, jnp.exp(cand - (m1[0] + m2[0])), 0.0),
                        axis=0, keepdims=True)
            s1_scr[h] = jnp.where(s1 >= t1, s1, -jnp.inf)
            s2_scr[h] = jnp.where(s2 >= t2, s2, -jnp.inf)
            e1_scr[h] = jnp.exp(s1 - m1[0])
            e2_scr[h] = jnp.exp(s2 - m2[0]) / z
            tau_scr[h] = tau
        acc_scr[...] = jnp.zeros_like(acc_scr)

    n_i1 = eb // N_KEYS
    half = eb // 2

    def gate_and_project(sc_scr, blk, vt_col):
        for hf in range(2):
            row_blocks = []
            for c in range(half // N_KEYS):
                ci = hf * (half // N_KEYS) + c
                i1 = blk * n_i1 + ci
                s1_rows = [s1_scr[h, pl.ds(i1, 1), :] for h in range(P_HEADS)]
                e1_rows = [e1_scr[h, pl.ds(i1, 1), :] for h in range(P_HEADS)]
                col_blocks = []
                for tc in range(tn // LANES):
                    ln = slice(LANES * tc, LANES * (tc + 1))
                    gate = jnp.zeros((N_KEYS, LANES), F32)
                    for h in range(P_HEADS):
                        pair = s1_rows[h][:, ln] + s2_scr[h, :, ln]
                        gate = gate + jnp.where(pair >= tau_scr[h, :, ln],
                                                e1_rows[h][:, ln] * e2_scr[h, :, ln], 0.0)
                    act = _gelu(sc_scr[N_KEYS * ci:N_KEYS * (ci + 1), ln])
                    col_blocks.append((gate * act).astype(BF16))
                row_blocks.append(jnp.concatenate(col_blocks, axis=1))
            coef = jnp.concatenate(row_blocks, axis=0)
            lo = vt_col + half * hf
            acc_scr[...] += jnp.dot(vt_ref[:, lo:lo + half], coef, preferred_element_type=F32)

    sy_scr[...] = _dot_nt(uy_ref[...], hb)
    gate_and_project(sx_scr, 2 * j, 0)
    sx_scr[...] = _dot_nt(uxn_ref[...], hb)
    gate_and_project(sy_scr, 2 * j + 1, eb)

    @pl.when(j == pl.num_programs(1) - 1)
    def _():
        d = acc_scr.shape[0]
        gt = gt_ref[...]
        gt = jnp.broadcast_to(gt, (gt.shape[0], rep, d)).reshape(tn, d)
        x2 = x1_ref[...] + gt * acc_scr[...].T
        if final:
            x2 = _rms(x2) * gf_ref[...]
        out_ref[...] = x2


def _peer_call(h2, x1, gt, lp, g_final, *, tn, eb, final):
    b, t, d = x1.shape
    n = b * t
    assert n % tn == 0 and (t % tn == 0 or tn % t == 0)
    nbt = max(1, tn // t)
    tiles_per_b = max(1, t // tn)
    n_exp = lp["p_u"].shape[0]
    n_blk = n_exp // eb
    assert n_blk % 2 == 0
    kern = functools.partial(_peer_kernel, tn=tn, eb=eb, rep=tn // nbt, final=final)
    full = lambda a: pl.BlockSpec(a.shape, lambda ti, e: (0,) * a.ndim)
    tok = pl.BlockSpec((tn, d), lambda ti, e: (ti, 0))
    sel = pltpu.VMEM((P_HEADS, N_KEYS, tn), F32)
    blk_scores = pltpu.VMEM((eb, tn), F32)
    out = pl.pallas_call(
        kern,
        grid=(n // tn, n_blk // 2),
        in_specs=[tok, tok,
                  pl.BlockSpec((nbt, 1, d), lambda ti, e: (ti // tiles_per_b, 0, 0)),
                  full(lp["p_wq"]), full(lp["p_bq"]), full(lp["p_keys"]),
                  pl.BlockSpec((eb, d), lambda ti, e: (0, 0)),
                  pl.BlockSpec((eb, d), lambda ti, e: (2 * e + 1, 0)),
                  pl.BlockSpec((eb, d), lambda ti, e: (jnp.minimum(2 * e + 2, n_blk - 1), 0)),
                  pl.BlockSpec((d, 2 * eb), lambda ti, e: (0, e)),
                  full(g_final)],
        out_specs=tok,
        out_shape=jax.ShapeDtypeStruct((n, d), F32),
        scratch_shapes=[sel, sel, sel, sel, pltpu.VMEM((P_HEADS, 1, tn), F32),
                        blk_scores, blk_scores, pltpu.VMEM((d, tn), F32)],
        compiler_params=_params("arbitrary", "arbitrary"),
        name="peer",
    )(h2.reshape(n, d), x1.reshape(n, d), gt, lp["p_wq"], lp["p_bq"], lp["p_keys"],
      lp["p_u"], lp["p_u"], lp["p_u"], lp["p_vt"], g_final)
    return out.reshape(b, t, d)


def _layer(x, mod, lp, vfirst, past, q_offset, g_final, final, tiles):
    b, t, d = x.shape
    sh_t, sc_t, gt_t, sh_c, sc_c, gt_c = (m[:, None, :] for m in jnp.split(mod, 6, axis=-1))
    qi, k, v, ik, iw, rw, gates = _inproj_call(
        x, sc_t, sh_t, lp["g_norm1"], lp["w_in"], lp["idx_k_g"], lp["idx_k_b"], tiles["tm"])

    if past is None:
        k_all, v_all, ik_all = k, v, ik
        m0 = jnp.zeros((b, RW_HEADS, RW_HEAD_DIM, RW_HEAD_DIM), F32)
        shift0 = jnp.zeros((b, 1, RW_PAD), F32)
    else:
        k_past, v_past, ik_past, s0, rw_prev = past
        pl_ = k_past.shape[1]
        k_all = jnp.concatenate([k_past.reshape(b, pl_, KV_W), k], axis=1)
        v_all = jnp.concatenate([v_past.reshape(b, pl_, KV_W), v], axis=1)
        ik_all = jnp.concatenate([ik_past, ik], axis=1)
        m0 = jnp.swapaxes(s0, -1, -2)
        shift0 = jnp.pad(rw_prev, ((0, 0), (0, 0), (0, RW_PAD - RW_COLS)))
    tq = tiles["tq"]
    t_pad = -(-t // tq) * tq
    qpad = ((0, 0), (0, t_pad - t), (0, 0))
    o_a = _dsa_call(jnp.pad(qi, qpad), jnp.pad(iw, qpad), k_all, v_all, ik_all,
                    q_offset=q_offset, tq=tq, tk=tiles["tk"])[:, :t]

    r, lw, k2, v2, kkn, bb, g, bonus = _rwprep_call(rw, shift0, lp, vfirst, tiles["tm"])
    if vfirst is None:
        vfirst = v2
    o_scan, m_new = _scan_call(r, lw, k2, v2, kkn, bb, m0, tiles["c"])

    x1, h2 = _merge_call(x, o_a, o_scan, bonus, g, gates, gt_t, sc_c, sh_c, lp, tiles["tm"])
    x2 = _peer_call(h2, x1, gt_c, lp, g_final, tn=tiles["tn"], eb=tiles["eb"], final=final)
    state = (k.reshape(b, t, KV_HEADS, HEAD_DIM), v.reshape(b, t, KV_HEADS, HEAD_DIM), ik,
             jnp.swapaxes(m_new, -1, -2), rw[:, -1:, :RW_COLS])
    return x2, vfirst, state


def _tiles(t):
    return {"tm": min(t, 256), "tq": LANES, "tk": 256, "c": min(t, CHUNK),
            "tn": 512 if t >= 512 else LANES, "eb": 512}


def kernel(x_prompt, x_sample, cache_k, cache_v, cache_kidx, state_wkv, state_shift, c_prompt, c_sample, w_ada, b_ada, g_norm1, w_in, idx_k_g, idx_k_b, rw_mu, rw_w0, rw_w_up, rw_a0, rw_a_up, rw_g_up, rw_k_k, rw_k_a, rw_r_k, rw_lnx_g, rw_lnx_b, rw_v0, rw_v_down, rw_v_up, w_oa, w_ob, w_out, g_norm2, peer_wq, peer_bq, peer_sub_keys, peer_u, peer_v, g_final):
    depth = w_in.shape[0]
    nbp = x_prompt.shape[0]
    past_len = cache_k.shape[2]
    bd = _head_block_diag()
    row = lambda a: a.reshape(1, -1)
    xp, xs = x_prompt, x_sample
    vf_p, vf_s = None, None
    new_p, new_s = [], []
    c_all = jnp.concatenate([c_prompt, c_sample], axis=0)
    gf = row(g_final)
    for l in range(depth):
        lp = {
            "g_norm1": row(g_norm1[l]), "w_in": _pack_w_in(w_in[l]),
            "idx_k_g": row(idx_k_g[l]), "idx_k_b": row(idx_k_b[l]),
            "mu": jnp.pad(row(rw_mu[l]), ((0, 0), (0, RW_PAD - RW_COLS))),
            "w0": row(rw_w0[l]), "a0": row(rw_a0[l]),
            "w_up": _pad_rows(rw_w_up[l], 0, LANES),
            "a_up": _pad_rows(rw_a_up[l], W_LORA, LANES),
            "g_up": _pad_rows(rw_g_up[l], 0, RW_PAD - 3 * RW_WIDTH - LANES),
            "k_k": row(rw_k_k[l]), "k_a": row(rw_k_a[l]), "r_k": row(rw_r_k[l]),
            "lnx_g": row(rw_lnx_g[l]), "lnx_b": row(rw_lnx_b[l]), "bd": bd,
            "w_oa": w_oa[l].astype(BF16), "w_ob": w_ob[l].astype(BF16),
            "w_out": w_out[l].astype(BF16), "g_norm2": row(g_norm2[l]),
            "p_wq": peer_wq[l].astype(BF16), "p_bq": row(peer_bq[l]),
            "p_keys": peer_sub_keys[l].reshape(2 * P_HEADS, N_KEYS, P_HALF).astype(BF16),
            "p_u": peer_u[l].astype(BF16), "p_vt": peer_v[l].T.astype(BF16),
        }
        if l > 0:
            lp["v0"] = row(rw_v0[l - 1])
            lp["v_down"] = jnp.pad(rw_v_down[l - 1], ((0, 0), (0, LANES - V_LORA))).astype(BF16)
            lp["v_up"] = _pad_rows(rw_v_up[l - 1], 0, LANES)
        mod = _mod_call(c_all, w_ada[l], b_ada[l])
        final = l == depth - 1
        xp, vf_p, st_p = _layer(xp, mod[:nbp], lp, vf_p, None, 0, gf, final,
                                _tiles(xp.shape[1]))
        past = (cache_k[l], cache_v[l], cache_kidx[l], state_wkv[l], state_shift[l])
        xs, vf_s, st_s = _layer(xs, mod[nbp:], lp, vf_s, past, past_len, gf, final,
                                _tiles(xs.shape[1]))
        new_p.append(st_p)
        new_s.append(st_s)

    def stk(lst, i):
        return jnp.stack([e[i] for e in lst], axis=0)

    return (xp, xs,
            stk(new_p, 0), stk(new_p, 1), stk(new_p, 2), stk(new_p, 3), stk(new_p, 4),
            stk(new_s, 0), stk(new_s, 1), stk(new_s, 2), stk(new_s, 3), stk(new_s, 4))
```

```python
import functools

import numpy as np
import jax
import jax.numpy as jnp
from jax import lax
from jax.experimental import pallas as pl
from jax.experimental.pallas import tpu as pltpu

F32 = jnp.float32
BF16 = jnp.bfloat16
I32 = jnp.int32
HIGHEST = lax.Precision.HIGHEST

CHUNK = 64
N_HEADS = 8
HEAD_DIM = 64
KV_HEADS = 2
GROUP = N_HEADS // KV_HEADS
IDX_HEADS = 8
IDX_DIM = 64
DSA_TOPK = 256
ATTN_SCALE = HEAD_DIM ** -0.5
RW_HEADS = 8
RW_HEAD_DIM = 64
RW_WIDTH = RW_HEADS * RW_HEAD_DIM
W_LORA = 64
A_LORA = 64
V_LORA = 32
G_LORA = 160
RW_COLS = 3 * RW_WIDTH + W_LORA + A_LORA + G_LORA
N_KEYS = 128
P_HEADS = 8
P_HALF = 128
P_TOPK = 16
PEER_EB = 512
EPS = 1e-6
GN_EPS = 64e-5

LANES = 128
SUBLANES = 8
VMEM_LIMIT = 56 * 1024 * 1024

DSA_Q = N_HEADS * HEAD_DIM
IDX_Q = IDX_HEADS * IDX_DIM
KV_W = KV_HEADS * HEAD_DIM
QI_W = DSA_Q + IDX_Q
SMALL_W = 4 * LANES
RW_PAD = 15 * LANES
NEG = -1e30


def _params(*sem):
    return pltpu.CompilerParams(dimension_semantics=sem, vmem_limit_bytes=VMEM_LIMIT)


def _bdot(a, b):
    return jnp.dot(a.astype(BF16), b.astype(BF16), preferred_element_type=F32)


def _hdot(a, b):
    return jnp.dot(a, b, precision=HIGHEST, preferred_element_type=F32)


def _dot_nt(a, b, precision=None):
    return lax.dot_general(a, b, (((1,), (1,)), ((), ())), precision=precision,
                           preferred_element_type=F32)


def _dot_tn(a, b, precision=None):
    return lax.dot_general(a, b, (((0,), (0,)), ((), ())), precision=precision,
                           preferred_element_type=F32)


def _sigmoid(x):
    return 1.0 / (1.0 + jnp.exp(-x))


def _rms(x):
    return x * lax.rsqrt(jnp.mean(x * x, axis=-1, keepdims=True) + EPS)


def _mod_kernel(c_ref, w_ref, b_ref, o_ref):
    c = c_ref[...]
    o_ref[...] = _bdot(c * _sigmoid(c), w_ref[...]) + b_ref[...]


def _mod_call(c_all, w_ada, b_ada):
    nb, d = c_all.shape
    ncol = w_ada.shape[1] // d
    return pl.pallas_call(
        _mod_kernel,
        grid=(ncol,),
        in_specs=[pl.BlockSpec((nb, d), lambda j: (0, 0)),
                  pl.BlockSpec((d, d), lambda j: (0, j)),
                  pl.BlockSpec((1, d), lambda j: (0, j))],
        out_specs=pl.BlockSpec((nb, d), lambda j: (0, j)),
        out_shape=jax.ShapeDtypeStruct((nb, ncol * d), F32),
        compiler_params=_params("arbitrary"),
        name="adaln_mod",
    )(c_all, w_ada, b_ada.reshape(1, -1))


def _inproj_kernel(x_ref, sc_ref, sh_ref, g_ref, w_ref, ikg_ref, ikb_ref,
                   qi_ref, k_ref, v_ref, ik_ref, iw_ref, rw_ref, gates_ref):
    x = x_ref[0]
    h = (_rms(x) * g_ref[...]) * (1.0 + sc_ref[0]) + sh_ref[0]
    hb = h.astype(BF16)
    o0 = QI_W
    o1 = o0 + SMALL_W
    o2 = o1 + RW_PAD
    qi_ref[0] = jnp.dot(hb, w_ref[:, 0:o0], preferred_element_type=F32)
    small = jnp.dot(hb, w_ref[:, o0:o1], preferred_element_type=F32)
    k_ref[0] = small[:, 0:LANES]
    v_ref[0] = small[:, LANES:2 * LANES]
    ik = small[:, 2 * LANES:2 * LANES + IDX_DIM]
    ikc = ik - jnp.mean(ik, axis=-1, keepdims=True)
    ikn = ikc * lax.rsqrt(jnp.mean(ikc * ikc, axis=-1, keepdims=True) + EPS)
    ik_ref[0] = ikn * ikg_ref[...] + ikb_ref[...]
    iw_ref[0] = small[:, 3 * LANES:3 * LANES + IDX_HEADS]
    rw_ref[0] = jnp.dot(hb, w_ref[:, o1:o2], preferred_element_type=F32)
    gates_ref[0] = jnp.dot(hb, w_ref[:, o2:], preferred_element_type=F32)


def _pack_w_in(w_in):
    d = w_in.shape[0]
    offs = np.cumsum([0, DSA_Q, KV_W, KV_W, IDX_Q, IDX_DIM, IDX_HEADS, RW_COLS, 2 * d])
    q, k, v, iq, ik, iw, rw, gates = (w_in[:, offs[i]:offs[i + 1]] for i in range(8))
    z = lambda n: jnp.zeros((d, n), w_in.dtype)
    packed = jnp.concatenate(
        [q, iq, k, v, ik, z(LANES - IDX_DIM), iw, z(LANES - IDX_HEADS),
         rw, z(RW_PAD - RW_COLS), gates], axis=1)
    return packed.astype(BF16)


def _inproj_call(x, sc, sh, g1, w_packed, ikg, ikb, tm):
    b, t, d = x.shape
    nw = w_packed.shape[1]
    tok = lambda n: pl.BlockSpec((1, tm, n), lambda bi, i: (bi, i, 0))
    row = lambda n: pl.BlockSpec((1, n), lambda bi, i: (0, 0))
    per_b = pl.BlockSpec((1, 1, d), lambda bi, i: (bi, 0, 0))
    widths = (QI_W, LANES, LANES, IDX_DIM, IDX_HEADS, RW_PAD, 2 * d)
    return pl.pallas_call(
        _inproj_kernel,
        grid=(b, t // tm),
        in_specs=[tok(d), per_b, per_b, row(d),
                  pl.BlockSpec((d, nw), lambda bi, i: (0, 0)),
                  row(IDX_DIM), row(IDX_DIM)],
        out_specs=[tok(n) for n in widths],
        out_shape=[jax.ShapeDtypeStruct((b, t, n), F32) for n in widths],
        compiler_params=_params("arbitrary", "arbitrary"),
        name="norm_inproj",
    )(x, sc, sh, g1, w_packed, ikg, ikb)


def _dsa_kernel(qi_ref, iwt_ref, k_ref, vt_ref, ik_ref, o_ref,
                key_scr, bias_scr, iq_scr, qg_scr, *, tq, tk, l_valid, q_offset, topk):
    qb = pl.program_id(1)
    int_min = jnp.int32(-2 ** 31)
    q0 = q_offset + qb * tq
    last_chunk = (q0 + tq - 1) // CHUNK
    n_adm = jnp.minimum((last_chunk + 1) * CHUNK, l_valid)
    n_kt = (n_adm + tk - 1) // tk

    x = qi_ref[0]
    for h in range(IDX_HEADS):
        iq_scr[h] = x[:, DSA_Q + IDX_DIM * h:DSA_Q + IDX_DIM * (h + 1)].astype(BF16)
    for g in range(KV_HEADS):
        for r in range(GROUP):
            h = GROUP * g + r
            qg_scr[g, r * tq:(r + 1) * tq, :] = (
                x[:, HEAD_DIM * h:HEAD_DIM * (h + 1)] * ATTN_SCALE).astype(BF16)
    iwt = iwt_ref[0]
    q_chunk = (q0 + lax.broadcasted_iota(I32, (1, tq), 1)) // CHUNK
    row_iota = lax.broadcasted_iota(I32, (tk, tq), 0)

    def tile_base(kt):
        return pl.multiple_of(kt * tk, tk)

    def score_body(kt, carry):
        base = tile_base(kt)
        ikt = ik_ref[0, pl.ds(base, tk), :].astype(BF16)
        acc = jnp.zeros((tk, tq), F32)
        for h in range(IDX_HEADS):
            acc = acc + iwt[h:h + 1, :] * jnp.maximum(_dot_nt(ikt, iq_scr[h]), 0.0)
        acc = jnp.where(acc == 0.0, 0.0, acc)
        bits = lax.bitcast_convert_type(acc, I32)
        key = jnp.where(bits < 0, bits ^ jnp.int32(0x7FFFFFFF), bits)
        kpos = base + row_iota
        adm = (kpos < l_valid) & ((kpos // CHUNK) <= q_chunk)
        key_scr[pl.ds(base, tk), :] = jnp.where(adm, key, int_min)
        return carry

    lax.fori_loop(0, n_kt, score_body, 0)

    acc_rows = 4 * SUBLANES

    def count(pred_fn):
        def body(kt, c):
            base = tile_base(kt)
            m = jnp.where(pred_fn(key_scr[pl.ds(base, tk), :], base + row_iota), 1, 0)
            return c + jnp.sum(m.reshape(tk // acc_rows, acc_rows, tq), axis=0)
        c = lax.fori_loop(0, n_kt, body, jnp.zeros((acc_rows, tq), I32))
        return jnp.sum(c, axis=0, keepdims=True)

    def bit_body(i, tb):
        cand_b = tb | lax.shift_left(jnp.int32(1), 31 - i)
        cand = cand_b ^ int_min
        cnt = count(lambda kk, idx: kk >= cand)
        return jnp.where(cnt >= topk, cand_b, tb)

    tau = lax.fori_loop(0, 32, bit_body, jnp.zeros((1, tq), I32)) ^ int_min
    cnt_ge = count(lambda kk, idx: kk >= tau)
    cnt_gt = count(lambda kk, idx: kk > tau)
    need = topk - cnt_gt
    excess = (tau > int_min) & (cnt_ge - cnt_gt > need)
    any_excess = jnp.max(jnp.where(excess, 1, 0)) > 0

    idx_bits = 13
    def tie_limit():
        def jbody(i, j):
            cand_j = j | lax.shift_left(jnp.int32(1), idx_bits - 1 - i)
            f = count(lambda kk, idx: (kk == tau) & (idx < cand_j))
            return jnp.where(f <= need, cand_j, j)
        return lax.fori_loop(0, idx_bits, jbody, jnp.zeros((1, tq), I32))

    j_lim = lax.cond(any_excess, tie_limit,
                     lambda: jnp.full((1, tq), 2 ** idx_bits - 1, I32))

    def bias_body(kt, carry):
        base = tile_base(kt)
        kk = key_scr[pl.ds(base, tk), :]
        sel = (kk > tau) | ((kk == tau) & ((base + row_iota) < j_lim))
        sel = sel & (kk != int_min)
        bias_scr[pl.ds(base, tk), :] = jnp.where(sel, 0.0, NEG)
        return carry

    lax.fori_loop(0, n_kt, bias_body, 0)

    def attn_body(kt, carry):
        base = tile_base(kt)
        bias = bias_scr[pl.ds(base, tk), :]
        k_all = k_ref[0, pl.ds(base, tk), :]
        vt_all = vt_ref[0, kt]
        new = []
        s_groups = [_dot_nt(k_all[:, HEAD_DIM * g:HEAD_DIM * (g + 1)].astype(BF16), qg_scr[g])
                    for g in range(KV_HEADS)]
        for g in range(KV_HEADS):
            s_all = s_groups[g]
            ps, stats = [], []
            for r in range(GROUP):
                m, l, acc = carry[GROUP * g + r]
                s = s_all[:, r * tq:(r + 1) * tq] + bias
                m_new = jnp.maximum(m, jnp.max(s, axis=0, keepdims=True))
                alpha = jnp.exp(m - m_new)
                p = jnp.exp(s - m_new)
                ps.append(p.astype(BF16))
                stats.append((m_new, l * alpha + jnp.sum(p, axis=0, keepdims=True), alpha, acc))
            pv = jnp.dot(vt_all[HEAD_DIM * g:HEAD_DIM * (g + 1), :].astype(BF16),
                         jnp.concatenate(ps, axis=1), preferred_element_type=F32)
            for r, (m_new, l_new, alpha, acc) in enumerate(stats):
                new.append((m_new, l_new, acc * alpha + pv[:, r * tq:(r + 1) * tq]))
        return tuple(new)

    init = tuple((jnp.full((1, tq), NEG, F32), jnp.zeros((1, tq), F32),
                  jnp.zeros((HEAD_DIM, tq), F32)) for _ in range(N_HEADS))
    fin = lax.fori_loop(0, n_kt, attn_body, init)
    o_ref[0] = jnp.concatenate([acc / l for _, l, acc in fin], axis=0).T


def _dsa_call(qi, iw, k_all, v_all, ik_all, *, q_offset, tq, tk):
    b, t, _ = qi.shape
    l_valid = k_all.shape[1]
    topk = min(DSA_TOPK, l_valid // 4)
    assert topk <= tk and t % tq == 0
    l_pad = -(-l_valid // tk) * tk
    assert l_pad < 2 ** 13 - 1
    pad = ((0, 0), (0, l_pad - l_valid), (0, 0))
    k_p, v_p, ik_p = (jnp.pad(a, pad) for a in (k_all, v_all, ik_all))
    nkt = l_pad // tk
    vt = jnp.swapaxes(v_p.reshape(b, nkt, tk, KV_W), 2, 3)
    iwt = jnp.swapaxes(iw, 1, 2)
    kern = functools.partial(_dsa_kernel, tq=tq, tk=tk, l_valid=l_valid,
                             q_offset=q_offset, topk=topk)
    return pl.pallas_call(
        kern,
        grid=(b, t // tq),
        in_specs=[pl.BlockSpec((1, tq, QI_W), lambda bi, i: (bi, i, 0)),
                  pl.BlockSpec((1, IDX_HEADS, tq), lambda bi, i: (bi, 0, i)),
                  pl.BlockSpec((1, l_pad, KV_W), lambda bi, i: (bi, 0, 0)),
                  pl.BlockSpec((1, nkt, KV_W, tk), lambda bi, i: (bi, 0, 0, 0)),
                  pl.BlockSpec((1, l_pad, IDX_DIM), lambda bi, i: (bi, 0, 0))],
        out_specs=pl.BlockSpec((1, tq, DSA_Q), lambda bi, i: (bi, i, 0)),
        out_shape=jax.ShapeDtypeStruct((b, t, DSA_Q), F32),
        scratch_shapes=[pltpu.VMEM((l_pad, tq), I32), pltpu.VMEM((l_pad, tq), F32),
                        pltpu.VMEM((IDX_HEADS, tq, IDX_DIM), BF16),
                        pltpu.VMEM((KV_HEADS, GROUP * tq, HEAD_DIM), BF16)],
        compiler_params=_params("arbitrary", "arbitrary"),
        name="dsa_attention",
    )(qi, iwt, k_p, vt, ik_p)


def _rwprep_kernel(*refs, has_vfirst):
    (rw_ref, prev8_ref, shift0_ref, mu_ref, w0_ref, a0_ref, wup_ref, aup_ref, gup_ref,
     kk_ref, ka_ref, rk_ref, bd_ref) = refs[:13]
    if has_vfirst:
        vfirst_ref, v0_ref, vdown_ref, vup_ref = refs[13:17]
        outs = refs[17:]
    else:
        outs = refs[13:]
    r_o, lw_o, k_o, v_o, kkn_o, b_o, g_o, bonus_o = outs
    i = pl.program_id(1)
    rw = rw_ref[0]
    prev = jnp.where(i == 0, shift0_ref[0], prev8_ref[0][SUBLANES - 1:SUBLANES, :])
    row = lax.broadcasted_iota(I32, rw.shape, 0)
    shifted = jnp.where(row == 0, prev, pltpu.roll(rw, 1, 0))
    mix = rw + mu_ref[...] * (shifted - rw)
    w3 = RW_WIDTH
    r = mix[:, 0:w3]
    kr = mix[:, w3:2 * w3]
    vr = mix[:, 2 * w3:3 * w3]
    wa = mix[:, 3 * w3:3 * w3 + LANES]
    gd = mix[:, 3 * w3 + LANES:]
    z = w0_ref[...] + _bdot(jnp.tanh(wa), wup_ref[...])
    nz = -z
    softplus = jnp.maximum(nz, 0.0) + jnp.log(1.0 + jnp.exp(-jnp.abs(nz)))
    lw = -jnp.exp(-softplus - 0.5)
    a = _sigmoid(a0_ref[...] + _bdot(wa, aup_ref[...]))
    g = _bdot(_sigmoid(gd), gup_ref[...])
    if has_vfirst:
        lora = _bdot(_bdot(vr, vdown_ref[...]), vup_ref[...])
        vr = vr + (vfirst_ref[0] - vr) * _sigmoid(v0_ref[...] + lora)
    bd = bd_ref[...]
    kkr = kr * kk_ref[...]
    kkn = kkr / jnp.maximum(jnp.sqrt(_hdot(kkr * kkr, bd)), 1e-12)
    k2 = kr * (1.0 + (a - 1.0) * ka_ref[...])
    r_o[0] = r
    lw_o[0] = lw
    k_o[0] = k2
    v_o[0] = vr
    kkn_o[0] = kkn
    b_o[0] = kkn * a
    g_o[0] = g
    bonus_o[0] = _hdot(r * k2 * rk_ref[...], bd) * vr


def _head_block_diag():
    h = np.arange(RW_WIDTH) // RW_HEAD_DIM
    return jnp.asarray((h[:, None] == h[None, :]).astype(np.float32))


def _pad_rows(w, lo, total):
    return jnp.pad(w, ((lo, total - lo - w.shape[0]), (0, 0))).astype(BF16)


def _rwprep_call(rw, shift0, lp, vfirst, tm):
    b, t, _ = rw.shape
    has_vfirst = vfirst is not None
    tok = lambda n: pl.BlockSpec((1, tm, n), lambda bi, i: (bi, i, 0))
    full = lambda a: pl.BlockSpec(a.shape, lambda bi, i: (0,) * a.ndim)
    consts = [lp["mu"], lp["w0"], lp["a0"], lp["w_up"], lp["a_up"], lp["g_up"],
              lp["k_k"], lp["k_a"], lp["r_k"], lp["bd"]]
    args = [rw, rw, shift0] + consts
    in_specs = [tok(RW_PAD),
                pl.BlockSpec((1, SUBLANES, RW_PAD),
                             lambda bi, i: (bi, jnp.maximum(i * (tm // SUBLANES) - 1, 0), 0)),
                pl.BlockSpec((1, 1, RW_PAD), lambda bi, i: (bi, 0, 0))]
    in_specs += [full(a) for a in consts]
    if has_vfirst:
        extra = [lp["v0"], lp["v_down"], lp["v_up"]]
        args += [vfirst] + extra
        in_specs += [tok(RW_WIDTH)] + [full(a) for a in extra]
    return pl.pallas_call(
        functools.partial(_rwprep_kernel, has_vfirst=has_vfirst),
        grid=(b, t // tm),
        in_specs=in_specs,
        out_specs=[tok(RW_WIDTH)] * 8,
        out_shape=[jax.ShapeDtypeStruct((b, t, RW_WIDTH), F32)] * 8,
        compiler_params=_params("arbitrary", "arbitrary"),
        name="rwkv_prep",
    )(*args)


def _scan_intra_kernel(r_ref, lw_ref, k_ref, v_ref, kk_ref, b_ref,
                       r2_ref, oi_ref, a_ref, d_ref, *, c, nc):
    row = lax.broadcasted_iota(I32, (c, c), 0)
    col = lax.broadcasted_iota(I32, (c, c), 1)
    incl = row >= col
    strict = row > col
    eye_c = jnp.where(row == col, 1.0, 0.0)
    ones_incl = jnp.where(incl, 1.0, 0.0)
    n = RW_HEAD_DIM
    rn = lax.broadcasted_iota(I32, (n, n), 0)
    cn = lax.broadcasted_iota(I32, (n, n), 1)
    nlev = int(np.log2(c))

    prep = []
    for ci in range(nc):
        rows = slice(ci * c, (ci + 1) * c)
        lw, k, b = lw_ref[0, rows, :], k_ref[0, rows, :], b_ref[0, rows, :]
        cum = _hdot(ones_incl, lw)
        total = cum[c - 1:c, :]
        g_inv = jnp.exp(-cum)
        g_rem = jnp.exp(total - cum)
        prep.append(dict(
            alpha=kk_ref[0, rows, :] * jnp.exp(cum - lw), beta=b * g_inv, kappa=k * g_inv,
            rho=r_ref[0, rows, :] * jnp.exp(cum), khat=k * g_rem, bhat=b * g_rem,
            g_tot=jnp.exp(total), v=v_ref[0, rows, :]))
    units = [(ci, h) for ci in range(nc) for h in range(RW_HEADS)]

    def head(ci, h, name):
        return prep[ci][name][:, n * h:n * (h + 1)]

    grams = [_dot_nt(
        jnp.concatenate([head(ci, h, "alpha"), head(ci, h, "rho")], axis=0).astype(BF16),
        jnp.concatenate([head(ci, h, "beta"), head(ci, h, "kappa")], axis=0).astype(BF16))
        for ci, h in units]
    l_ak = [jnp.where(strict, g[:c, c:], 0.0) for g in grams]
    l_rb = [jnp.where(incl, g[c:, :c], 0.0) for g in grams]
    l_rk = [jnp.where(incl, g[c:, c:], 0.0) for g in grams]
    ps = [-jnp.where(strict, g[:c, :c], 0.0) for g in grams]
    tinvs = [eye_c + p for p in ps]
    for _ in range(nlev - 1):
        ps = [_bdot(p, p) for p in ps]
        tinvs = [t + _bdot(t, p) for t, p in zip(tinvs, ps)]
    lvs = [_bdot(jnp.concatenate([ak, rk], axis=0), head(ci, h, "v"))
           for ak, rk, (ci, h) in zip(l_ak, l_rk, units)]
    wys = [_hdot(t, jnp.concatenate([head(ci, h, "alpha"), lv[:c]], axis=1))
           for t, lv, (ci, h) in zip(tinvs, lvs, units)]
    rbs = [_bdot(rb, wy) for rb, wy in zip(l_rb, wys)]
    bws = [_dot_tn(head(ci, h, "bhat"), wy, HIGHEST) for wy, (ci, h) in zip(wys, units)]
    kvs = [_dot_tn(head(ci, h, "khat"), head(ci, h, "v"), HIGHEST) for ci, h in units]
    for ci in range(nc):
        rows = slice(ci * c, (ci + 1) * c)
        mine = [u for u, (cj, _) in enumerate(units) if cj == ci]
        r2_ref[0, rows, :] = jnp.concatenate(
            [head(ci, h, "rho") - rbs[u][:, :n] for h, u in enumerate(mine)], axis=1)
        oi_ref[0, rows, :] = jnp.concatenate(
            [lvs[u][c:] - rbs[u][:, n:] for u in mine], axis=1)
        for h, u in enumerate(mine):
            dg = jnp.where(rn == cn,
                           jnp.broadcast_to(prep[ci]["g_tot"][:, n * h:n * (h + 1)], (n, n)), 0.0)
            a_ref[0, ci, h] = dg - bws[u][:, :n]
            d_ref[0, ci, h] = kvs[u] - bws[u][:, n:]


def _scan_inter_kernel(r2_ref, oi_ref, a_ref, d_ref, m0_ref, o_ref, mout_ref, m_scr, *, nb):
    ci = pl.program_id(1)

    @pl.when(ci == 0)
    def _():
        m_scr[...] = m0_ref[...]

    n = RW_HEAD_DIM
    for bi in range(nb):
        r2 = r2_ref[bi]
        outs = []
        for h in range(RW_HEADS):
            m0 = m_scr[bi, h]
            outs.append(_hdot(r2[:, n * h:n * (h + 1)], m0))
            m_scr[bi, h] = _hdot(a_ref[bi, 0, h], m0) + d_ref[bi, 0, h]
        o_ref[bi] = jnp.concatenate(outs, axis=1) + oi_ref[bi]

    @pl.when(ci == pl.num_programs(1) - 1)
    def _():
        mout_ref[...] = m_scr[...]


def _scan_call(r, lw, k, v, kk, bb, m0, c):
    b, t, w = r.shape
    nch = t // c
    nc = 2 if nch % 2 == 0 else 1
    nb = next(n for n in (4, 2, 1) if b % n == 0)
    hd = RW_HEAD_DIM
    tok = pl.BlockSpec((1, nc * c, w), lambda bi, i: (bi, i, 0))
    mats = pl.BlockSpec((1, nc, RW_HEADS, hd, hd), lambda bi, i: (bi, i, 0, 0, 0))
    r2, oi, a_mat, d_mat = pl.pallas_call(
        functools.partial(_scan_intra_kernel, c=c, nc=nc),
        grid=(b, nch // nc),
        in_specs=[tok] * 6,
        out_specs=[tok, tok, mats, mats],
        out_shape=[jax.ShapeDtypeStruct((b, t, w), F32)] * 2
        + [jax.ShapeDtypeStruct((b, nch, RW_HEADS, hd, hd), F32)] * 2,
        compiler_params=_params("arbitrary", "arbitrary"),
        name="rwkv_chunk_terms",
    )(r, lw, k, v, kk, bb)
    tok_b = pl.BlockSpec((nb, c, w), lambda bi, i: (bi, i, 0))
    mat_b = pl.BlockSpec((nb, 1, RW_HEADS, hd, hd), lambda bi, i: (bi, i, 0, 0, 0))
    st = pl.BlockSpec((nb, RW_HEADS, hd, hd), lambda bi, i: (bi, 0, 0, 0))
    return pl.pallas_call(
        functools.partial(_scan_inter_kernel, nb=nb),
        grid=(b // nb, nch),
        in_specs=[tok_b, tok_b, mat_b, mat_b, st],
        out_specs=[tok_b, st],
        out_shape=[jax.ShapeDtypeStruct((b, t, w), F32),
                   jax.ShapeDtypeStruct(m0.shape, F32)],
        scratch_shapes=[pltpu.VMEM((nb, RW_HEADS, hd, hd), F32)],
        compiler_params=_params("arbitrary", "arbitrary"),
        name="rwkv_scan",
    )(r2, oi, a_mat, d_mat, m0)


def _merge_kernel(x_ref, oa_ref, os_ref, bonus_ref, g_ref, gates_ref, gt_ref, sc_ref, sh_ref,
                  woa_ref, wob_ref, wout_ref, lng_ref, lnb_ref, gn2_ref, bd_ref,
                  x1_ref, h2_ref):
    bd = bd_ref[...]
    inv_n = 1.0 / RW_HEAD_DIM
    o = os_ref[0]
    oc = o - _hdot(o, bd) * inv_n
    var = _hdot(oc * oc, bd) * inv_n
    y = oc * lax.rsqrt(var + GN_EPS) * lng_ref[...] + lnb_ref[...]
    ob = (y + bonus_ref[0]) * g_ref[0]
    gates = gates_ref[0]
    d = x_ref.shape[-1]
    merged = (_sigmoid(gates[:, :d]) * _bdot(oa_ref[0], woa_ref[...])
              + _sigmoid(gates[:, d:]) * _bdot(ob, wob_ref[...]))
    x1 = x_ref[0] + gt_ref[0] * _bdot(merged, wout_ref[...])
    x1_ref[0] = x1
    h2 = (_rms(x1) * gn2_ref[...]) * (1.0 + sc_ref[0]) + sh_ref[0]
    h2_ref[0] = h2.astype(BF16)


def _merge_call(x, oa, o_scan, bonus, g, gates, gt, sc, sh, lp, tm):
    b, t, d = x.shape
    tok = lambda n: pl.BlockSpec((1, tm, n), lambda bi, i: (bi, i, 0))
    per_b = pl.BlockSpec((1, 1, d), lambda bi, i: (bi, 0, 0))
    full = lambda a: pl.BlockSpec(a.shape, lambda bi, i: (0,) * a.ndim)
    consts = [lp["w_oa"], lp["w_ob"], lp["w_out"], lp["lnx_g"], lp["lnx_b"], lp["g_norm2"],
              lp["bd"]]
    return pl.pallas_call(
        _merge_kernel,
        grid=(b, t // tm),
        in_specs=[tok(d), tok(DSA_Q), tok(RW_WIDTH), tok(RW_WIDTH), tok(RW_WIDTH), tok(2 * d),
                  per_b, per_b, per_b] + [full(a) for a in consts],
        out_specs=[tok(d), tok(d)],
        out_shape=[jax.ShapeDtypeStruct((b, t, d), F32), jax.ShapeDtypeStruct((b, t, d), BF16)],
        compiler_params=_params("arbitrary", "arbitrary"),
        name="merge_out",
    )(x, oa, o_scan, bonus, g, gates, gt, sc, sh, *consts)


def _kth_largest_rows(x, kth):
    work = x
    cnt = jnp.zeros((1, x.shape[1]), F32)
    tau = jnp.full((1, x.shape[1]), -jnp.inf, F32)
    for _ in range(kth):
        mx = jnp.max(work, axis=0, keepdims=True)
        eq = work == mx
        tau = jnp.where(cnt < kth, mx, tau)
        cnt = cnt + jnp.sum(jnp.where(eq, 1.0, 0.0), axis=0, keepdims=True)
        work = jnp.where(eq, -jnp.inf, work)
    return tau


def _top_rows(x, kth):
    work = x
    tops = []
    for _ in range(kth):
        mx = jnp.max(work, axis=0, keepdims=True)
        tops.append(mx)
        work = jnp.where(work == mx, -jnp.inf, work)
    return tops


def _gelu(x):
    return 0.5 * x * (1.0 + lax.erf(x * (2.0 ** -0.5)))


def _peer_kernel(h2_ref, x1_ref, gt_ref, wq_ref, bq_ref, keys_ref, u0_ref, uy_ref, uxn_ref,
                 vt_ref, gf_ref, out_ref, s1_scr, s2_scr, e1_scr, e2_scr, tau_scr,
                 sx_scr, sy_scr, acc_scr, *, tn, eb, rep, final):
    j = pl.program_id(1)
    hb = h2_ref[...]

    @pl.when(j == 0)
    def _():
        sx_scr[...] = _dot_nt(u0_ref[...], hb)
        q = (jnp.dot(hb, wq_ref[...], preferred_element_type=F32) + bq_ref[...]).astype(BF16)
        for h in range(P_HEADS):
            halves = []
            for c in range(2):
                hc = 2 * h + c
                s = _dot_nt(keys_ref[hc], q[:, P_HALF * hc:P_HALF * (hc + 1)])
                halves.append((s, _top_rows(s, P_TOPK)))
            (s1, m1), (s2, m2) = halves
            m1s = jnp.concatenate(m1, axis=0)
            m2s = jnp.concatenate(m2, axis=0)
            hk = P_TOPK // 2
            m2lo = m2s[:hk]
            skip2 = lax.broadcasted_iota(I32, m2lo.shape, 0) < 2
            cand = jnp.concatenate(
                [m1s + m2[0], m1s[:hk] + m2[1], m2s[hk:] + m1[0]]
                + [jnp.where(skip2, -jnp.inf, m2lo + m1[r1]) for r1 in range(5)], axis=0)
            tau = _kth_largest_rows(cand, P_TOPK)
            z = jnp.sum(jnp.where(cand >= tau, jnp.exp(cand - (m1[0] + m2[0])), 0.0),
                        axis=0, keepdims=True)
            s1_scr[h] = jnp.where(s1 >= m1[P_TOPK - 1], s1, -jnp.inf)
            s2_scr[h] = jnp.where(s2 >= m2[P_TOPK - 1], s2, -jnp.inf)
            e1_scr[h] = jnp.exp(s1 - m1[0])
            e2_scr[h] = jnp.exp(s2 - m2[0]) / z
            tau_scr[h] = tau
        acc_scr[...] = jnp.zeros_like(acc_scr)

    n_i1 = eb // N_KEYS
    half = eb // 2

    def gate_and_project(sc_scr, blk, vt_col):
        for hf in range(2):
            row_blocks = []
            for c in range(half // N_KEYS):
                ci = hf * (half // N_KEYS) + c
                i1 = blk * n_i1 + ci
                s1_rows = [s1_scr[h, pl.ds(i1, 1), :] for h in range(P_HEADS)]
                e1_rows = [e1_scr[h, pl.ds(i1, 1), :] for h in range(P_HEADS)]
                col_blocks = []
                for tc in range(tn // LANES):
                    ln = slice(LANES * tc, LANES * (tc + 1))
                    gate = jnp.zeros((N_KEYS, LANES), F32)
                    for h in range(P_HEADS):
                        pair = s1_rows[h][:, ln] + s2_scr[h, :, ln]
                        gate = gate + jnp.where(pair >= tau_scr[h, :, ln],
                                                e1_rows[h][:, ln] * e2_scr[h, :, ln], 0.0)
                    act = _gelu(sc_scr[N_KEYS * ci:N_KEYS * (ci + 1), ln])
                    col_blocks.append((gate * act).astype(BF16))
                row_blocks.append(jnp.concatenate(col_blocks, axis=1))
            coef = jnp.concatenate(row_blocks, axis=0)
            lo = vt_col + half * hf
            acc_scr[...] += jnp.dot(vt_ref[0, :, lo:lo + half], coef,
                                    preferred_element_type=F32)

    sy_scr[...] = _dot_nt(uy_ref[...], hb)
    gate_and_project(sx_scr, 2 * j, 0)
    sx_scr[...] = _dot_nt(uxn_ref[...], hb)
    gate_and_project(sy_scr, 2 * j + 1, eb)

    @pl.when(j == pl.num_programs(1) - 1)
    def _():
        d = acc_scr.shape[0]
        gt = gt_ref[...]
        gt = jnp.broadcast_to(gt, (gt.shape[0], rep, d)).reshape(tn, d)
        x2 = x1_ref[...] + gt * acc_scr[...].T
        if final:
            x2 = _rms(x2) * gf_ref[...]
        out_ref[...] = x2


def _peer_call(h2, x1, gt, lp, g_final, *, tn, eb, final):
    b, t, d = x1.shape
    n = b * t
    assert n % tn == 0 and (t % tn == 0 or tn % t == 0)
    nbt = max(1, tn // t)
    tiles_per_b = max(1, t // tn)
    n_exp = lp["p_u"].shape[0]
    n_blk = n_exp // eb
    assert n_blk % 2 == 0
    kern = functools.partial(_peer_kernel, tn=tn, eb=eb, rep=tn // nbt, final=final)
    full = lambda a: pl.BlockSpec(a.shape, lambda ti, e: (0,) * a.ndim)
    tok = pl.BlockSpec((tn, d), lambda ti, e: (ti, 0))
    sel = pltpu.VMEM((P_HEADS, N_KEYS, tn), F32)
    blk_scores = pltpu.VMEM((eb, tn), F32)
    out = pl.pallas_call(
        kern,
        grid=(n // tn, n_blk // 2),
        in_specs=[tok, tok,
                  pl.BlockSpec((nbt, 1, d), lambda ti, e: (ti // tiles_per_b, 0, 0)),
                  full(lp["p_wq"]), full(lp["p_bq"]), full(lp["p_keys"]),
                  pl.BlockSpec((eb, d), lambda ti, e: (0, 0)),
                  pl.BlockSpec((eb, d), lambda ti, e: (2 * e + 1, 0)),
                  pl.BlockSpec((eb, d), lambda ti, e: (jnp.minimum(2 * e + 2, n_blk - 1), 0)),
                  pl.BlockSpec((1, d, 2 * eb), lambda ti, e: (e, 0, 0)),
                  full(g_final)],
        out_specs=tok,
        out_shape=jax.ShapeDtypeStruct((n, d), F32),
        scratch_shapes=[sel, sel, sel, sel, pltpu.VMEM((P_HEADS, 1, tn), F32),
                        blk_scores, blk_scores, pltpu.VMEM((d, tn), F32)],
        compiler_params=_params("arbitrary", "arbitrary"),
        name="peer",
    )(h2.reshape(n, d), x1.reshape(n, d), gt, lp["p_wq"], lp["p_bq"], lp["p_keys"],
      lp["p_u"], lp["p_u"], lp["p_u"], lp["p_vt"], g_final)
    return out.reshape(b, t, d)


def _layer(x, mod, lp, vfirst, past, q_offset, g_final, final, tiles):
    b, t, d = x.shape
    sh_t, sc_t, gt_t, sh_c, sc_c, gt_c = (m[:, None, :] for m in jnp.split(mod, 6, axis=-1))
    qi, k, v, ik, iw, rw, gates = _inproj_call(
        x, sc_t, sh_t, lp["g_norm1"], lp["w_in"], lp["idx_k_g"], lp["idx_k_b"], tiles["tm"])

    if past is None:
        k_all, v_all, ik_all = k, v, ik
        m0 = jnp.zeros((b, RW_HEADS, RW_HEAD_DIM, RW_HEAD_DIM), F32)
        shift0 = jnp.zeros((b, 1, RW_PAD), F32)
    else:
        k_past, v_past, ik_past, s0, rw_prev = past
        pl_ = k_past.shape[1]
        k_all = jnp.concatenate([k_past.reshape(b, pl_, KV_W), k], axis=1)
        v_all = jnp.concatenate([v_past.reshape(b, pl_, KV_W), v], axis=1)
        ik_all = jnp.concatenate([ik_past, ik], axis=1)
        m0 = jnp.swapaxes(s0, -1, -2)
        shift0 = jnp.pad(rw_prev, ((0, 0), (0, 0), (0, RW_PAD - RW_COLS)))
    tq = tiles["tq"]
    t_pad = -(-t // tq) * tq
    qpad = ((0, 0), (0, t_pad - t), (0, 0))
    o_a = _dsa_call(jnp.pad(qi, qpad), jnp.pad(iw, qpad), k_all, v_all, ik_all,
                    q_offset=q_offset, tq=tq, tk=tiles["tk"])[:, :t]

    r, lw, k2, v2, kkn, bb, g, bonus = _rwprep_call(rw, shift0, lp, vfirst, tiles["tm"])
    if vfirst is None:
        vfirst = v2
    o_scan, m_new = _scan_call(r, lw, k2, v2, kkn, bb, m0, tiles["c"])

    x1, h2 = _merge_call(x, o_a, o_scan, bonus, g, gates, gt_t, sc_c, sh_c, lp, tiles["tm"])
    x2 = _peer_call(h2, x1, gt_c, lp, g_final, tn=tiles["tn"], eb=tiles["eb"], final=final)
    state = (k.reshape(b, t, KV_HEADS, HEAD_DIM), v.reshape(b, t, KV_HEADS, HEAD_DIM), ik,
             jnp.swapaxes(m_new, -1, -2), rw[:, -1:, :RW_COLS])
    return x2, vfirst, state


def _tiles(t):
    return {"tm": min(t, 256), "tq": 2 * LANES if t % (2 * LANES) == 0 else LANES,
            "tk": 256, "c": min(t, CHUNK),
            "tn": 512 if t >= 512 else LANES, "eb": PEER_EB}


def kernel(x_prompt, x_sample, cache_k, cache_v, cache_kidx, state_wkv, state_shift, c_prompt, c_sample, w_ada, b_ada, g_norm1, w_in, idx_k_g, idx_k_b, rw_mu, rw_w0, rw_w_up, rw_a0, rw_a_up, rw_g_up, rw_k_k, rw_k_a, rw_r_k, rw_lnx_g, rw_lnx_b, rw_v0, rw_v_down, rw_v_up, w_oa, w_ob, w_out, g_norm2, peer_wq, peer_bq, peer_sub_keys, peer_u, peer_v, g_final):
    depth = w_in.shape[0]
    nbp = x_prompt.shape[0]
    past_len = cache_k.shape[2]
    bd = _head_block_diag()
    row = lambda a: a.reshape(1, -1)
    xp, xs = x_prompt, x_sample
    vf_p, vf_s = None, None
    new_p, new_s = [], []
    c_all = jnp.concatenate([c_prompt, c_sample], axis=0)
    gf = row(g_final)
    for l in range(depth):
        lp = {
            "g_norm1": row(g_norm1[l]), "w_in": _pack_w_in(w_in[l]),
            "idx_k_g": row(idx_k_g[l]), "idx_k_b": row(idx_k_b[l]),
            "mu": jnp.pad(row(rw_mu[l]), ((0, 0), (0, RW_PAD - RW_COLS))),
            "w0": row(rw_w0[l]), "a0": row(rw_a0[l]),
            "w_up": _pad_rows(rw_w_up[l], 0, LANES),
            "a_up": _pad_rows(rw_a_up[l], W_LORA, LANES),
            "g_up": _pad_rows(rw_g_up[l], 0, RW_PAD - 3 * RW_WIDTH - LANES),
            "k_k": row(rw_k_k[l]), "k_a": row(rw_k_a[l]), "r_k": row(rw_r_k[l]),
            "lnx_g": row(rw_lnx_g[l]), "lnx_b": row(rw_lnx_b[l]), "bd": bd,
            "w_oa": w_oa[l].astype(BF16), "w_ob": w_ob[l].astype(BF16),
            "w_out": w_out[l].astype(BF16), "g_norm2": row(g_norm2[l]),
            "p_wq": peer_wq[l].astype(BF16), "p_bq": row(peer_bq[l]),
            "p_keys": peer_sub_keys[l].reshape(2 * P_HEADS, N_KEYS, P_HALF).astype(BF16),
            "p_u": peer_u[l].astype(BF16),
            "p_vt": jnp.swapaxes(peer_v[l].reshape(-1, 2 * PEER_EB, peer_v.shape[-1]), 1, 2
                                 ).astype(BF16),
        }
        if l > 0:
            lp["v0"] = row(rw_v0[l - 1])
            lp["v_down"] = jnp.pad(rw_v_down[l - 1], ((0, 0), (0, LANES - V_LORA))).astype(BF16)
            lp["v_up"] = _pad_rows(rw_v_up[l - 1], 0, LANES)
        mod = _mod_call(c_all, w_ada[l], b_ada[l])
        final = l == depth - 1
        xp, vf_p, st_p = _layer(xp, mod[:nbp], lp, vf_p, None, 0, gf, final,
                                _tiles(xp.shape[1]))
        past = (cache_k[l], cache_v[l], cache_kidx[l], state_wkv[l], state_shift[l])
        xs, vf_s, st_s = _layer(xs, mod[nbp:], lp, vf_s, past, past_len, gf, final,
                                _tiles(xs.shape[1]))
        new_p.append(st_p)
        new_s.append(st_s)

    def stk(lst, i):
        return jnp.stack([e[i] for e in lst], axis=0)

    return (xp, xs,
            stk(new_p, 0), stk(new_p, 1), stk(new_p, 2), stk(new_p, 3), stk(new_p, 4),
            stk(new_s, 0), stk(new_s, 1), stk(new_s, 2), stk(new_s, 3), stk(new_s, 4))
```

```python
import functools

import numpy as np
import jax
import jax.numpy as jnp
from jax import lax
from jax.experimental import pallas as pl
from jax.experimental.pallas import tpu as pltpu

F32 = jnp.float32
BF16 = jnp.bfloat16
I32 = jnp.int32
HIGHEST = lax.Precision.HIGHEST

CHUNK = 64
N_HEADS = 8
HEAD_DIM = 64
KV_HEADS = 2
GROUP = N_HEADS // KV_HEADS
IDX_HEADS = 8
IDX_DIM = 64
DSA_TOPK = 256
ATTN_SCALE = HEAD_DIM ** -0.5
RW_HEADS = 8
RW_HEAD_DIM = 64
RW_WIDTH = RW_HEADS * RW_HEAD_DIM
W_LORA = 64
A_LORA = 64
V_LORA = 32
G_LORA = 160
RW_COLS = 3 * RW_WIDTH + W_LORA + A_LORA + G_LORA
N_KEYS = 128
P_HEADS = 8
P_HALF = 128
P_TOPK = 16
PEER_EB = 512
EPS = 1e-6
GN_EPS = 64e-5

LANES = 128
SUBLANES = 8
VMEM_LIMIT = 56 * 1024 * 1024

DSA_Q = N_HEADS * HEAD_DIM
IDX_Q = IDX_HEADS * IDX_DIM
KV_W = KV_HEADS * HEAD_DIM
QI_W = DSA_Q + IDX_Q
SMALL_W = 4 * LANES
RW_PAD = 15 * LANES
NEG = -1e30


def _params(*sem):
    return pltpu.CompilerParams(dimension_semantics=sem, vmem_limit_bytes=VMEM_LIMIT)


def _bdot(a, b):
    return jnp.dot(a.astype(BF16), b.astype(BF16), preferred_element_type=F32)


def _hdot(a, b):
    return jnp.dot(a, b, precision=HIGHEST, preferred_element_type=F32)


def _dot_nt(a, b, precision=None):
    return lax.dot_general(a, b, (((1,), (1,)), ((), ())), precision=precision,
                           preferred_element_type=F32)


def _dot_tn(a, b, precision=None):
    return lax.dot_general(a, b, (((0,), (0,)), ((), ())), precision=precision,
                           preferred_element_type=F32)


def _sigmoid(x):
    return 1.0 / (1.0 + jnp.exp(-x))


def _rms(x):
    return x * lax.rsqrt(jnp.mean(x * x, axis=-1, keepdims=True) + EPS)


def _mod_kernel(c_ref, w_ref, b_ref, o_ref):
    c = c_ref[...]
    o_ref[...] = _bdot(c * _sigmoid(c), w_ref[...]) + b_ref[...]


def _mod_call(c_all, w_ada, b_ada):
    nb, d = c_all.shape
    ncol = w_ada.shape[1] // d
    return pl.pallas_call(
        _mod_kernel,
        grid=(ncol,),
        in_specs=[pl.BlockSpec((nb, d), lambda j: (0, 0)),
                  pl.BlockSpec((d, d), lambda j: (0, j)),
                  pl.BlockSpec((1, d), lambda j: (0, j))],
        out_specs=pl.BlockSpec((nb, d), lambda j: (0, j)),
        out_shape=jax.ShapeDtypeStruct((nb, ncol * d), F32),
        compiler_params=_params("arbitrary"),
        name="adaln_mod",
    )(c_all, w_ada, b_ada.reshape(1, -1))


def _inproj_kernel(x_ref, sc_ref, sh_ref, g_ref, w_ref, ikg_ref, ikb_ref,
                   qi_ref, k_ref, v_ref, ik_ref, iw_ref, rw_ref, gates_ref):
    x = x_ref[0]
    h = (_rms(x) * g_ref[...]) * (1.0 + sc_ref[0]) + sh_ref[0]
    hb = h.astype(BF16)
    o0 = QI_W
    o1 = o0 + SMALL_W
    o2 = o1 + RW_PAD
    qi_ref[0] = jnp.dot(hb, w_ref[:, 0:o0], preferred_element_type=F32)
    small = jnp.dot(hb, w_ref[:, o0:o1], preferred_element_type=F32)
    k_ref[0] = small[:, 0:LANES]
    v_ref[0] = small[:, LANES:2 * LANES]
    ik = small[:, 2 * LANES:2 * LANES + IDX_DIM]
    ikc = ik - jnp.mean(ik, axis=-1, keepdims=True)
    ikn = ikc * lax.rsqrt(jnp.mean(ikc * ikc, axis=-1, keepdims=True) + EPS)
    ik_ref[0] = ikn * ikg_ref[...] + ikb_ref[...]
    iw_ref[0] = small[:, 3 * LANES:3 * LANES + IDX_HEADS]
    rw_ref[0] = jnp.dot(hb, w_ref[:, o1:o2], preferred_element_type=F32)
    gates_ref[0] = jnp.dot(hb, w_ref[:, o2:], preferred_element_type=F32)


def _pack_w_in(w_in):
    d = w_in.shape[0]
    offs = np.cumsum([0, DSA_Q, KV_W, KV_W, IDX_Q, IDX_DIM, IDX_HEADS, RW_COLS, 2 * d])
    q, k, v, iq, ik, iw, rw, gates = (w_in[:, offs[i]:offs[i + 1]] for i in range(8))
    z = lambda n: jnp.zeros((d, n), w_in.dtype)
    packed = jnp.concatenate(
        [q, iq, k, v, ik, z(LANES - IDX_DIM), iw, z(LANES - IDX_HEADS),
         rw, z(RW_PAD - RW_COLS), gates], axis=1)
    return packed.astype(BF16)


def _inproj_call(x, sc, sh, g1, w_packed, ikg, ikb, tm):
    b, t, d = x.shape
    nw = w_packed.shape[1]
    tok = lambda n: pl.BlockSpec((1, tm, n), lambda bi, i: (bi, i, 0))
    row = lambda n: pl.BlockSpec((1, n), lambda bi, i: (0, 0))
    per_b = pl.BlockSpec((1, 1, d), lambda bi, i: (bi, 0, 0))
    widths = (QI_W, LANES, LANES, IDX_DIM, IDX_HEADS, RW_PAD, 2 * d)
    return pl.pallas_call(
        _inproj_kernel,
        grid=(b, t // tm),
        in_specs=[tok(d), per_b, per_b, row(d),
                  pl.BlockSpec((d, nw), lambda bi, i: (0, 0)),
                  row(IDX_DIM), row(IDX_DIM)],
        out_specs=[tok(n) for n in widths],
        out_shape=[jax.ShapeDtypeStruct((b, t, n), F32) for n in widths],
        compiler_params=_params("arbitrary", "arbitrary"),
        name="norm_inproj",
    )(x, sc, sh, g1, w_packed, ikg, ikb)


def _dsa_kernel(qi_ref, iwt_ref, k_ref, vt_ref, ik_ref, o_ref,
                key_scr, bias_scr, iq_scr, qg_scr, *, tq, tk, l_valid, q_offset, topk):
    qb = pl.program_id(1)
    int_min = jnp.int32(-2 ** 31)
    q0 = q_offset + qb * tq
    last_chunk = (q0 + tq - 1) // CHUNK
    n_adm = jnp.minimum((last_chunk + 1) * CHUNK, l_valid)
    n_kt = (n_adm + tk - 1) // tk

    x = qi_ref[0]
    for h in range(IDX_HEADS):
        iq_scr[h] = x[:, DSA_Q + IDX_DIM * h:DSA_Q + IDX_DIM * (h + 1)].astype(BF16)
    for g in range(KV_HEADS):
        for r in range(GROUP):
            h = GROUP * g + r
            qg_scr[g, r * tq:(r + 1) * tq, :] = (
                x[:, HEAD_DIM * h:HEAD_DIM * (h + 1)] * ATTN_SCALE).astype(BF16)
    iwt = iwt_ref[0]
    q_chunk = (q0 + lax.broadcasted_iota(I32, (1, tq), 1)) // CHUNK
    row_iota = lax.broadcasted_iota(I32, (tk, tq), 0)

    def tile_base(kt):
        return pl.multiple_of(kt * tk, tk)

    def score_body(kt, carry):
        base = tile_base(kt)
        ikt = ik_ref[0, pl.ds(base, tk), :].astype(BF16)
        acc = jnp.zeros((tk, tq), F32)
        for h in range(IDX_HEADS):
            acc = acc + iwt[h:h + 1, :] * jnp.maximum(_dot_nt(ikt, iq_scr[h]), 0.0)
        acc = jnp.where(acc == 0.0, 0.0, acc)
        bits = lax.bitcast_convert_type(acc, I32)
        key = jnp.where(bits < 0, bits ^ jnp.int32(0x7FFFFFFF), bits)
        kpos = base + row_iota
        adm = (kpos < l_valid) & ((kpos // CHUNK) <= q_chunk)
        key_scr[pl.ds(base, tk), :] = jnp.where(adm, key, int_min)
        return carry

    lax.fori_loop(0, n_kt, score_body, 0)

    acc_rows = 4 * SUBLANES

    def count(pred_fn):
        def body(kt, c):
            base = tile_base(kt)
            m = jnp.where(pred_fn(key_scr[pl.ds(base, tk), :], base + row_iota), 1, 0)
            return c + jnp.sum(m.reshape(tk // acc_rows, acc_rows, tq), axis=0)
        c = lax.fori_loop(0, n_kt // 2, lambda i, c: body(2 * i + 1, body(2 * i, c)),
                          jnp.zeros((acc_rows, tq), I32))
        c = lax.fori_loop(2 * (n_kt // 2), n_kt, body, c)
        return jnp.sum(c, axis=0, keepdims=True)

    def bit_body(i, tb):
        cand_b = tb | lax.shift_left(jnp.int32(1), 31 - i)
        cand = cand_b ^ int_min
        cnt = count(lambda kk, idx: kk >= cand)
        return jnp.where(cnt >= topk, cand_b, tb)

    tau = lax.fori_loop(0, 32, bit_body, jnp.zeros((1, tq), I32)) ^ int_min
    cnt_ge = count(lambda kk, idx: kk >= tau)
    cnt_gt = count(lambda kk, idx: kk > tau)
    need = topk - cnt_gt
    excess = (tau > int_min) & (cnt_ge - cnt_gt > need)
    any_excess = jnp.max(jnp.where(excess, 1, 0)) > 0

    idx_bits = 13
    def tie_limit():
        def jbody(i, j):
            cand_j = j | lax.shift_left(jnp.int32(1), idx_bits - 1 - i)
            f = count(lambda kk, idx: (kk == tau) & (idx < cand_j))
            return jnp.where(f <= need, cand_j, j)
        return lax.fori_loop(0, idx_bits, jbody, jnp.zeros((1, tq), I32))

    j_lim = lax.cond(any_excess, tie_limit,
                     lambda: jnp.full((1, tq), 2 ** idx_bits - 1, I32))

    def bias_body(kt, carry):
        base = tile_base(kt)
        kk = key_scr[pl.ds(base, tk), :]
        sel = (kk > tau) | ((kk == tau) & ((base + row_iota) < j_lim))
        sel = sel & (kk != int_min)
        bias_scr[pl.ds(base, tk), :] = jnp.where(sel, 0.0, NEG)
        return carry

    lax.fori_loop(0, n_kt, bias_body, 0)

    def attn_body(kt, carry):
        base = tile_base(kt)
        bias = bias_scr[pl.ds(base, tk), :]
        k_all = k_ref[0, pl.ds(base, tk), :]
        vt_all = vt_ref[0, kt]
        new = []
        s_groups = [_dot_nt(k_all[:, HEAD_DIM * g:HEAD_DIM * (g + 1)].astype(BF16), qg_scr[g])
                    for g in range(KV_HEADS)]
        for g in range(KV_HEADS):
            s_all = s_groups[g]
            ps, stats = [], []
            for r in range(GROUP):
                m, l, acc = carry[GROUP * g + r]
                s = s_all[:, r * tq:(r + 1) * tq] + bias
                m_new = jnp.maximum(m, jnp.max(s, axis=0, keepdims=True))
                alpha = jnp.exp(m - m_new)
                p = jnp.exp(s - m_new)
                ps.append(p.astype(BF16))
                stats.append((m_new, l * alpha + jnp.sum(p, axis=0, keepdims=True), alpha, acc))
            pv = jnp.dot(vt_all[HEAD_DIM * g:HEAD_DIM * (g + 1), :].astype(BF16),
                         jnp.concatenate(ps, axis=1), preferred_element_type=F32)
            for r, (m_new, l_new, alpha, acc) in enumerate(stats):
                new.append((m_new, l_new, acc * alpha + pv[:, r * tq:(r + 1) * tq]))
        return tuple(new)

    init = tuple((jnp.full((1, tq), NEG, F32), jnp.zeros((1, tq), F32),
                  jnp.zeros((HEAD_DIM, tq), F32)) for _ in range(N_HEADS))
    fin = lax.fori_loop(0, n_kt, attn_body, init)
    o_ref[0] = jnp.concatenate([acc / l for _, l, acc in fin], axis=0).T


def _dsa_call(qi, iw, k_all, v_all, ik_all, *, q_offset, tq, tk):
    b, t, _ = qi.shape
    l_valid = k_all.shape[1]
    topk = min(DSA_TOPK, l_valid // 4)
    assert topk <= tk and t % tq == 0
    l_pad = -(-l_valid // tk) * tk
    assert l_pad < 2 ** 13 - 1
    pad = ((0, 0), (0, l_pad - l_valid), (0, 0))
    k_p, v_p, ik_p = (jnp.pad(a, pad) for a in (k_all, v_all, ik_all))
    nkt = l_pad // tk
    vt = jnp.swapaxes(v_p.reshape(b, nkt, tk, KV_W), 2, 3)
    iwt = jnp.swapaxes(iw, 1, 2)
    kern = functools.partial(_dsa_kernel, tq=tq, tk=tk, l_valid=l_valid,
                             q_offset=q_offset, topk=topk)
    return pl.pallas_call(
        kern,
        grid=(b, t // tq),
        in_specs=[pl.BlockSpec((1, tq, QI_W), lambda bi, i: (bi, i, 0)),
                  pl.BlockSpec((1, IDX_HEADS, tq), lambda bi, i: (bi, 0, i)),
                  pl.BlockSpec((1, l_pad, KV_W), lambda bi, i: (bi, 0, 0)),
                  pl.BlockSpec((1, nkt, KV_W, tk), lambda bi, i: (bi, 0, 0, 0)),
                  pl.BlockSpec((1, l_pad, IDX_DIM), lambda bi, i: (bi, 0, 0))],
        out_specs=pl.BlockSpec((1, tq, DSA_Q), lambda bi, i: (bi, i, 0)),
        out_shape=jax.ShapeDtypeStruct((b, t, DSA_Q), F32),
        scratch_shapes=[pltpu.VMEM((l_pad, tq), I32), pltpu.VMEM((l_pad, tq), F32),
                        pltpu.VMEM((IDX_HEADS, tq, IDX_DIM), BF16),
                        pltpu.VMEM((KV_HEADS, GROUP * tq, HEAD_DIM), BF16)],
        compiler_params=_params("arbitrary", "arbitrary"),
        name="dsa_attention",
    )(qi, iwt, k_p, vt, ik_p)


def _rwprep_kernel(*refs, has_vfirst):
    (rw_ref, prev8_ref, shift0_ref, mu_ref, w0_ref, a0_ref, wup_ref, aup_ref, gup_ref,
     kk_ref, ka_ref, rk_ref, bd_ref) = refs[:13]
    if has_vfirst:
        vfirst_ref, v0_ref, vdown_ref, vup_ref = refs[13:17]
        outs = refs[17:]
    else:
        outs = refs[13:]
    r_o, lw_o, k_o, v_o, kkn_o, b_o, g_o, bonus_o = outs
    i = pl.program_id(1)
    rw = rw_ref[0]
    prev = jnp.where(i == 0, shift0_ref[0], prev8_ref[0][SUBLANES - 1:SUBLANES, :])
    row = lax.broadcasted_iota(I32, rw.shape, 0)
    shifted = jnp.where(row == 0, prev, pltpu.roll(rw, 1, 0))
    mix = rw + mu_ref[...] * (shifted - rw)
    w3 = RW_WIDTH
    r = mix[:, 0:w3]
    kr = mix[:, w3:2 * w3]
    vr = mix[:, 2 * w3:3 * w3]
    wa = mix[:, 3 * w3:3 * w3 + LANES]
    gd = mix[:, 3 * w3 + LANES:]
    z = w0_ref[...] + _bdot(jnp.tanh(wa), wup_ref[...])
    nz = -z
    softplus = jnp.maximum(nz, 0.0) + jnp.log(1.0 + jnp.exp(-jnp.abs(nz)))
    lw = -jnp.exp(-softplus - 0.5)
    a = _sigmoid(a0_ref[...] + _bdot(wa, aup_ref[...]))
    g = _bdot(_sigmoid(gd), gup_ref[...])
    if has_vfirst:
        lora = _bdot(_bdot(vr, vdown_ref[...]), vup_ref[...])
        vr = vr + (vfirst_ref[0] - vr) * _sigmoid(v0_ref[...] + lora)
    bd = bd_ref[...]
    kkr = kr * kk_ref[...]
    kkn = kkr / jnp.maximum(jnp.sqrt(_hdot(kkr * kkr, bd)), 1e-12)
    k2 = kr * (1.0 + (a - 1.0) * ka_ref[...])
    r_o[0] = r
    lw_o[0] = lw
    k_o[0] = k2
    v_o[0] = vr
    kkn_o[0] = kkn
    b_o[0] = kkn * a
    g_o[0] = g
    bonus_o[0] = _hdot(r * k2 * rk_ref[...], bd) * vr


def _head_block_diag():
    h = np.arange(RW_WIDTH) // RW_HEAD_DIM
    return jnp.asarray((h[:, None] == h[None, :]).astype(np.float32))


def _pad_rows(w, lo, total):
    return jnp.pad(w, ((lo, total - lo - w.shape[0]), (0, 0))).astype(BF16)


def _rwprep_call(rw, shift0, lp, vfirst, tm):
    b, t, _ = rw.shape
    has_vfirst = vfirst is not None
    tok = lambda n: pl.BlockSpec((1, tm, n), lambda bi, i: (bi, i, 0))
    full = lambda a: pl.BlockSpec(a.shape, lambda bi, i: (0,) * a.ndim)
    consts = [lp["mu"], lp["w0"], lp["a0"], lp["w_up"], lp["a_up"], lp["g_up"],
              lp["k_k"], lp["k_a"], lp["r_k"], lp["bd"]]
    args = [rw, rw, shift0] + consts
    in_specs = [tok(RW_PAD),
                pl.BlockSpec((1, SUBLANES, RW_PAD),
                             lambda bi, i: (bi, jnp.maximum(i * (tm // SUBLANES) - 1, 0), 0)),
                pl.BlockSpec((1, 1, RW_PAD), lambda bi, i: (bi, 0, 0))]
    in_specs += [full(a) for a in consts]
    if has_vfirst:
        extra = [lp["v0"], lp["v_down"], lp["v_up"]]
        args += [vfirst] + extra
        in_specs += [tok(RW_WIDTH)] + [full(a) for a in extra]
    return pl.pallas_call(
        functools.partial(_rwprep_kernel, has_vfirst=has_vfirst),
        grid=(b, t // tm),
        in_specs=in_specs,
        out_specs=[tok(RW_WIDTH)] * 8,
        out_shape=[jax.ShapeDtypeStruct((b, t, RW_WIDTH), F32)] * 8,
        compiler_params=_params("arbitrary", "arbitrary"),
        name="rwkv_prep",
    )(*args)


def _scan_intra_kernel(r_ref, lw_ref, k_ref, v_ref, kk_ref, b_ref,
                       r2_ref, oi_ref, a_ref, d_ref, *, c, nc):
    row = lax.broadcasted_iota(I32, (c, c), 0)
    col = lax.broadcasted_iota(I32, (c, c), 1)
    incl = row >= col
    strict = row > col
    eye_c = jnp.where(row == col, 1.0, 0.0)
    ones_incl = jnp.where(incl, 1.0, 0.0)
    n = RW_HEAD_DIM
    rn = lax.broadcasted_iota(I32, (n, n), 0)
    cn = lax.broadcasted_iota(I32, (n, n), 1)
    nlev = int(np.log2(c))

    prep = []
    for ci in range(nc):
        rows = slice(ci * c, (ci + 1) * c)
        lw, k, b = lw_ref[0, rows, :], k_ref[0, rows, :], b_ref[0, rows, :]
        cum = _hdot(ones_incl, lw)
        total = cum[c - 1:c, :]
        g_inv = jnp.exp(-cum)
        g_rem = jnp.exp(total - cum)
        prep.append(dict(
            alpha=kk_ref[0, rows, :] * jnp.exp(cum - lw), beta=b * g_inv, kappa=k * g_inv,
            rho=r_ref[0, rows, :] * jnp.exp(cum), khat=k * g_rem, bhat=b * g_rem,
            g_tot=jnp.exp(total), v=v_ref[0, rows, :]))
    units = [(ci, h) for ci in range(nc) for h in range(RW_HEADS)]

    def head(ci, h, name):
        return prep[ci][name][:, n * h:n * (h + 1)]

    grams = [_dot_nt(
        jnp.concatenate([head(ci, h, "alpha"), head(ci, h, "rho")], axis=0).astype(BF16),
        jnp.concatenate([head(ci, h, "beta"), head(ci, h, "kappa")], axis=0).astype(BF16))
        for ci, h in units]
    l_ak = [jnp.where(strict, g[:c, c:], 0.0) for g in grams]
    l_rb = [jnp.where(incl, g[c:, :c], 0.0) for g in grams]
    l_rk = [jnp.where(incl, g[c:, c:], 0.0) for g in grams]
    ps = [-jnp.where(strict, g[:c, :c], 0.0) for g in grams]
    tinvs = [eye_c + p for p in ps]
    for _ in range(nlev - 1):
        ps = [_bdot(p, p) for p in ps]
        tinvs = [t + _bdot(t, p) for t, p in zip(tinvs, ps)]
    lvs = [_bdot(jnp.concatenate([ak, rk], axis=0), head(ci, h, "v"))
           for ak, rk, (ci, h) in zip(l_ak, l_rk, units)]
    wys = [_hdot(t, jnp.concatenate([head(ci, h, "alpha"), lv[:c]], axis=1))
           for t, lv, (ci, h) in zip(tinvs, lvs, units)]
    rbs = [_bdot(rb, wy) for rb, wy in zip(l_rb, wys)]
    bws = [_dot_tn(head(ci, h, "bhat"), wy, HIGHEST) for wy, (ci, h) in zip(wys, units)]
    kvs = [_dot_tn(head(ci, h, "khat"), head(ci, h, "v"), HIGHEST) for ci, h in units]
    for ci in range(nc):
        rows = slice(ci * c, (ci + 1) * c)
        mine = [u for u, (cj, _) in enumerate(units) if cj == ci]
        r2_ref[0, rows, :] = jnp.concatenate(
            [head(ci, h, "rho") - rbs[u][:, :n] for h, u in enumerate(mine)], axis=1)
        oi_ref[0, rows, :] = jnp.concatenate(
            [lvs[u][c:] - rbs[u][:, n:] for u in mine], axis=1)
        for h, u in enumerate(mine):
            dg = jnp.where(rn == cn,
                           jnp.broadcast_to(prep[ci]["g_tot"][:, n * h:n * (h + 1)], (n, n)), 0.0)
            a_ref[0, ci, h] = dg - bws[u][:, :n]
            d_ref[0, ci, h] = kvs[u] - bws[u][:, n:]


def _scan_inter_kernel(r2_ref, oi_ref, a_ref, d_ref, m0_ref, o_ref, mout_ref, m_scr, *, nb):
    ci = pl.program_id(1)

    @pl.when(ci == 0)
    def _():
        m_scr[...] = m0_ref[...]

    n = RW_HEAD_DIM
    for bi in range(nb):
        r2 = r2_ref[bi]
        outs = []
        for h in range(RW_HEADS):
            m0 = m_scr[bi, h]
            outs.append(_hdot(r2[:, n * h:n * (h + 1)], m0))
            m_scr[bi, h] = _hdot(a_ref[bi, 0, h], m0) + d_ref[bi, 0, h]
        o_ref[bi] = jnp.concatenate(outs, axis=1) + oi_ref[bi]

    @pl.when(ci == pl.num_programs(1) - 1)
    def _():
        mout_ref[...] = m_scr[...]


def _scan_call(r, lw, k, v, kk, bb, m0, c):
    b, t, w = r.shape
    nch = t // c
    nc = 2 if nch % 2 == 0 else 1
    nb = next(n for n in (4, 2, 1) if b % n == 0)
    hd = RW_HEAD_DIM
    tok = pl.BlockSpec((1, nc * c, w), lambda bi, i: (bi, i, 0))
    mats = pl.BlockSpec((1, nc, RW_HEADS, hd, hd), lambda bi, i: (bi, i, 0, 0, 0))
    r2, oi, a_mat, d_mat = pl.pallas_call(
        functools.partial(_scan_intra_kernel, c=c, nc=nc),
        grid=(b, nch // nc),
        in_specs=[tok] * 6,
        out_specs=[tok, tok, mats, mats],
        out_shape=[jax.ShapeDtypeStruct((b, t, w), F32)] * 2
        + [jax.ShapeDtypeStruct((b, nch, RW_HEADS, hd, hd), F32)] * 2,
        compiler_params=_params("arbitrary", "arbitrary"),
        name="rwkv_chunk_terms",
    )(r, lw, k, v, kk, bb)
    tok_b = pl.BlockSpec((nb, c, w), lambda bi, i: (bi, i, 0))
    mat_b = pl.BlockSpec((nb, 1, RW_HEADS, hd, hd), lambda bi, i: (bi, i, 0, 0, 0))
    st = pl.BlockSpec((nb, RW_HEADS, hd, hd), lambda bi, i: (bi, 0, 0, 0))
    return pl.pallas_call(
        functools.partial(_scan_inter_kernel, nb=nb),
        grid=(b // nb, nch),
        in_specs=[tok_b, tok_b, mat_b, mat_b, st],
        out_specs=[tok_b, st],
        out_shape=[jax.ShapeDtypeStruct((b, t, w), F32),
                   jax.ShapeDtypeStruct(m0.shape, F32)],
        scratch_shapes=[pltpu.VMEM((nb, RW_HEADS, hd, hd), F32)],
        compiler_params=_params("arbitrary", "arbitrary"),
        name="rwkv_scan",
    )(r2, oi, a_mat, d_mat, m0)


def _merge_kernel(x_ref, oa_ref, os_ref, bonus_ref, g_ref, gates_ref, gt_ref, sc_ref, sh_ref,
                  woa_ref, wob_ref, wout_ref, lng_ref, lnb_ref, gn2_ref, bd_ref,
                  x1_ref, h2_ref):
    bd = bd_ref[...]
    inv_n = 1.0 / RW_HEAD_DIM
    o = os_ref[0]
    oc = o - _hdot(o, bd) * inv_n
    var = _hdot(oc * oc, bd) * inv_n
    y = oc * lax.rsqrt(var + GN_EPS) * lng_ref[...] + lnb_ref[...]
    ob = (y + bonus_ref[0]) * g_ref[0]
    gates = gates_ref[0]
    d = x_ref.shape[-1]
    merged = (_sigmoid(gates[:, :d]) * _bdot(oa_ref[0], woa_ref[...])
              + _sigmoid(gates[:, d:]) * _bdot(ob, wob_ref[...]))
    x1 = x_ref[0] + gt_ref[0] * _bdot(merged, wout_ref[...])
    x1_ref[0] = x1
    h2 = (_rms(x1) * gn2_ref[...]) * (1.0 + sc_ref[0]) + sh_ref[0]
    h2_ref[0] = h2.astype(BF16)


def _merge_call(x, oa, o_scan, bonus, g, gates, gt, sc, sh, lp, tm):
    b, t, d = x.shape
    tok = lambda n: pl.BlockSpec((1, tm, n), lambda bi, i: (bi, i, 0))
    per_b = pl.BlockSpec((1, 1, d), lambda bi, i: (bi, 0, 0))
    full = lambda a: pl.BlockSpec(a.shape, lambda bi, i: (0,) * a.ndim)
    consts = [lp["w_oa"], lp["w_ob"], lp["w_out"], lp["lnx_g"], lp["lnx_b"], lp["g_norm2"],
              lp["bd"]]
    return pl.pallas_call(
        _merge_kernel,
        grid=(b, t // tm),
        in_specs=[tok(d), tok(DSA_Q), tok(RW_WIDTH), tok(RW_WIDTH), tok(RW_WIDTH), tok(2 * d),
                  per_b, per_b, per_b] + [full(a) for a in consts],
        out_specs=[tok(d), tok(d)],
        out_shape=[jax.ShapeDtypeStruct((b, t, d), F32), jax.ShapeDtypeStruct((b, t, d), BF16)],
        compiler_params=_params("arbitrary", "arbitrary"),
        name="merge_out",
    )(x, oa, o_scan, bonus, g, gates, gt, sc, sh, *consts)


def _kth_largest_rows(x, kth):
    work = x
    cnt = jnp.zeros((1, x.shape[1]), F32)
    tau = jnp.full((1, x.shape[1]), -jnp.inf, F32)
    for _ in range(kth):
        mx = jnp.max(work, axis=0, keepdims=True)
        eq = work == mx
        tau = jnp.where(cnt < kth, mx, tau)
        cnt = cnt + jnp.sum(jnp.where(eq, 1.0, 0.0), axis=0, keepdims=True)
        work = jnp.where(eq, -jnp.inf, work)
    return tau


def _top_rows(x, kth):
    work = x
    tops = []
    for _ in range(kth):
        mx = jnp.max(work, axis=0, keepdims=True)
        tops.append(mx)
        work = jnp.where(work == mx, -jnp.inf, work)
    return tops


def _gelu(x):
    return 0.5 * x * (1.0 + lax.erf(x * (2.0 ** -0.5)))


def _peer_kernel(h2_ref, x1_ref, gt_ref, wq_ref, bq_ref, keys_ref, u0_ref,
                 uy0_ref, uy1_ref, uxn0_ref, uxn1_ref, vt0_ref, vt1_ref, vt2_ref, vt3_ref,
                 gf_ref, out_ref, s1_scr, s2_scr, e1_scr, e2_scr, tau_scr,
                 sx_scr, sy_scr, acc_scr, *, tn, eb, rep, final):
    j = pl.program_id(1)
    hb = h2_ref[...]

    @pl.when(j == 0)
    def _():
        sx_scr[...] = _dot_nt(u0_ref[...], hb)
        q = (jnp.dot(hb, wq_ref[...], preferred_element_type=F32) + bq_ref[...]).astype(BF16)
        for h in range(P_HEADS):
            halves = []
            for c in range(2):
                hc = 2 * h + c
                s = _dot_nt(keys_ref[hc], q[:, P_HALF * hc:P_HALF * (hc + 1)])
                halves.append((s, _top_rows(s, P_TOPK)))
            (s1, m1), (s2, m2) = halves
            m1s = jnp.concatenate(m1, axis=0)
            m2s = jnp.concatenate(m2, axis=0)
            hk = P_TOPK // 2
            m2lo = m2s[:hk]
            skip2 = lax.broadcasted_iota(I32, m2lo.shape, 0) < 2
            cand = jnp.concatenate(
                [m1s + m2[0], m1s[:hk] + m2[1], m2s[hk:] + m1[0]]
                + [jnp.where(skip2, -jnp.inf, m2lo + m1[r1]) for r1 in range(5)], axis=0)
            tau = _kth_largest_rows(cand, P_TOPK)
            z = jnp.sum(jnp.where(cand >= tau, jnp.exp(cand - (m1[0] + m2[0])), 0.0),
                        axis=0, keepdims=True)
            s1_scr[h] = jnp.where(s1 >= m1[P_TOPK - 1], s1, -jnp.inf)
            s2_scr[h] = jnp.where(s2 >= m2[P_TOPK - 1], s2, -jnp.inf)
            e1_scr[h] = jnp.exp(s1 - m1[0])
            e2_scr[h] = jnp.exp(s2 - m2[0]) / z
            tau_scr[h] = tau
        acc_scr[...] = jnp.zeros_like(acc_scr)

    n_i1 = eb // N_KEYS
    half = eb // 2

    def gate_and_project(sc_scr, blk, vt_halves):
        for hf in range(2):
            row_blocks = []
            for c in range(half // N_KEYS):
                ci = hf * (half // N_KEYS) + c
                i1 = blk * n_i1 + ci
                s1_rows = [s1_scr[h, pl.ds(i1, 1), :] for h in range(P_HEADS)]
                e1_rows = [e1_scr[h, pl.ds(i1, 1), :] for h in range(P_HEADS)]
                col_blocks = []
                for tc in range(tn // LANES):
                    ln = slice(LANES * tc, LANES * (tc + 1))
                    gate = jnp.zeros((N_KEYS, LANES), F32)
                    for h in range(P_HEADS):
                        pair = s1_rows[h][:, ln] + s2_scr[h, :, ln]
                        gate = gate + jnp.where(pair >= tau_scr[h, :, ln],
                                                e1_rows[h][:, ln] * e2_scr[h, :, ln], 0.0)
                    act = _gelu(sc_scr[N_KEYS * ci:N_KEYS * (ci + 1), ln])
                    col_blocks.append((gate * act).astype(BF16))
                row_blocks.append(jnp.concatenate(col_blocks, axis=1))
            coef = jnp.concatenate(row_blocks, axis=0)
            acc_scr[...] += jnp.dot(vt_halves[hf][0], coef, preferred_element_type=F32)

    def block_scores(sc_scr, u_halves):
        for hf, u_half in enumerate(u_halves):
            sc_scr[half * hf:half * (hf + 1), :] = _dot_nt(u_half[...], hb)

    block_scores(sy_scr, (uy0_ref, uy1_ref))
    gate_and_project(sx_scr, 2 * j, (vt0_ref, vt1_ref))
    block_scores(sx_scr, (uxn0_ref, uxn1_ref))
    gate_and_project(sy_scr, 2 * j + 1, (vt2_ref, vt3_ref))

    @pl.when(j == pl.num_programs(1) - 1)
    def _():
        d = acc_scr.shape[0]
        gt = gt_ref[...]
        gt = jnp.broadcast_to(gt, (gt.shape[0], rep, d)).reshape(tn, d)
        x2 = x1_ref[...] + gt * acc_scr[...].T
        if final:
            x2 = _rms(x2) * gf_ref[...]
        out_ref[...] = x2


def _peer_call(h2, x1, gt, lp, g_final, *, tn, eb, final):
    b, t, d = x1.shape
    n = b * t
    assert n % tn == 0 and (t % tn == 0 or tn % t == 0)
    nbt = max(1, tn // t)
    tiles_per_b = max(1, t // tn)
    n_exp = lp["p_u"].shape[0]
    n_blk = n_exp // eb
    half = eb // 2
    assert n_blk % 2 == 0
    kern = functools.partial(_peer_kernel, tn=tn, eb=eb, rep=tn // nbt, final=final)
    full = lambda a: pl.BlockSpec(a.shape, lambda ti, e: (0,) * a.ndim)
    tok = pl.BlockSpec((tn, d), lambda ti, e: (ti, 0))
    sel = pltpu.VMEM((P_HEADS, N_KEYS, tn), F32)
    blk_scores = pltpu.VMEM((eb, tn), F32)
    out = pl.pallas_call(
        kern,
        grid=(n // tn, n_blk // 2),
        in_specs=[tok, tok,
                  pl.BlockSpec((nbt, 1, d), lambda ti, e: (ti // tiles_per_b, 0, 0)),
                  full(lp["p_wq"]), full(lp["p_bq"]), full(lp["p_keys"]),
                  pl.BlockSpec((eb, d), lambda ti, e: (0, 0))]
        + [pl.BlockSpec((half, d), lambda ti, e, i=i: (2 * (2 * e + 1) + i, 0))
           for i in range(2)]
        + [pl.BlockSpec((half, d),
                        lambda ti, e, i=i: (2 * jnp.minimum(2 * e + 2, n_blk - 1) + i, 0))
           for i in range(2)]
        + [pl.BlockSpec((1, d, half), lambda ti, e, i=i: (4 * e + i, 0, 0)) for i in range(4)]
        + [full(g_final)],
        out_specs=tok,
        out_shape=jax.ShapeDtypeStruct((n, d), F32),
        scratch_shapes=[sel, sel, sel, sel, pltpu.VMEM((P_HEADS, 1, tn), F32),
                        blk_scores, blk_scores, pltpu.VMEM((d, tn), F32)],
        compiler_params=_params("arbitrary", "arbitrary"),
        name="peer",
    )(h2.reshape(n, d), x1.reshape(n, d), gt, lp["p_wq"], lp["p_bq"], lp["p_keys"],
      *([lp["p_u"]] * 5), *([lp["p_vt"]] * 4), g_final)
    return out.reshape(b, t, d)


def _layer(x, mod, lp, vfirst, past, q_offset, g_final, final, tiles):
    b, t, d = x.shape
    sh_t, sc_t, gt_t, sh_c, sc_c, gt_c = (m[:, None, :] for m in jnp.split(mod, 6, axis=-1))
    qi, k, v, ik, iw, rw, gates = _inproj_call(
        x, sc_t, sh_t, lp["g_norm1"], lp["w_in"], lp["idx_k_g"], lp["idx_k_b"], tiles["tm"])

    if past is None:
        k_all, v_all, ik_all = k, v, ik
        m0 = jnp.zeros((b, RW_HEADS, RW_HEAD_DIM, RW_HEAD_DIM), F32)
        shift0 = jnp.zeros((b, 1, RW_PAD), F32)
    else:
        k_past, v_past, ik_past, s0, rw_prev = past
        pl_ = k_past.shape[1]
        k_all = jnp.concatenate([k_past.reshape(b, pl_, KV_W), k], axis=1)
        v_all = jnp.concatenate([v_past.reshape(b, pl_, KV_W), v], axis=1)
        ik_all = jnp.concatenate([ik_past, ik], axis=1)
        m0 = jnp.swapaxes(s0, -1, -2)
        shift0 = jnp.pad(rw_prev, ((0, 0), (0, 0), (0, RW_PAD - RW_COLS)))
    tq = tiles["tq"]
    t_pad = -(-t // tq) * tq
    qpad = ((0, 0), (0, t_pad - t), (0, 0))
    o_a = _dsa_call(jnp.pad(qi, qpad), jnp.pad(iw, qpad), k_all, v_all, ik_all,
                    q_offset=q_offset, tq=tq, tk=tiles["tk"])[:, :t]

    r, lw, k2, v2, kkn, bb, g, bonus = _rwprep_call(rw, shift0, lp, vfirst, tiles["tm"])
    if vfirst is None:
        vfirst = v2
    o_scan, m_new = _scan_call(r, lw, k2, v2, kkn, bb, m0, tiles["c"])

    x1, h2 = _merge_call(x, o_a, o_scan, bonus, g, gates, gt_t, sc_c, sh_c, lp, tiles["tm"])
    x2 = _peer_call(h2, x1, gt_c, lp, g_final, tn=tiles["tn"], eb=tiles["eb"], final=final)
    state = (k.reshape(b, t, KV_HEADS, HEAD_DIM), v.reshape(b, t, KV_HEADS, HEAD_DIM), ik,
             jnp.swapaxes(m_new, -1, -2), rw[:, -1:, :RW_COLS])
    return x2, vfirst, state


def _tiles(t):
    return {"tm": min(t, 256), "tq": 2 * LANES if t % (2 * LANES) == 0 else LANES,
            "tk": 256, "c": min(t, CHUNK),
            "tn": 512 if t >= 512 else LANES, "eb": PEER_EB}


def kernel(x_prompt, x_sample, cache_k, cache_v, cache_kidx, state_wkv, state_shift, c_prompt, c_sample, w_ada, b_ada, g_norm1, w_in, idx_k_g, idx_k_b, rw_mu, rw_w0, rw_w_up, rw_a0, rw_a_up, rw_g_up, rw_k_k, rw_k_a, rw_r_k, rw_lnx_g, rw_lnx_b, rw_v0, rw_v_down, rw_v_up, w_oa, w_ob, w_out, g_norm2, peer_wq, peer_bq, peer_sub_keys, peer_u, peer_v, g_final):
    depth = w_in.shape[0]
    nbp = x_prompt.shape[0]
    past_len = cache_k.shape[2]
    bd = _head_block_diag()
    row = lambda a: a.reshape(1, -1)
    xp, xs = x_prompt, x_sample
    vf_p, vf_s = None, None
    new_p, new_s = [], []
    c_all = jnp.concatenate([c_prompt, c_sample], axis=0)
    gf = row(g_final)
    for l in range(depth):
        lp = {
            "g_norm1": row(g_norm1[l]), "w_in": _pack_w_in(w_in[l]),
            "idx_k_g": row(idx_k_g[l]), "idx_k_b": row(idx_k_b[l]),
            "mu": jnp.pad(row(rw_mu[l]), ((0, 0), (0, RW_PAD - RW_COLS))),
            "w0": row(rw_w0[l]), "a0": row(rw_a0[l]),
            "w_up": _pad_rows(rw_w_up[l], 0, LANES),
            "a_up": _pad_rows(rw_a_up[l], W_LORA, LANES),
            "g_up": _pad_rows(rw_g_up[l], 0, RW_PAD - 3 * RW_WIDTH - LANES),
            "k_k": row(rw_k_k[l]), "k_a": row(rw_k_a[l]), "r_k": row(rw_r_k[l]),
            "lnx_g": row(rw_lnx_g[l]), "lnx_b": row(rw_lnx_b[l]), "bd": bd,
            "w_oa": w_oa[l].astype(BF16), "w_ob": w_ob[l].astype(BF16),
            "w_out": w_out[l].astype(BF16), "g_norm2": row(g_norm2[l]),
            "p_wq": peer_wq[l].astype(BF16), "p_bq": row(peer_bq[l]),
            "p_keys": peer_sub_keys[l].reshape(2 * P_HEADS, N_KEYS, P_HALF).astype(BF16),
            "p_u": peer_u[l].astype(BF16),
            "p_vt": jnp.swapaxes(peer_v[l].reshape(-1, PEER_EB // 2, peer_v.shape[-1]), 1, 2
                                 ).astype(BF16),
        }
        if l > 0:
            lp["v0"] = row(rw_v0[l - 1])
            lp["v_down"] = jnp.pad(rw_v_down[l - 1], ((0, 0), (0, LANES - V_LORA))).astype(BF16)
            lp["v_up"] = _pad_rows(rw_v_up[l - 1], 0, LANES)
        mod = _mod_call(c_all, w_ada[l], b_ada[l])
        final = l == depth - 1
        xp, vf_p, st_p = _layer(xp, mod[:nbp], lp, vf_p, None, 0, gf, final,
                                _tiles(xp.shape[1]))
        past = (cache_k[l], cache_v[l], cache_kidx[l], state_wkv[l], state_shift[l])
        xs, vf_s, st_s = _layer(xs, mod[nbp:], lp, vf_s, past, past_len, gf, final,
                                _tiles(xs.shape[1]))
        new_p.append(st_p)
        new_s.append(st_s)

    def stk(lst, i):
        return jnp.stack([e[i] for e in lst], axis=0)

    return (xp, xs,
            stk(new_p, 0), stk(new_p, 1), stk(new_p, 2), stk(new_p, 3), stk(new_p, 4),
            stk(new_s, 0), stk(new_s, 1), stk(new_s, 2), stk(new_s, 3), stk(new_s, 4))
```

```python
import functools

import numpy as np
import jax
import jax.numpy as jnp
from jax import lax
from jax.experimental import pallas as pl
from jax.experimental.pallas import tpu as pltpu

F32 = jnp.float32
BF16 = jnp.bfloat16
I32 = jnp.int32
HIGHEST = lax.Precision.HIGHEST

CHUNK = 64
N_HEADS = 8
HEAD_DIM = 64
KV_HEADS = 2
GROUP = N_HEADS // KV_HEADS
IDX_HEADS = 8
IDX_DIM = 64
DSA_TOPK = 256
ATTN_SCALE = HEAD_DIM ** -0.5
RW_HEADS = 8
RW_HEAD_DIM = 64
RW_WIDTH = RW_HEADS * RW_HEAD_DIM
W_LORA = 64
A_LORA = 64
V_LORA = 32
G_LORA = 160
RW_COLS = 3 * RW_WIDTH + W_LORA + A_LORA + G_LORA
N_KEYS = 128
P_HEADS = 8
P_HALF = 128
P_TOPK = 16
PEER_EB = 512
EPS = 1e-6
GN_EPS = 64e-5

LANES = 128
SUBLANES = 8
VMEM_LIMIT = 56 * 1024 * 1024

DSA_Q = N_HEADS * HEAD_DIM
IDX_Q = IDX_HEADS * IDX_DIM
KV_W = KV_HEADS * HEAD_DIM
QI_W = DSA_Q + IDX_Q
SMALL_W = 4 * LANES
RW_PAD = 15 * LANES
NEG = -1e30


def _params(*sem):
    return pltpu.CompilerParams(dimension_semantics=sem, vmem_limit_bytes=VMEM_LIMIT)


def _bdot(a, b):
    return jnp.dot(a.astype(BF16), b.astype(BF16), preferred_element_type=F32)


def _hdot(a, b):
    return jnp.dot(a, b, precision=HIGHEST, preferred_element_type=F32)


def _dot_nt(a, b, precision=None):
    return lax.dot_general(a, b, (((1,), (1,)), ((), ())), precision=precision,
                           preferred_element_type=F32)


def _dot_tn(a, b, precision=None):
    return lax.dot_general(a, b, (((0,), (0,)), ((), ())), precision=precision,
                           preferred_element_type=F32)


def _sigmoid(x):
    return 1.0 / (1.0 + jnp.exp(-x))


def _rms(x):
    return x * lax.rsqrt(jnp.mean(x * x, axis=-1, keepdims=True) + EPS)


def _mod_kernel(c_ref, w_ref, b_ref, o_ref):
    c = c_ref[...]
    o_ref[...] = _bdot(c * _sigmoid(c), w_ref[...]) + b_ref[...]


def _mod_call(c_all, w_ada, b_ada):
    nb, d = c_all.shape
    ncol = w_ada.shape[1] // d
    return pl.pallas_call(
        _mod_kernel,
        grid=(ncol,),
        in_specs=[pl.BlockSpec((nb, d), lambda j: (0, 0)),
                  pl.BlockSpec((d, d), lambda j: (0, j)),
                  pl.BlockSpec((1, d), lambda j: (0, j))],
        out_specs=pl.BlockSpec((nb, d), lambda j: (0, j)),
        out_shape=jax.ShapeDtypeStruct((nb, ncol * d), F32),
        compiler_params=_params("arbitrary"),
        name="adaln_mod",
    )(c_all, w_ada, b_ada.reshape(1, -1))


def _inproj_kernel(x_ref, sc_ref, sh_ref, g_ref, w_ref, ikg_ref, ikb_ref,
                   qi_ref, k_ref, v_ref, ik_ref, iw_ref, rw_ref, gates_ref):
    x = x_ref[0]
    h = (_rms(x) * g_ref[...]) * (1.0 + sc_ref[0]) + sh_ref[0]
    hb = h.astype(BF16)
    o0 = QI_W
    o1 = o0 + SMALL_W
    o2 = o1 + RW_PAD
    qi_ref[0] = jnp.dot(hb, w_ref[:, 0:o0], preferred_element_type=F32)
    small = jnp.dot(hb, w_ref[:, o0:o1], preferred_element_type=F32)
    k_ref[0] = small[:, 0:LANES]
    v_ref[0] = small[:, LANES:2 * LANES]
    ik = small[:, 2 * LANES:2 * LANES + IDX_DIM]
    ikc = ik - jnp.mean(ik, axis=-1, keepdims=True)
    ikn = ikc * lax.rsqrt(jnp.mean(ikc * ikc, axis=-1, keepdims=True) + EPS)
    ik_ref[0] = ikn * ikg_ref[...] + ikb_ref[...]
    iw_ref[0] = small[:, 3 * LANES:3 * LANES + IDX_HEADS]
    rw_ref[0] = jnp.dot(hb, w_ref[:, o1:o2], preferred_element_type=F32)
    gates_ref[0] = jnp.dot(hb, w_ref[:, o2:], preferred_element_type=F32)


def _pack_w_in(w_in):
    d = w_in.shape[0]
    offs = np.cumsum([0, DSA_Q, KV_W, KV_W, IDX_Q, IDX_DIM, IDX_HEADS, RW_COLS, 2 * d])
    q, k, v, iq, ik, iw, rw, gates = (w_in[:, offs[i]:offs[i + 1]] for i in range(8))
    z = lambda n: jnp.zeros((d, n), w_in.dtype)
    packed = jnp.concatenate(
        [q, iq, k, v, ik, z(LANES - IDX_DIM), iw, z(LANES - IDX_HEADS),
         rw, z(RW_PAD - RW_COLS), gates], axis=1)
    return packed.astype(BF16)


def _inproj_call(x, sc, sh, g1, w_packed, ikg, ikb, tm):
    b, t, d = x.shape
    nw = w_packed.shape[1]
    tok = lambda n: pl.BlockSpec((1, tm, n), lambda bi, i: (bi, i, 0))
    row = lambda n: pl.BlockSpec((1, n), lambda bi, i: (0, 0))
    per_b = pl.BlockSpec((1, 1, d), lambda bi, i: (bi, 0, 0))
    widths = (QI_W, LANES, LANES, IDX_DIM, IDX_HEADS, RW_PAD, 2 * d)
    return pl.pallas_call(
        _inproj_kernel,
        grid=(b, t // tm),
        in_specs=[tok(d), per_b, per_b, row(d),
                  pl.BlockSpec((d, nw), lambda bi, i: (0, 0)),
                  row(IDX_DIM), row(IDX_DIM)],
        out_specs=[tok(n) for n in widths],
        out_shape=[jax.ShapeDtypeStruct((b, t, n), F32) for n in widths],
        compiler_params=_params("arbitrary", "arbitrary"),
        name="norm_inproj",
    )(x, sc, sh, g1, w_packed, ikg, ikb)


def _dsa_kernel(qi_ref, iwt_ref, k_ref, vt_ref, ik_ref, o_ref,
                key_scr, bias_scr, iq_scr, qg_scr, *, tq, tk, l_valid, q_offset, topk):
    qb = pl.program_id(1)
    int_min = jnp.int32(-2 ** 31)
    q0 = q_offset + qb * tq
    last_chunk = (q0 + tq - 1) // CHUNK
    n_adm = jnp.minimum((last_chunk + 1) * CHUNK, l_valid)
    n_kt = (n_adm + tk - 1) // tk

    x = qi_ref[0]
    for h in range(IDX_HEADS):
        iq_scr[h] = x[:, DSA_Q + IDX_DIM * h:DSA_Q + IDX_DIM * (h + 1)].astype(BF16)
    for g in range(KV_HEADS):
        for r in range(GROUP):
            h = GROUP * g + r
            qg_scr[g, r * tq:(r + 1) * tq, :] = (
                x[:, HEAD_DIM * h:HEAD_DIM * (h + 1)] * ATTN_SCALE).astype(BF16)
    iwt = iwt_ref[0]
    q_chunk = (q0 + lax.broadcasted_iota(I32, (1, tq), 1)) // CHUNK
    row_iota = lax.broadcasted_iota(I32, (tk, tq), 0)

    def tile_base(kt):
        return pl.multiple_of(kt * tk, tk)

    def score_body(kt, carry):
        base = tile_base(kt)
        ikt = ik_ref[0, pl.ds(base, tk), :].astype(BF16)
        acc = jnp.zeros((tk, tq), F32)
        for h in range(IDX_HEADS):
            acc = acc + iwt[h:h + 1, :] * jnp.maximum(_dot_nt(ikt, iq_scr[h]), 0.0)
        acc = jnp.where(acc == 0.0, 0.0, acc)
        bits = lax.bitcast_convert_type(acc, I32)
        key = jnp.where(bits < 0, bits ^ jnp.int32(0x7FFFFFFF), bits)
        kpos = base + row_iota
        adm = (kpos < l_valid) & ((kpos // CHUNK) <= q_chunk)
        key_scr[pl.ds(base, tk), :] = jnp.where(adm, key, int_min)
        return carry

    lax.fori_loop(0, n_kt, score_body, 0)

    acc_rows = 4 * SUBLANES

    def count(pred_fn):
        def body(kt, c):
            base = tile_base(kt)
            m = jnp.where(pred_fn(key_scr[pl.ds(base, tk), :], base + row_iota), 1, 0)
            return c + jnp.sum(m.reshape(tk // acc_rows, acc_rows, tq), axis=0)
        c = lax.fori_loop(0, n_kt // 2, lambda i, c: body(2 * i + 1, body(2 * i, c)),
                          jnp.zeros((acc_rows, tq), I32))
        c = lax.fori_loop(2 * (n_kt // 2), n_kt, body, c)
        return jnp.sum(c, axis=0, keepdims=True)

    def bit_body(i, tb):
        cand_b = tb | lax.shift_left(jnp.int32(1), 31 - i)
        cand = cand_b ^ int_min
        cnt = count(lambda kk, idx: kk >= cand)
        return jnp.where(cnt >= topk, cand_b, tb)

    tau = lax.fori_loop(0, 32, bit_body, jnp.zeros((1, tq), I32)) ^ int_min
    cnt_ge = count(lambda kk, idx: kk >= tau)
    cnt_gt = count(lambda kk, idx: kk > tau)
    need = topk - cnt_gt
    excess = (tau > int_min) & (cnt_ge - cnt_gt > need)
    any_excess = jnp.max(jnp.where(excess, 1, 0)) > 0

    idx_bits = 13
    def tie_limit():
        def jbody(i, j):
            cand_j = j | lax.shift_left(jnp.int32(1), idx_bits - 1 - i)
            f = count(lambda kk, idx: (kk == tau) & (idx < cand_j))
            return jnp.where(f <= need, cand_j, j)
        return lax.fori_loop(0, idx_bits, jbody, jnp.zeros((1, tq), I32))

    j_lim = lax.cond(any_excess, tie_limit,
                     lambda: jnp.full((1, tq), 2 ** idx_bits - 1, I32))

    def bias_body(kt, carry):
        base = tile_base(kt)
        kk = key_scr[pl.ds(base, tk), :]
        sel = (kk > tau) | ((kk == tau) & ((base + row_iota) < j_lim))
        sel = sel & (kk != int_min)
        bias_scr[pl.ds(base, tk), :] = jnp.where(sel, 0.0, NEG)
        return carry

    lax.fori_loop(0, n_kt, bias_body, 0)

    def attn_body(kt, carry):
        base = tile_base(kt)
        bias = bias_scr[pl.ds(base, tk), :]
        k_all = k_ref[0, pl.ds(base, tk), :]
        vt_all = vt_ref[0, kt]
        new = []
        s_groups = [_dot_nt(k_all[:, HEAD_DIM * g:HEAD_DIM * (g + 1)].astype(BF16), qg_scr[g])
                    for g in range(KV_HEADS)]
        for g in range(KV_HEADS):
            s_all = s_groups[g]
            ps, stats = [], []
            for r in range(GROUP):
                m, l, acc = carry[GROUP * g + r]
                s = s_all[:, r * tq:(r + 1) * tq] + bias
                m_new = jnp.maximum(m, jnp.max(s, axis=0, keepdims=True))
                alpha = jnp.exp(m - m_new)
                p = jnp.exp(s - m_new)
                ps.append(p.astype(BF16))
                stats.append((m_new, l * alpha + jnp.sum(p, axis=0, keepdims=True), alpha, acc))
            pv = jnp.dot(vt_all[HEAD_DIM * g:HEAD_DIM * (g + 1), :].astype(BF16),
                         jnp.concatenate(ps, axis=1), preferred_element_type=F32)
            for r, (m_new, l_new, alpha, acc) in enumerate(stats):
                new.append((m_new, l_new, acc * alpha + pv[:, r * tq:(r + 1) * tq]))
        return tuple(new)

    init = tuple((jnp.full((1, tq), NEG, F32), jnp.zeros((1, tq), F32),
                  jnp.zeros((HEAD_DIM, tq), F32)) for _ in range(N_HEADS))
    fin = lax.fori_loop(0, n_kt, attn_body, init)
    o_ref[0] = jnp.concatenate([acc / l for _, l, acc in fin], axis=0).T


def _dsa_call(qi, iw, k_all, v_all, ik_all, *, q_offset, tq, tk):
    b, t, _ = qi.shape
    l_valid = k_all.shape[1]
    topk = min(DSA_TOPK, l_valid // 4)
    assert topk <= tk and t % tq == 0
    l_pad = -(-l_valid // tk) * tk
    assert l_pad < 2 ** 13 - 1
    pad = ((0, 0), (0, l_pad - l_valid), (0, 0))
    k_p, v_p, ik_p = (jnp.pad(a, pad) for a in (k_all, v_all, ik_all))
    nkt = l_pad // tk
    vt = jnp.swapaxes(v_p.reshape(b, nkt, tk, KV_W), 2, 3)
    iwt = jnp.swapaxes(iw, 1, 2)
    kern = functools.partial(_dsa_kernel, tq=tq, tk=tk, l_valid=l_valid,
                             q_offset=q_offset, topk=topk)
    return pl.pallas_call(
        kern,
        grid=(b, t // tq),
        in_specs=[pl.BlockSpec((1, tq, QI_W), lambda bi, i: (bi, i, 0)),
                  pl.BlockSpec((1, IDX_HEADS, tq), lambda bi, i: (bi, 0, i)),
                  pl.BlockSpec((1, l_pad, KV_W), lambda bi, i: (bi, 0, 0)),
                  pl.BlockSpec((1, nkt, KV_W, tk), lambda bi, i: (bi, 0, 0, 0)),
                  pl.BlockSpec((1, l_pad, IDX_DIM), lambda bi, i: (bi, 0, 0))],
        out_specs=pl.BlockSpec((1, tq, DSA_Q), lambda bi, i: (bi, i, 0)),
        out_shape=jax.ShapeDtypeStruct((b, t, DSA_Q), F32),
        scratch_shapes=[pltpu.VMEM((l_pad, tq), I32), pltpu.VMEM((l_pad, tq), F32),
                        pltpu.VMEM((IDX_HEADS, tq, IDX_DIM), BF16),
                        pltpu.VMEM((KV_HEADS, GROUP * tq, HEAD_DIM), BF16)],
        compiler_params=_params("arbitrary", "arbitrary"),
        name="dsa_attention",
    )(qi, iwt, k_p, vt, ik_p)


def _rwprep_kernel(*refs, has_vfirst):
    (rw_ref, prev8_ref, shift0_ref, mu_ref, w0_ref, a0_ref, wup_ref, aup_ref, gup_ref,
     kk_ref, ka_ref, rk_ref, bd_ref) = refs[:13]
    if has_vfirst:
        vfirst_ref, v0_ref, vdown_ref, vup_ref = refs[13:17]
        outs = refs[17:]
    else:
        outs = refs[13:]
    r_o, lw_o, k_o, v_o, kkn_o, b_o, g_o, bonus_o = outs
    i = pl.program_id(1)
    rw = rw_ref[0]
    prev = jnp.where(i == 0, shift0_ref[0], prev8_ref[0][SUBLANES - 1:SUBLANES, :])
    row = lax.broadcasted_iota(I32, rw.shape, 0)
    shifted = jnp.where(row == 0, prev, pltpu.roll(rw, 1, 0))
    mix = rw + mu_ref[...] * (shifted - rw)
    w3 = RW_WIDTH
    r = mix[:, 0:w3]
    kr = mix[:, w3:2 * w3]
    vr = mix[:, 2 * w3:3 * w3]
    wa = mix[:, 3 * w3:3 * w3 + LANES]
    gd = mix[:, 3 * w3 + LANES:]
    z = w0_ref[...] + _bdot(jnp.tanh(wa), wup_ref[...])
    nz = -z
    softplus = jnp.maximum(nz, 0.0) + jnp.log(1.0 + jnp.exp(-jnp.abs(nz)))
    lw = -jnp.exp(-softplus - 0.5)
    a = _sigmoid(a0_ref[...] + _bdot(wa, aup_ref[...]))
    g = _bdot(_sigmoid(gd), gup_ref[...])
    if has_vfirst:
        lora = _bdot(_bdot(vr, vdown_ref[...]), vup_ref[...])
        vr = vr + (vfirst_ref[0] - vr) * _sigmoid(v0_ref[...] + lora)
    bd = bd_ref[...]
    kkr = kr * kk_ref[...]
    kkn = kkr / jnp.maximum(jnp.sqrt(_hdot(kkr * kkr, bd)), 1e-12)
    k2 = kr * (1.0 + (a - 1.0) * ka_ref[...])
    r_o[0] = r
    lw_o[0] = lw
    k_o[0] = k2
    v_o[0] = vr
    kkn_o[0] = kkn
    b_o[0] = kkn * a
    g_o[0] = g
    bonus_o[0] = _hdot(r * k2 * rk_ref[...], bd) * vr


def _head_block_diag():
    h = np.arange(RW_WIDTH) // RW_HEAD_DIM
    return jnp.asarray((h[:, None] == h[None, :]).astype(np.float32))


def _pad_rows(w, lo, total):
    return jnp.pad(w, ((lo, total - lo - w.shape[0]), (0, 0))).astype(BF16)


def _rwprep_call(rw, shift0, lp, vfirst, tm):
    b, t, _ = rw.shape
    has_vfirst = vfirst is not None
    tok = lambda n: pl.BlockSpec((1, tm, n), lambda bi, i: (bi, i, 0))
    full = lambda a: pl.BlockSpec(a.shape, lambda bi, i: (0,) * a.ndim)
    consts = [lp["mu"], lp["w0"], lp["a0"], lp["w_up"], lp["a_up"], lp["g_up"],
              lp["k_k"], lp["k_a"], lp["r_k"], lp["bd"]]
    args = [rw, rw, shift0] + consts
    in_specs = [tok(RW_PAD),
                pl.BlockSpec((1, SUBLANES, RW_PAD),
                             lambda bi, i: (bi, jnp.maximum(i * (tm // SUBLANES) - 1, 0), 0)),
                pl.BlockSpec((1, 1, RW_PAD), lambda bi, i: (bi, 0, 0))]
    in_specs += [full(a) for a in consts]
    if has_vfirst:
        extra = [lp["v0"], lp["v_down"], lp["v_up"]]
        args += [vfirst] + extra
        in_specs += [tok(RW_WIDTH)] + [full(a) for a in extra]
    return pl.pallas_call(
        functools.partial(_rwprep_kernel, has_vfirst=has_vfirst),
        grid=(b, t // tm),
        in_specs=in_specs,
        out_specs=[tok(RW_WIDTH)] * 8,
        out_shape=[jax.ShapeDtypeStruct((b, t, RW_WIDTH), F32)] * 8,
        compiler_params=_params("arbitrary", "arbitrary"),
        name="rwkv_prep",
    )(*args)


def _scan_intra_kernel(r_ref, lw_ref, k_ref, v_ref, kk_ref, b_ref,
                       r2_ref, oi_ref, a_ref, d_ref, *, c, nc):
    row = lax.broadcasted_iota(I32, (c, c), 0)
    col = lax.broadcasted_iota(I32, (c, c), 1)
    incl = row >= col
    strict = row > col
    eye_c = jnp.where(row == col, 1.0, 0.0)
    ones_incl = jnp.where(incl, 1.0, 0.0)
    n = RW_HEAD_DIM
    rn = lax.broadcasted_iota(I32, (n, n), 0)
    cn = lax.broadcasted_iota(I32, (n, n), 1)
    nlev = int(np.log2(c))

    prep = []
    for ci in range(nc):
        rows = slice(ci * c, (ci + 1) * c)
        lw, k, b = lw_ref[0, rows, :], k_ref[0, rows, :], b_ref[0, rows, :]
        cum = _hdot(ones_incl, lw)
        total = cum[c - 1:c, :]
        g_inv = jnp.exp(-cum)
        g_rem = jnp.exp(total - cum)
        prep.append(dict(
            alpha=kk_ref[0, rows, :] * jnp.exp(cum - lw), beta=b * g_inv, kappa=k * g_inv,
            rho=r_ref[0, rows, :] * jnp.exp(cum), khat=k * g_rem, bhat=b * g_rem,
            g_tot=jnp.exp(total), v=v_ref[0, rows, :]))
    units = [(ci, h) for ci in range(nc) for h in range(RW_HEADS)]

    def head(ci, h, name):
        return prep[ci][name][:, n * h:n * (h + 1)]

    grams = [_dot_nt(
        jnp.concatenate([head(ci, h, "alpha"), head(ci, h, "rho")], axis=0).astype(BF16),
        jnp.concatenate([head(ci, h, "beta"), head(ci, h, "kappa")], axis=0).astype(BF16))
        for ci, h in units]
    l_ak = [jnp.where(strict, g[:c, c:], 0.0) for g in grams]
    l_rb = [jnp.where(incl, g[c:, :c], 0.0) for g in grams]
    l_rk = [jnp.where(incl, g[c:, c:], 0.0) for g in grams]
    ps = [-jnp.where(strict, g[:c, :c], 0.0) for g in grams]
    tinvs = [eye_c + p for p in ps]
    for _ in range(nlev - 1):
        ps = [_bdot(p, p) for p in ps]
        tinvs = [t + _bdot(t, p) for t, p in zip(tinvs, ps)]
    lvs = [_bdot(jnp.concatenate([ak, rk], axis=0), head(ci, h, "v"))
           for ak, rk, (ci, h) in zip(l_ak, l_rk, units)]
    wys = [_hdot(t, jnp.concatenate([head(ci, h, "alpha"), lv[:c]], axis=1))
           for t, lv, (ci, h) in zip(tinvs, lvs, units)]
    rbs = [_bdot(rb, wy) for rb, wy in zip(l_rb, wys)]
    bws = [_dot_tn(head(ci, h, "bhat"), wy, HIGHEST) for wy, (ci, h) in zip(wys, units)]
    kvs = [_dot_tn(head(ci, h, "khat"), head(ci, h, "v"), HIGHEST) for ci, h in units]
    for ci in range(nc):
        rows = slice(ci * c, (ci + 1) * c)
        mine = [u for u, (cj, _) in enumerate(units) if cj == ci]
        r2_ref[0, rows, :] = jnp.concatenate(
            [head(ci, h, "rho") - rbs[u][:, :n] for h, u in enumerate(mine)], axis=1)
        oi_ref[0, rows, :] = jnp.concatenate(
            [lvs[u][c:] - rbs[u][:, n:] for u in mine], axis=1)
        for h, u in enumerate(mine):
            dg = jnp.where(rn == cn,
                           jnp.broadcast_to(prep[ci]["g_tot"][:, n * h:n * (h + 1)], (n, n)), 0.0)
            a_ref[0, ci, h] = dg - bws[u][:, :n]
            d_ref[0, ci, h] = kvs[u] - bws[u][:, n:]


def _scan_inter_kernel(r2_ref, oi_ref, a_ref, d_ref, m0_ref, o_ref, mout_ref, m_scr, *, nb):
    ci = pl.program_id(1)

    @pl.when(ci == 0)
    def _():
        m_scr[...] = m0_ref[...]

    n = RW_HEAD_DIM
    for bi in range(nb):
        r2 = r2_ref[bi]
        outs = []
        for h in range(RW_HEADS):
            m0 = m_scr[bi, h]
            outs.append(_hdot(r2[:, n * h:n * (h + 1)], m0))
            m_scr[bi, h] = _hdot(a_ref[bi, 0, h], m0) + d_ref[bi, 0, h]
        o_ref[bi] = jnp.concatenate(outs, axis=1) + oi_ref[bi]

    @pl.when(ci == pl.num_programs(1) - 1)
    def _():
        mout_ref[...] = m_scr[...]


def _scan_call(r, lw, k, v, kk, bb, m0, c):
    b, t, w = r.shape
    nch = t // c
    nc = 2 if nch % 2 == 0 else 1
    nb = next(n for n in (4, 2, 1) if b % n == 0)
    hd = RW_HEAD_DIM
    tok = pl.BlockSpec((1, nc * c, w), lambda bi, i: (bi, i, 0))
    mats = pl.BlockSpec((1, nc, RW_HEADS, hd, hd), lambda bi, i: (bi, i, 0, 0, 0))
    r2, oi, a_mat, d_mat = pl.pallas_call(
        functools.partial(_scan_intra_kernel, c=c, nc=nc),
        grid=(b, nch // nc),
        in_specs=[tok] * 6,
        out_specs=[tok, tok, mats, mats],
        out_shape=[jax.ShapeDtypeStruct((b, t, w), F32)] * 2
        + [jax.ShapeDtypeStruct((b, nch, RW_HEADS, hd, hd), F32)] * 2,
        compiler_params=_params("arbitrary", "arbitrary"),
        name="rwkv_chunk_terms",
    )(r, lw, k, v, kk, bb)
    tok_b = pl.BlockSpec((nb, c, w), lambda bi, i: (bi, i, 0))
    mat_b = pl.BlockSpec((nb, 1, RW_HEADS, hd, hd), lambda bi, i: (bi, i, 0, 0, 0))
    st = pl.BlockSpec((nb, RW_HEADS, hd, hd), lambda bi, i: (bi, 0, 0, 0))
    return pl.pallas_call(
        functools.partial(_scan_inter_kernel, nb=nb),
        grid=(b // nb, nch),
        in_specs=[tok_b, tok_b, mat_b, mat_b, st],
        out_specs=[tok_b, st],
        out_shape=[jax.ShapeDtypeStruct((b, t, w), F32),
                   jax.ShapeDtypeStruct(m0.shape, F32)],
        scratch_shapes=[pltpu.VMEM((nb, RW_HEADS, hd, hd), F32)],
        compiler_params=_params("arbitrary", "arbitrary"),
        name="rwkv_scan",
    )(r2, oi, a_mat, d_mat, m0)


def _merge_kernel(x_ref, oa_ref, os_ref, bonus_ref, g_ref, gates_ref, gt_ref, sc_ref, sh_ref,
                  woa_ref, wob_ref, wout_ref, lng_ref, lnb_ref, gn2_ref, bd_ref,
                  x1_ref, h2_ref):
    bd = bd_ref[...]
    inv_n = 1.0 / RW_HEAD_DIM
    o = os_ref[0]
    oc = o - _hdot(o, bd) * inv_n
    var = _hdot(oc * oc, bd) * inv_n
    y = oc * lax.rsqrt(var + GN_EPS) * lng_ref[...] + lnb_ref[...]
    ob = (y + bonus_ref[0]) * g_ref[0]
    gates = gates_ref[0]
    d = x_ref.shape[-1]
    merged = (_sigmoid(gates[:, :d]) * _bdot(oa_ref[0], woa_ref[...])
              + _sigmoid(gates[:, d:]) * _bdot(ob, wob_ref[...]))
    x1 = x_ref[0] + gt_ref[0] * _bdot(merged, wout_ref[...])
    x1_ref[0] = x1
    h2 = (_rms(x1) * gn2_ref[...]) * (1.0 + sc_ref[0]) + sh_ref[0]
    h2_ref[0] = h2.astype(BF16)


def _merge_call(x, oa, o_scan, bonus, g, gates, gt, sc, sh, lp, tm):
    b, t, d = x.shape
    tok = lambda n: pl.BlockSpec((1, tm, n), lambda bi, i: (bi, i, 0))
    per_b = pl.BlockSpec((1, 1, d), lambda bi, i: (bi, 0, 0))
    full = lambda a: pl.BlockSpec(a.shape, lambda bi, i: (0,) * a.ndim)
    consts = [lp["w_oa"], lp["w_ob"], lp["w_out"], lp["lnx_g"], lp["lnx_b"], lp["g_norm2"],
              lp["bd"]]
    return pl.pallas_call(
        _merge_kernel,
        grid=(b, t // tm),
        in_specs=[tok(d), tok(DSA_Q), tok(RW_WIDTH), tok(RW_WIDTH), tok(RW_WIDTH), tok(2 * d),
                  per_b, per_b, per_b] + [full(a) for a in consts],
        out_specs=[tok(d), tok(d)],
        out_shape=[jax.ShapeDtypeStruct((b, t, d), F32), jax.ShapeDtypeStruct((b, t, d), BF16)],
        compiler_params=_params("arbitrary", "arbitrary"),
        name="merge_out",
    )(x, oa, o_scan, bonus, g, gates, gt, sc, sh, *consts)


def _kth_largest_rows(x, kth):
    work = x
    cnt = jnp.zeros((1, x.shape[1]), F32)
    tau = jnp.full((1, x.shape[1]), -jnp.inf, F32)
    for _ in range(kth):
        mx = jnp.max(work, axis=0, keepdims=True)
        eq = work == mx
        tau = jnp.where(cnt < kth, mx, tau)
        cnt = cnt + jnp.sum(jnp.where(eq, 1.0, 0.0), axis=0, keepdims=True)
        work = jnp.where(eq, -jnp.inf, work)
    return tau


def _top_rows(x, kth):
    work = x
    tops = []
    for _ in range(kth):
        mx = jnp.max(work, axis=0, keepdims=True)
        tops.append(mx)
        work = jnp.where(work == mx, -jnp.inf, work)
    return tops


def _gelu(x):
    return 0.5 * x * (1.0 + lax.erf(x * (2.0 ** -0.5)))


def _peer_kernel(h2_ref, x1_ref, gt_ref, wq_ref, bq_ref, keys_ref, u0_ref,
                 uy0_ref, uy1_ref, uxn0_ref, uxn1_ref, vtx_ref, vty_ref, gf_ref, out_ref, s1_scr, s2_scr, tau_scr, lz_scr,
                 sx_scr, sy_scr, cx_scr, cy_scr, acc_scr, *, tn, eb, rep, final):
    j = pl.program_id(1)
    hb = h2_ref[...]

    @pl.when(j == 0)
    def _():
        sx_scr[...] = _dot_nt(u0_ref[...], hb)
        q = (jnp.dot(hb, wq_ref[...], preferred_element_type=F32) + bq_ref[...]).astype(BF16)
        for h in range(P_HEADS):
            halves = []
            for c in range(2):
                hc = 2 * h + c
                s = _dot_nt(keys_ref[hc], q[:, P_HALF * hc:P_HALF * (hc + 1)])
                halves.append((s, _top_rows(s, P_TOPK)))
            (s1, m1), (s2, m2) = halves
            m1s = jnp.concatenate(m1, axis=0)
            m2s = jnp.concatenate(m2, axis=0)
            hk = P_TOPK // 2
            m2lo = m2s[:hk]
            skip2 = lax.broadcasted_iota(I32, m2lo.shape, 0) < 2
            cand = jnp.concatenate(
                [m1s + m2[0], m1s[:hk] + m2[1], m2s[hk:] + m1[0]]
                + [jnp.where(skip2, -jnp.inf, m2lo + m1[r1]) for r1 in range(5)], axis=0)
            tau = _kth_largest_rows(cand, P_TOPK)
            z = jnp.sum(jnp.where(cand >= tau, jnp.exp(cand - (m1[0] + m2[0])), 0.0),
                        axis=0, keepdims=True)
            s1_scr[h] = jnp.where(s1 >= m1[P_TOPK - 1], s1, -jnp.inf)
            s2_scr[h] = jnp.where(s2 >= m2[P_TOPK - 1], s2, -jnp.inf)
            lz_scr[h] = m1[0] + m2[0] + jnp.log(z)
            tau_scr[h] = tau
        acc_scr[...] = jnp.zeros_like(acc_scr)

    n_i1 = eb // N_KEYS
    half = eb // 2

    def gate_and_project(sc_scr, coef_scr, blk, vt_blk):
        for ci in range(n_i1):
            i1 = blk * n_i1 + ci
            s1_rows = [s1_scr[h, pl.ds(i1, 1), :] for h in range(P_HEADS)]
            for tc in range(tn // LANES):
                ln = slice(LANES * tc, LANES * (tc + 1))
                gate = jnp.zeros((N_KEYS, LANES), F32)
                for h in range(P_HEADS):
                    pair = s1_rows[h][:, ln] + s2_scr[h, :, ln]
                    gate = gate + jnp.where(pair >= tau_scr[h, :, ln],
                                            jnp.exp(pair - lz_scr[h, :, ln]), 0.0)
                act = _gelu(sc_scr[N_KEYS * ci:N_KEYS * (ci + 1), ln])
                coef_scr[N_KEYS * ci:N_KEYS * (ci + 1), ln] = (gate * act).astype(BF16)
        acc_scr[...] += jnp.dot(vt_blk[0], coef_scr[...], preferred_element_type=F32)

    def block_scores(sc_scr, u_halves):
        for hf, u_half in enumerate(u_halves):
            sc_scr[half * hf:half * (hf + 1), :] = _dot_nt(u_half[...], hb)

    block_scores(sy_scr, (uy0_ref, uy1_ref))
    gate_and_project(sx_scr, cx_scr, 2 * j, vtx_ref)
    block_scores(sx_scr, (uxn0_ref, uxn1_ref))
    gate_and_project(sy_scr, cy_scr, 2 * j + 1, vty_ref)

    @pl.when(j == pl.num_programs(1) - 1)
    def _():
        d = acc_scr.shape[0]
        gt = gt_ref[...]
        gt = jnp.broadcast_to(gt, (gt.shape[0], rep, d)).reshape(tn, d)
        x2 = x1_ref[...] + gt * acc_scr[...].T
        if final:
            x2 = _rms(x2) * gf_ref[...]
        out_ref[...] = x2


def _peer_call(h2, x1, gt, lp, g_final, *, tn, eb, final):
    b, t, d = x1.shape
    n = b * t
    assert n % tn == 0 and (t % tn == 0 or tn % t == 0)
    nbt = max(1, tn // t)
    tiles_per_b = max(1, t // tn)
    n_exp = lp["p_u"].shape[0]
    n_blk = n_exp // eb
    half = eb // 2
    assert n_blk % 2 == 0
    kern = functools.partial(_peer_kernel, tn=tn, eb=eb, rep=tn // nbt, final=final)
    full = lambda a: pl.BlockSpec(a.shape, lambda ti, e: (0,) * a.ndim)
    tok = pl.BlockSpec((tn, d), lambda ti, e: (ti, 0))
    sel = pltpu.VMEM((P_HEADS, N_KEYS, tn), F32)
    per_head_row = pltpu.VMEM((P_HEADS, 1, tn), F32)
    blk_scores = pltpu.VMEM((eb, tn), F32)
    blk_coef = pltpu.VMEM((eb, tn), BF16)
    out = pl.pallas_call(
        kern,
        grid=(n // tn, n_blk // 2),
        in_specs=[tok, tok,
                  pl.BlockSpec((nbt, 1, d), lambda ti, e: (ti // tiles_per_b, 0, 0)),
                  full(lp["p_wq"]), full(lp["p_bq"]), full(lp["p_keys"]),
                  pl.BlockSpec((eb, d), lambda ti, e: (0, 0))]
        + [pl.BlockSpec((half, d), lambda ti, e, i=i: (2 * (2 * e + 1) + i, 0))
           for i in range(2)]
        + [pl.BlockSpec((half, d),
                        lambda ti, e, i=i: (2 * jnp.minimum(2 * e + 2, n_blk - 1) + i, 0))
           for i in range(2)]
        + [pl.BlockSpec((1, d, eb), lambda ti, e, i=i: (2 * e + i, 0, 0)) for i in range(2)]
        + [full(g_final)],
        out_specs=tok,
        out_shape=jax.ShapeDtypeStruct((n, d), F32),
        scratch_shapes=[sel, sel, per_head_row, per_head_row,
                        blk_scores, blk_scores, blk_coef, blk_coef, pltpu.VMEM((d, tn), F32)],
        compiler_params=_params("arbitrary", "arbitrary"),
        name="peer",
    )(h2.reshape(n, d), x1.reshape(n, d), gt, lp["p_wq"], lp["p_bq"], lp["p_keys"],
      *([lp["p_u"]] * 5), *([lp["p_vt"]] * 2), g_final)
    return out.reshape(b, t, d)


def _layer(x, mod, lp, vfirst, past, q_offset, g_final, final, tiles):
    b, t, d = x.shape
    sh_t, sc_t, gt_t, sh_c, sc_c, gt_c = (m[:, None, :] for m in jnp.split(mod, 6, axis=-1))
    qi, k, v, ik, iw, rw, gates = _inproj_call(
        x, sc_t, sh_t, lp["g_norm1"], lp["w_in"], lp["idx_k_g"], lp["idx_k_b"], tiles["tm"])

    if past is None:
        k_all, v_all, ik_all = k, v, ik
        m0 = jnp.zeros((b, RW_HEADS, RW_HEAD_DIM, RW_HEAD_DIM), F32)
        shift0 = jnp.zeros((b, 1, RW_PAD), F32)
    else:
        k_past, v_past, ik_past, s0, rw_prev = past
        pl_ = k_past.shape[1]
        k_all = jnp.concatenate([k_past.reshape(b, pl_, KV_W), k], axis=1)
        v_all = jnp.concatenate([v_past.reshape(b, pl_, KV_W), v], axis=1)
        ik_all = jnp.concatenate([ik_past, ik], axis=1)
        m0 = jnp.swapaxes(s0, -1, -2)
        shift0 = jnp.pad(rw_prev, ((0, 0), (0, 0), (0, RW_PAD - RW_COLS)))
    tq = tiles["tq"]
    t_pad = -(-t // tq) * tq
    qpad = ((0, 0), (0, t_pad - t), (0, 0))
    o_a = _dsa_call(jnp.pad(qi, qpad), jnp.pad(iw, qpad), k_all, v_all, ik_all,
                    q_offset=q_offset, tq=tq, tk=tiles["tk"])[:, :t]

    r, lw, k2, v2, kkn, bb, g, bonus = _rwprep_call(rw, shift0, lp, vfirst, tiles["tm"])
    if vfirst is None:
        vfirst = v2
    o_scan, m_new = _scan_call(r, lw, k2, v2, kkn, bb, m0, tiles["c"])

    x1, h2 = _merge_call(x, o_a, o_scan, bonus, g, gates, gt_t, sc_c, sh_c, lp, tiles["tm"])
    x2 = _peer_call(h2, x1, gt_c, lp, g_final, tn=tiles["tn"], eb=tiles["eb"], final=final)
    state = (k.reshape(b, t, KV_HEADS, HEAD_DIM), v.reshape(b, t, KV_HEADS, HEAD_DIM), ik,
             jnp.swapaxes(m_new, -1, -2), rw[:, -1:, :RW_COLS])
    return x2, vfirst, state


def _tiles(t):
    return {"tm": min(t, 256), "tq": 2 * LANES if t % (2 * LANES) == 0 else LANES,
            "tk": 256, "c": min(t, CHUNK),
            "tn": 512 if t >= 512 else LANES, "eb": PEER_EB}


def kernel(x_prompt, x_sample, cache_k, cache_v, cache_kidx, state_wkv, state_shift, c_prompt, c_sample, w_ada, b_ada, g_norm1, w_in, idx_k_g, idx_k_b, rw_mu, rw_w0, rw_w_up, rw_a0, rw_a_up, rw_g_up, rw_k_k, rw_k_a, rw_r_k, rw_lnx_g, rw_lnx_b, rw_v0, rw_v_down, rw_v_up, w_oa, w_ob, w_out, g_norm2, peer_wq, peer_bq, peer_sub_keys, peer_u, peer_v, g_final):
    depth = w_in.shape[0]
    nbp = x_prompt.shape[0]
    past_len = cache_k.shape[2]
    bd = _head_block_diag()
    row = lambda a: a.reshape(1, -1)
    xp, xs = x_prompt, x_sample
    vf_p, vf_s = None, None
    new_p, new_s = [], []
    c_all = jnp.concatenate([c_prompt, c_sample], axis=0)
    gf = row(g_final)
    for l in range(depth):
        lp = {
            "g_norm1": row(g_norm1[l]), "w_in": _pack_w_in(w_in[l]),
            "idx_k_g": row(idx_k_g[l]), "idx_k_b": row(idx_k_b[l]),
            "mu": jnp.pad(row(rw_mu[l]), ((0, 0), (0, RW_PAD - RW_COLS))),
            "w0": row(rw_w0[l]), "a0": row(rw_a0[l]),
            "w_up": _pad_rows(rw_w_up[l], 0, LANES),
            "a_up": _pad_rows(rw_a_up[l], W_LORA, LANES),
            "g_up": _pad_rows(rw_g_up[l], 0, RW_PAD - 3 * RW_WIDTH - LANES),
            "k_k": row(rw_k_k[l]), "k_a": row(rw_k_a[l]), "r_k": row(rw_r_k[l]),
            "lnx_g": row(rw_lnx_g[l]), "lnx_b": row(rw_lnx_b[l]), "bd": bd,
            "w_oa": w_oa[l].astype(BF16), "w_ob": w_ob[l].astype(BF16),
            "w_out": w_out[l].astype(BF16), "g_norm2": row(g_norm2[l]),
            "p_wq": peer_wq[l].astype(BF16), "p_bq": row(peer_bq[l]),
            "p_keys": peer_sub_keys[l].reshape(2 * P_HEADS, N_KEYS, P_HALF).astype(BF16),
            "p_u": peer_u[l].astype(BF16),
            "p_vt": jnp.swapaxes(peer_v[l].reshape(-1, PEER_EB, peer_v.shape[-1]), 1, 2
                                 ).astype(BF16),
        }
        if l > 0:
            lp["v0"] = row(rw_v0[l - 1])
            lp["v_down"] = jnp.pad(rw_v_down[l - 1], ((0, 0), (0, LANES - V_LORA))).astype(BF16)
            lp["v_up"] = _pad_rows(rw_v_up[l - 1], 0, LANES)
        mod = _mod_call(c_all, w_ada[l], b_ada[l])
        final = l == depth - 1
        xp, vf_p, st_p = _layer(xp, mod[:nbp], lp, vf_p, None, 0, gf, final,
                                _tiles(xp.shape[1]))
        past = (cache_k[l], cache_v[l], cache_kidx[l], state_wkv[l], state_shift[l])
        xs, vf_s, st_s = _layer(xs, mod[nbp:], lp, vf_s, past, past_len, gf, final,
                                _tiles(xs.shape[1]))
        new_p.append(st_p)
        new_s.append(st_s)

    def stk(lst, i):
        return jnp.stack([e[i] for e in lst], axis=0)

    return (xp, xs,
            stk(new_p, 0), stk(new_p, 1), stk(new_p, 2), stk(new_p, 3), stk(new_p, 4),
            stk(new_s, 0), stk(new_s, 1), stk(new_s, 2), stk(new_s, 3), stk(new_s, 4))
```

```python
import functools

import numpy as np
import jax
import jax.numpy as jnp
from jax import lax
from jax.experimental import pallas as pl
from jax.experimental.pallas import tpu as pltpu

F32 = jnp.float32
BF16 = jnp.bfloat16
I32 = jnp.int32
HIGHEST = lax.Precision.HIGHEST

CHUNK = 64
N_HEADS = 8
HEAD_DIM = 64
KV_HEADS = 2
GROUP = N_HEADS // KV_HEADS
IDX_HEADS = 8
IDX_DIM = 64
DSA_TOPK = 256
ATTN_SCALE = HEAD_DIM ** -0.5
RW_HEADS = 8
RW_HEAD_DIM = 64
RW_WIDTH = RW_HEADS * RW_HEAD_DIM
W_LORA = 64
A_LORA = 64
V_LORA = 32
G_LORA = 160
RW_COLS = 3 * RW_WIDTH + W_LORA + A_LORA + G_LORA
N_KEYS = 128
P_HEADS = 8
P_HALF = 128
P_TOPK = 16
PEER_EB = 512
PEER_STEP_BLOCKS = 4
EPS = 1e-6
GN_EPS = 64e-5

LANES = 128
SUBLANES = 8
VMEM_LIMIT = 56 * 1024 * 1024

DSA_Q = N_HEADS * HEAD_DIM
IDX_Q = IDX_HEADS * IDX_DIM
KV_W = KV_HEADS * HEAD_DIM
QI_W = DSA_Q + IDX_Q
SMALL_W = 4 * LANES
RW_PAD = 15 * LANES
NEG = -1e30


def _params(*sem):
    return pltpu.CompilerParams(dimension_semantics=sem, vmem_limit_bytes=VMEM_LIMIT)


def _bdot(a, b):
    return jnp.dot(a.astype(BF16), b.astype(BF16), preferred_element_type=F32)


def _hdot(a, b):
    return jnp.dot(a, b, precision=HIGHEST, preferred_element_type=F32)


def _dot_nt(a, b, precision=None):
    return lax.dot_general(a, b, (((1,), (1,)), ((), ())), precision=precision,
                           preferred_element_type=F32)


def _dot_tn(a, b, precision=None):
    return lax.dot_general(a, b, (((0,), (0,)), ((), ())), precision=precision,
                           preferred_element_type=F32)


def _sigmoid(x):
    return 1.0 / (1.0 + jnp.exp(-x))


def _rms(x):
    return x * lax.rsqrt(jnp.mean(x * x, axis=-1, keepdims=True) + EPS)


def _mod_kernel(c_ref, w_ref, b_ref, o_ref):
    c = c_ref[...]
    o_ref[...] = _bdot(c * _sigmoid(c), w_ref[...]) + b_ref[...]


def _mod_call(c_all, w_ada, b_ada):
    nb, d = c_all.shape
    ncol = w_ada.shape[1] // d
    return pl.pallas_call(
        _mod_kernel,
        grid=(ncol,),
        in_specs=[pl.BlockSpec((nb, d), lambda j: (0, 0)),
                  pl.BlockSpec((d, d), lambda j: (0, j)),
                  pl.BlockSpec((1, d), lambda j: (0, j))],
        out_specs=pl.BlockSpec((nb, d), lambda j: (0, j)),
        out_shape=jax.ShapeDtypeStruct((nb, ncol * d), F32),
        compiler_params=_params("arbitrary"),
        name="adaln_mod",
    )(c_all, w_ada, b_ada.reshape(1, -1))


def _inproj_kernel(x_ref, sc_ref, sh_ref, g_ref, w_ref, ikg_ref, ikb_ref,
                   qi_ref, k_ref, v_ref, ik_ref, iw_ref, rw_ref, gates_ref):
    x = x_ref[0]
    h = (_rms(x) * g_ref[...]) * (1.0 + sc_ref[0]) + sh_ref[0]
    hb = h.astype(BF16)
    o0 = QI_W
    o1 = o0 + SMALL_W
    o2 = o1 + RW_PAD
    qi_ref[0] = jnp.dot(hb, w_ref[:, 0:o0], preferred_element_type=F32)
    small = jnp.dot(hb, w_ref[:, o0:o1], preferred_element_type=F32)
    k_ref[0] = small[:, 0:LANES]
    v_ref[0] = small[:, LANES:2 * LANES]
    ik = small[:, 2 * LANES:2 * LANES + IDX_DIM]
    ikc = ik - jnp.mean(ik, axis=-1, keepdims=True)
    ikn = ikc * lax.rsqrt(jnp.mean(ikc * ikc, axis=-1, keepdims=True) + EPS)
    ik_ref[0] = ikn * ikg_ref[...] + ikb_ref[...]
    iw_ref[0] = small[:, 3 * LANES:3 * LANES + IDX_HEADS]
    rw_ref[0] = jnp.dot(hb, w_ref[:, o1:o2], preferred_element_type=F32)
    gates_ref[0] = jnp.dot(hb, w_ref[:, o2:], preferred_element_type=F32)


def _pack_w_in(w_in):
    d = w_in.shape[0]
    offs = np.cumsum([0, DSA_Q, KV_W, KV_W, IDX_Q, IDX_DIM, IDX_HEADS, RW_COLS, 2 * d])
    q, k, v, iq, ik, iw, rw, gates = (w_in[:, offs[i]:offs[i + 1]] for i in range(8))
    z = lambda n: jnp.zeros((d, n), w_in.dtype)
    packed = jnp.concatenate(
        [q, iq, k, v, ik, z(LANES - IDX_DIM), iw, z(LANES - IDX_HEADS),
         rw, z(RW_PAD - RW_COLS), gates], axis=1)
    return packed.astype(BF16)


def _inproj_call(x, sc, sh, g1, w_packed, ikg, ikb, tm):
    b, t, d = x.shape
    nw = w_packed.shape[1]
    tok = lambda n: pl.BlockSpec((1, tm, n), lambda bi, i: (bi, i, 0))
    row = lambda n: pl.BlockSpec((1, n), lambda bi, i: (0, 0))
    per_b = pl.BlockSpec((1, 1, d), lambda bi, i: (bi, 0, 0))
    widths = (QI_W, LANES, LANES, IDX_DIM, IDX_HEADS, RW_PAD, 2 * d)
    return pl.pallas_call(
        _inproj_kernel,
        grid=(b, t // tm),
        in_specs=[tok(d), per_b, per_b, row(d),
                  pl.BlockSpec((d, nw), lambda bi, i: (0, 0)),
                  row(IDX_DIM), row(IDX_DIM)],
        out_specs=[tok(n) for n in widths],
        out_shape=[jax.ShapeDtypeStruct((b, t, n), F32) for n in widths],
        compiler_params=_params("arbitrary", "arbitrary"),
        name="norm_inproj",
    )(x, sc, sh, g1, w_packed, ikg, ikb)


def _dsa_kernel(qi_ref, iwt_ref, k_ref, vt_ref, ik_ref, o_ref,
                key_scr, bias_scr, iq_scr, qg_scr, *, tq, tk, l_valid, q_offset, topk):
    qb = pl.program_id(1)
    int_min = jnp.int32(-2 ** 31)
    q0 = q_offset + qb * tq
    last_chunk = (q0 + tq - 1) // CHUNK
    n_adm = jnp.minimum((last_chunk + 1) * CHUNK, l_valid)
    n_kt = (n_adm + tk - 1) // tk

    x = qi_ref[0]
    for h in range(IDX_HEADS):
        iq_scr[h] = x[:, DSA_Q + IDX_DIM * h:DSA_Q + IDX_DIM * (h + 1)].astype(BF16)
    for g in range(KV_HEADS):
        for r in range(GROUP):
            h = GROUP * g + r
            qg_scr[g, r * tq:(r + 1) * tq, :] = (
                x[:, HEAD_DIM * h:HEAD_DIM * (h + 1)] * ATTN_SCALE).astype(BF16)
    iwt = iwt_ref[0]
    q_chunk = (q0 + lax.broadcasted_iota(I32, (1, tq), 1)) // CHUNK
    row_iota = lax.broadcasted_iota(I32, (tk, tq), 0)

    def tile_base(kt):
        return pl.multiple_of(kt * tk, tk)

    def score_body(kt, carry):
        base = tile_base(kt)
        ikt = ik_ref[0, pl.ds(base, tk), :].astype(BF16)
        acc = jnp.zeros((tk, tq), F32)
        for h in range(IDX_HEADS):
            acc = acc + iwt[h:h + 1, :] * jnp.maximum(_dot_nt(ikt, iq_scr[h]), 0.0)
        acc = jnp.where(acc == 0.0, 0.0, acc)
        bits = lax.bitcast_convert_type(acc, I32)
        key = jnp.where(bits < 0, bits ^ jnp.int32(0x7FFFFFFF), bits)
        kpos = base + row_iota
        adm = (kpos < l_valid) & ((kpos // CHUNK) <= q_chunk)
        key_scr[pl.ds(base, tk), :] = jnp.where(adm, key, int_min)
        return carry

    lax.fori_loop(0, n_kt, score_body, 0)

    acc_rows = 4 * SUBLANES

    def count(pred_fn):
        def body(kt, c):
            base = tile_base(kt)
            m = jnp.where(pred_fn(key_scr[pl.ds(base, tk), :], base + row_iota), 1, 0)
            return c + jnp.sum(m.reshape(tk // acc_rows, acc_rows, tq), axis=0)
        c = lax.fori_loop(0, n_kt // 2, lambda i, c: body(2 * i + 1, body(2 * i, c)),
                          jnp.zeros((acc_rows, tq), I32))
        c = lax.fori_loop(2 * (n_kt // 2), n_kt, body, c)
        return jnp.sum(c, axis=0, keepdims=True)

    def bit_body(i, tb):
        cand_b = tb | lax.shift_left(jnp.int32(1), 31 - i)
        cand = cand_b ^ int_min
        cnt = count(lambda kk, idx: kk >= cand)
        return jnp.where(cnt >= topk, cand_b, tb)

    tau = lax.fori_loop(0, 32, bit_body, jnp.zeros((1, tq), I32)) ^ int_min
    cnt_ge = count(lambda kk, idx: kk >= tau)
    cnt_gt = count(lambda kk, idx: kk > tau)
    need = topk - cnt_gt
    excess = (tau > int_min) & (cnt_ge - cnt_gt > need)
    any_excess = jnp.max(jnp.where(excess, 1, 0)) > 0

    idx_bits = 13
    def tie_limit():
        def jbody(i, j):
            cand_j = j | lax.shift_left(jnp.int32(1), idx_bits - 1 - i)
            f = count(lambda kk, idx: (kk == tau) & (idx < cand_j))
            return jnp.where(f <= need, cand_j, j)
        return lax.fori_loop(0, idx_bits, jbody, jnp.zeros((1, tq), I32))

    j_lim = lax.cond(any_excess, tie_limit,
                     lambda: jnp.full((1, tq), 2 ** idx_bits - 1, I32))

    def bias_body(kt, carry):
        base = tile_base(kt)
        kk = key_scr[pl.ds(base, tk), :]
        sel = (kk > tau) | ((kk == tau) & ((base + row_iota) < j_lim))
        sel = sel & (kk != int_min)
        bias_scr[pl.ds(base, tk), :] = jnp.where(sel, 0.0, NEG)
        return carry

    lax.fori_loop(0, n_kt, bias_body, 0)

    def attn_body(kt, carry):
        base = tile_base(kt)
        bias = bias_scr[pl.ds(base, tk), :]
        k_all = k_ref[0, pl.ds(base, tk), :]
        vt_all = vt_ref[0, kt]
        new = []
        s_groups = [_dot_nt(k_all[:, HEAD_DIM * g:HEAD_DIM * (g + 1)].astype(BF16), qg_scr[g])
                    for g in range(KV_HEADS)]
        for g in range(KV_HEADS):
            s_all = s_groups[g]
            ps, stats = [], []
            for r in range(GROUP):
                m, l, acc = carry[GROUP * g + r]
                s = s_all[:, r * tq:(r + 1) * tq] + bias
                m_new = jnp.maximum(m, jnp.max(s, axis=0, keepdims=True))
                alpha = jnp.exp(m - m_new)
                p = jnp.exp(s - m_new)
                ps.append(p.astype(BF16))
                stats.append((m_new, l * alpha + jnp.sum(p, axis=0, keepdims=True), alpha, acc))
            pv = jnp.dot(vt_all[HEAD_DIM * g:HEAD_DIM * (g + 1), :].astype(BF16),
                         jnp.concatenate(ps, axis=1), preferred_element_type=F32)
            for r, (m_new, l_new, alpha, acc) in enumerate(stats):
                new.append((m_new, l_new, acc * alpha + pv[:, r * tq:(r + 1) * tq]))
        return tuple(new)

    init = tuple((jnp.full((1, tq), NEG, F32), jnp.zeros((1, tq), F32),
                  jnp.zeros((HEAD_DIM, tq), F32)) for _ in range(N_HEADS))
    fin = lax.fori_loop(0, n_kt, attn_body, init)
    o_ref[0] = jnp.concatenate([acc / l for _, l, acc in fin], axis=0).T


def _dsa_call(qi, iw, k_all, v_all, ik_all, *, q_offset, tq, tk):
    b, t, _ = qi.shape
    l_valid = k_all.shape[1]
    topk = min(DSA_TOPK, l_valid // 4)
    assert topk <= tk and t % tq == 0
    l_pad = -(-l_valid // tk) * tk
    assert l_pad < 2 ** 13 - 1
    pad = ((0, 0), (0, l_pad - l_valid), (0, 0))
    k_p, v_p, ik_p = (jnp.pad(a, pad) for a in (k_all, v_all, ik_all))
    nkt = l_pad // tk
    vt = jnp.swapaxes(v_p.reshape(b, nkt, tk, KV_W), 2, 3)
    iwt = jnp.swapaxes(iw, 1, 2)
    kern = functools.partial(_dsa_kernel, tq=tq, tk=tk, l_valid=l_valid,
                             q_offset=q_offset, topk=topk)
    return pl.pallas_call(
        kern,
        grid=(b, t // tq),
        in_specs=[pl.BlockSpec((1, tq, QI_W), lambda bi, i: (bi, i, 0)),
                  pl.BlockSpec((1, IDX_HEADS, tq), lambda bi, i: (bi, 0, i)),
                  pl.BlockSpec((1, l_pad, KV_W), lambda bi, i: (bi, 0, 0)),
                  pl.BlockSpec((1, nkt, KV_W, tk), lambda bi, i: (bi, 0, 0, 0)),
                  pl.BlockSpec((1, l_pad, IDX_DIM), lambda bi, i: (bi, 0, 0))],
        out_specs=pl.BlockSpec((1, tq, DSA_Q), lambda bi, i: (bi, i, 0)),
        out_shape=jax.ShapeDtypeStruct((b, t, DSA_Q), F32),
        scratch_shapes=[pltpu.VMEM((l_pad, tq), I32), pltpu.VMEM((l_pad, tq), F32),
                        pltpu.VMEM((IDX_HEADS, tq, IDX_DIM), BF16),
                        pltpu.VMEM((KV_HEADS, GROUP * tq, HEAD_DIM), BF16)],
        compiler_params=_params("arbitrary", "arbitrary"),
        name="dsa_attention",
    )(qi, iwt, k_p, vt, ik_p)


def _rwprep_kernel(*refs, has_vfirst):
    (rw_ref, prev8_ref, shift0_ref, mu_ref, w0_ref, a0_ref, wup_ref, aup_ref, gup_ref,
     kk_ref, ka_ref, rk_ref, bd_ref) = refs[:13]
    if has_vfirst:
        vfirst_ref, v0_ref, vdown_ref, vup_ref = refs[13:17]
        outs = refs[17:]
    else:
        outs = refs[13:]
    r_o, lw_o, k_o, v_o, kkn_o, b_o, g_o, bonus_o = outs
    i = pl.program_id(1)
    rw = rw_ref[0]
    prev = jnp.where(i == 0, shift0_ref[0], prev8_ref[0][SUBLANES - 1:SUBLANES, :])
    row = lax.broadcasted_iota(I32, rw.shape, 0)
    shifted = jnp.where(row == 0, prev, pltpu.roll(rw, 1, 0))
    mix = rw + mu_ref[...] * (shifted - rw)
    w3 = RW_WIDTH
    r = mix[:, 0:w3]
    kr = mix[:, w3:2 * w3]
    vr = mix[:, 2 * w3:3 * w3]
    wa = mix[:, 3 * w3:3 * w3 + LANES]
    gd = mix[:, 3 * w3 + LANES:]
    z = w0_ref[...] + _bdot(jnp.tanh(wa), wup_ref[...])
    nz = -z
    softplus = jnp.maximum(nz, 0.0) + jnp.log(1.0 + jnp.exp(-jnp.abs(nz)))
    lw = -jnp.exp(-softplus - 0.5)
    a = _sigmoid(a0_ref[...] + _bdot(wa, aup_ref[...]))
    g = _bdot(_sigmoid(gd), gup_ref[...])
    if has_vfirst:
        lora = _bdot(_bdot(vr, vdown_ref[...]), vup_ref[...])
        vr = vr + (vfirst_ref[0] - vr) * _sigmoid(v0_ref[...] + lora)
    bd = bd_ref[...]
    kkr = kr * kk_ref[...]
    kkn = kkr / jnp.maximum(jnp.sqrt(_hdot(kkr * kkr, bd)), 1e-12)
    k2 = kr * (1.0 + (a - 1.0) * ka_ref[...])
    r_o[0] = r
    lw_o[0] = lw
    k_o[0] = k2
    v_o[0] = vr
    kkn_o[0] = kkn
    b_o[0] = kkn * a
    g_o[0] = g
    bonus_o[0] = _hdot(r * k2 * rk_ref[...], bd) * vr


def _head_block_diag():
    h = np.arange(RW_WIDTH) // RW_HEAD_DIM
    return jnp.asarray((h[:, None] == h[None, :]).astype(np.float32))


def _pad_rows(w, lo, total):
    return jnp.pad(w, ((lo, total - lo - w.shape[0]), (0, 0))).astype(BF16)


def _rwprep_call(rw, shift0, lp, vfirst, tm):
    b, t, _ = rw.shape
    has_vfirst = vfirst is not None
    tok = lambda n: pl.BlockSpec((1, tm, n), lambda bi, i: (bi, i, 0))
    full = lambda a: pl.BlockSpec(a.shape, lambda bi, i: (0,) * a.ndim)
    consts = [lp["mu"], lp["w0"], lp["a0"], lp["w_up"], lp["a_up"], lp["g_up"],
              lp["k_k"], lp["k_a"], lp["r_k"], lp["bd"]]
    args = [rw, rw, shift0] + consts
    in_specs = [tok(RW_PAD),
                pl.BlockSpec((1, SUBLANES, RW_PAD),
                             lambda bi, i: (bi, jnp.maximum(i * (tm // SUBLANES) - 1, 0), 0)),
                pl.BlockSpec((1, 1, RW_PAD), lambda bi, i: (bi, 0, 0))]
    in_specs += [full(a) for a in consts]
    if has_vfirst:
        extra = [lp["v0"], lp["v_down"], lp["v_up"]]
        args += [vfirst] + extra
        in_specs += [tok(RW_WIDTH)] + [full(a) for a in extra]
    return pl.pallas_call(
        functools.partial(_rwprep_kernel, has_vfirst=has_vfirst),
        grid=(b, t // tm),
        in_specs=in_specs,
        out_specs=[tok(RW_WIDTH)] * 8,
        out_shape=[jax.ShapeDtypeStruct((b, t, RW_WIDTH), F32)] * 8,
        compiler_params=_params("arbitrary", "arbitrary"),
        name="rwkv_prep",
    )(*args)


def _scan_intra_kernel(r_ref, lw_ref, k_ref, v_ref, kk_ref, b_ref,
                       r2_ref, oi_ref, a_ref, d_ref, *, c, nc):
    row = lax.broadcasted_iota(I32, (c, c), 0)
    col = lax.broadcasted_iota(I32, (c, c), 1)
    incl = row >= col
    strict = row > col
    eye_c = jnp.where(row == col, 1.0, 0.0)
    ones_incl = jnp.where(incl, 1.0, 0.0)
    n = RW_HEAD_DIM
    rn = lax.broadcasted_iota(I32, (n, n), 0)
    cn = lax.broadcasted_iota(I32, (n, n), 1)
    nlev = int(np.log2(c))

    prep = []
    for ci in range(nc):
        rows = slice(ci * c, (ci + 1) * c)
        lw, k, b = lw_ref[0, rows, :], k_ref[0, rows, :], b_ref[0, rows, :]
        cum = _hdot(ones_incl, lw)
        total = cum[c - 1:c, :]
        g_inv = jnp.exp(-cum)
        g_rem = jnp.exp(total - cum)
        prep.append(dict(
            alpha=kk_ref[0, rows, :] * jnp.exp(cum - lw), beta=b * g_inv, kappa=k * g_inv,
            rho=r_ref[0, rows, :] * jnp.exp(cum), khat=k * g_rem, bhat=b * g_rem,
            g_tot=jnp.exp(total), v=v_ref[0, rows, :]))
    units = [(ci, h) for ci in range(nc) for h in range(RW_HEADS)]

    def head(ci, h, name):
        return prep[ci][name][:, n * h:n * (h + 1)]

    grams = [_dot_nt(
        jnp.concatenate([head(ci, h, "alpha"), head(ci, h, "rho")], axis=0).astype(BF16),
        jnp.concatenate([head(ci, h, "beta"), head(ci, h, "kappa")], axis=0).astype(BF16))
        for ci, h in units]
    l_ak = [jnp.where(strict, g[:c, c:], 0.0) for g in grams]
    l_rb = [jnp.where(incl, g[c:, :c], 0.0) for g in grams]
    l_rk = [jnp.where(incl, g[c:, c:], 0.0) for g in grams]
    ps = [-jnp.where(strict, g[:c, :c], 0.0) for g in grams]
    tinvs = [eye_c + p for p in ps]
    for _ in range(nlev - 1):
        ps = [_bdot(p, p) for p in ps]
        tinvs = [t + _bdot(t, p) for t, p in zip(tinvs, ps)]
    lvs = [_bdot(jnp.concatenate([ak, rk], axis=0), head(ci, h, "v"))
           for ak, rk, (ci, h) in zip(l_ak, l_rk, units)]
    wys = [_hdot(t, jnp.concatenate([head(ci, h, "alpha"), lv[:c]], axis=1))
           for t, lv, (ci, h) in zip(tinvs, lvs, units)]
    rbs = [_bdot(rb, wy) for rb, wy in zip(l_rb, wys)]
    bws = [_dot_tn(head(ci, h, "bhat"), wy, HIGHEST) for wy, (ci, h) in zip(wys, units)]
    kvs = [_dot_tn(head(ci, h, "khat"), head(ci, h, "v"), HIGHEST) for ci, h in units]
    for ci in range(nc):
        rows = slice(ci * c, (ci + 1) * c)
        mine = [u for u, (cj, _) in enumerate(units) if cj == ci]
        r2_ref[0, rows, :] = jnp.concatenate(
            [head(ci, h, "rho") - rbs[u][:, :n] for h, u in enumerate(mine)], axis=1)
        oi_ref[0, rows, :] = jnp.concatenate(
            [lvs[u][c:] - rbs[u][:, n:] for u in mine], axis=1)
        for h, u in enumerate(mine):
            dg = jnp.where(rn == cn,
                           jnp.broadcast_to(prep[ci]["g_tot"][:, n * h:n * (h + 1)], (n, n)), 0.0)
            a_ref[0, ci, h] = dg - bws[u][:, :n]
            d_ref[0, ci, h] = kvs[u] - bws[u][:, n:]


def _scan_inter_kernel(r2_ref, oi_ref, a_ref, d_ref, m0_ref, o_ref, mout_ref, m_scr, *, nb):
    ci = pl.program_id(1)

    @pl.when(ci == 0)
    def _():
        m_scr[...] = m0_ref[...]

    n = RW_HEAD_DIM
    for bi in range(nb):
        r2 = r2_ref[bi]
        outs = []
        for h in range(RW_HEADS):
            m0 = m_scr[bi, h]
            outs.append(_hdot(r2[:, n * h:n * (h + 1)], m0))
            m_scr[bi, h] = _hdot(a_ref[bi, 0, h], m0) + d_ref[bi, 0, h]
        o_ref[bi] = jnp.concatenate(outs, axis=1) + oi_ref[bi]

    @pl.when(ci == pl.num_programs(1) - 1)
    def _():
        mout_ref[...] = m_scr[...]


def _scan_call(r, lw, k, v, kk, bb, m0, c):
    b, t, w = r.shape
    nch = t // c
    nc = next(n for n in (4, 2, 1) if nch % n == 0)
    nb = next(n for n in (4, 2, 1) if b % n == 0)
    hd = RW_HEAD_DIM
    tok = pl.BlockSpec((1, nc * c, w), lambda bi, i: (bi, i, 0))
    mats = pl.BlockSpec((1, nc, RW_HEADS, hd, hd), lambda bi, i: (bi, i, 0, 0, 0))
    r2, oi, a_mat, d_mat = pl.pallas_call(
        functools.partial(_scan_intra_kernel, c=c, nc=nc),
        grid=(b, nch // nc),
        in_specs=[tok] * 6,
        out_specs=[tok, tok, mats, mats],
        out_shape=[jax.ShapeDtypeStruct((b, t, w), F32)] * 2
        + [jax.ShapeDtypeStruct((b, nch, RW_HEADS, hd, hd), F32)] * 2,
        compiler_params=_params("arbitrary", "arbitrary"),
        name="rwkv_chunk_terms",
    )(r, lw, k, v, kk, bb)
    tok_b = pl.BlockSpec((nb, c, w), lambda bi, i: (bi, i, 0))
    mat_b = pl.BlockSpec((nb, 1, RW_HEADS, hd, hd), lambda bi, i: (bi, i, 0, 0, 0))
    st = pl.BlockSpec((nb, RW_HEADS, hd, hd), lambda bi, i: (bi, 0, 0, 0))
    return pl.pallas_call(
        functools.partial(_scan_inter_kernel, nb=nb),
        grid=(b // nb, nch),
        in_specs=[tok_b, tok_b, mat_b, mat_b, st],
        out_specs=[tok_b, st],
        out_shape=[jax.ShapeDtypeStruct((b, t, w), F32),
                   jax.ShapeDtypeStruct(m0.shape, F32)],
        scratch_shapes=[pltpu.VMEM((nb, RW_HEADS, hd, hd), F32)],
        compiler_params=_params("arbitrary", "arbitrary"),
        name="rwkv_scan",
    )(r2, oi, a_mat, d_mat, m0)


def _merge_kernel(x_ref, oa_ref, os_ref, bonus_ref, g_ref, gates_ref, gt_ref, sc_ref, sh_ref,
                  woa_ref, wob_ref, wout_ref, lng_ref, lnb_ref, gn2_ref, bd_ref,
                  x1_ref, h2_ref):
    bd = bd_ref[...]
    inv_n = 1.0 / RW_HEAD_DIM
    o = os_ref[0]
    oc = o - _hdot(o, bd) * inv_n
    var = _hdot(oc * oc, bd) * inv_n
    y = oc * lax.rsqrt(var + GN_EPS) * lng_ref[...] + lnb_ref[...]
    ob = (y + bonus_ref[0]) * g_ref[0]
    gates = gates_ref[0]
    d = x_ref.shape[-1]
    merged = (_sigmoid(gates[:, :d]) * _bdot(oa_ref[0], woa_ref[...])
              + _sigmoid(gates[:, d:]) * _bdot(ob, wob_ref[...]))
    x1 = x_ref[0] + gt_ref[0] * _bdot(merged, wout_ref[...])
    x1_ref[0] = x1
    h2 = (_rms(x1) * gn2_ref[...]) * (1.0 + sc_ref[0]) + sh_ref[0]
    h2_ref[0] = h2.astype(BF16)


def _merge_call(x, oa, o_scan, bonus, g, gates, gt, sc, sh, lp, tm):
    b, t, d = x.shape
    tok = lambda n: pl.BlockSpec((1, tm, n), lambda bi, i: (bi, i, 0))
    per_b = pl.BlockSpec((1, 1, d), lambda bi, i: (bi, 0, 0))
    full = lambda a: pl.BlockSpec(a.shape, lambda bi, i: (0,) * a.ndim)
    consts = [lp["w_oa"], lp["w_ob"], lp["w_out"], lp["lnx_g"], lp["lnx_b"], lp["g_norm2"],
              lp["bd"]]
    return pl.pallas_call(
        _merge_kernel,
        grid=(b, t // tm),
        in_specs=[tok(d), tok(DSA_Q), tok(RW_WIDTH), tok(RW_WIDTH), tok(RW_WIDTH), tok(2 * d),
                  per_b, per_b, per_b] + [full(a) for a in consts],
        out_specs=[tok(d), tok(d)],
        out_shape=[jax.ShapeDtypeStruct((b, t, d), F32), jax.ShapeDtypeStruct((b, t, d), BF16)],
        compiler_params=_params("arbitrary", "arbitrary"),
        name="merge_out",
    )(x, oa, o_scan, bonus, g, gates, gt, sc, sh, *consts)


def _kth_largest_rows(x, kth):
    work = x
    cnt = jnp.zeros((1, x.shape[1]), F32)
    tau = jnp.full((1, x.shape[1]), -jnp.inf, F32)
    for _ in range(kth):
        mx = jnp.max(work, axis=0, keepdims=True)
        eq = work == mx
        tau = jnp.where(cnt < kth, mx, tau)
        cnt = cnt + jnp.sum(jnp.where(eq, 1.0, 0.0), axis=0, keepdims=True)
        work = jnp.where(eq, -jnp.inf, work)
    return tau


def _top_rows(x, kth):
    work = x
    tops = []
    for _ in range(kth):
        mx = jnp.max(work, axis=0, keepdims=True)
        tops.append(mx)
        work = jnp.where(work == mx, -jnp.inf, work)
    return tops


def _gelu(x):
    return 0.5 * x * (1.0 + lax.erf(x * (2.0 ** -0.5)))


def _peer_kernel(h2_ref, x1_ref, gt_ref, wq_ref, bq_ref, keys_ref, ublk_ref, unext_ref,
                 vt_ref, gf_ref, out_ref, s1_scr, s2_scr, e1_scr, e2_scr, tau_scr,
                 sx_scr, sy_scr, acc_scr, *, tn, eb, rep, final):
    j = pl.program_id(1)
    hb = h2_ref[...]

    @pl.when(j == 0)
    def _():
        sx_scr[...] = _dot_nt(ublk_ref[0], hb)
        q = (jnp.dot(hb, wq_ref[...], preferred_element_type=F32) + bq_ref[...]).astype(BF16)
        for h in range(P_HEADS):
            halves = []
            for c in range(2):
                hc = 2 * h + c
                s = _dot_nt(keys_ref[hc], q[:, P_HALF * hc:P_HALF * (hc + 1)])
                halves.append((s, _top_rows(s, P_TOPK)))
            (s1, m1), (s2, m2) = halves
            m1s = jnp.concatenate(m1, axis=0)
            m2s = jnp.concatenate(m2, axis=0)
            hk = P_TOPK // 2
            m2lo = m2s[:hk]
            skip2 = lax.broadcasted_iota(I32, m2lo.shape, 0) < 2
            cand = jnp.concatenate(
                [m1s + m2[0], m1s[:hk] + m2[1], m2s[hk:] + m1[0]]
                + [jnp.where(skip2, -jnp.inf, m2lo + m1[r1]) for r1 in range(5)], axis=0)
            tau = _kth_largest_rows(cand, P_TOPK)
            z = jnp.sum(jnp.where(cand >= tau, jnp.exp(cand - (m1[0] + m2[0])), 0.0),
                        axis=0, keepdims=True)
            s1_scr[h] = jnp.where(s1 >= m1[P_TOPK - 1], s1, -jnp.inf)
            s2_scr[h] = jnp.where(s2 >= m2[P_TOPK - 1], s2, -jnp.inf)
            e1_scr[h] = jnp.exp(s1 - m1[0])
            e2_scr[h] = jnp.exp(s2 - m2[0]) / z
            tau_scr[h] = tau
        acc_scr[...] = jnp.zeros_like(acc_scr)

    n_i1 = eb // N_KEYS
    half = eb // 2

    def gate_and_project(sc_scr, blk, vt_blk):
        for hf in range(2):
            row_blocks = []
            for c in range(half // N_KEYS):
                ci = hf * (half // N_KEYS) + c
                i1 = blk * n_i1 + ci
                s1_rows = [s1_scr[h, pl.ds(i1, 1), :] for h in range(P_HEADS)]
                e1_rows = [e1_scr[h, pl.ds(i1, 1), :] for h in range(P_HEADS)]
                col_blocks = []
                for tc in range(tn // LANES):
                    ln = slice(LANES * tc, LANES * (tc + 1))
                    gate = jnp.zeros((N_KEYS, LANES), F32)
                    for h in range(P_HEADS):
                        pair = s1_rows[h][:, ln] + s2_scr[h, :, ln]
                        gate = gate + jnp.where(pair >= tau_scr[h, :, ln],
                                                e1_rows[h][:, ln] * e2_scr[h, :, ln], 0.0)
                    act = _gelu(sc_scr[N_KEYS * ci:N_KEYS * (ci + 1), ln])
                    col_blocks.append((gate * act).astype(BF16))
                row_blocks.append(jnp.concatenate(col_blocks, axis=1))
            coef = jnp.concatenate(row_blocks, axis=0)
            acc_scr[...] += jnp.dot(vt_blk[:, half * hf:half * (hf + 1)], coef,
                                    preferred_element_type=F32)

    bufs = (sx_scr, sy_scr)
    for i in range(PEER_STEP_BLOCKS):
        following = ublk_ref[i + 1] if i + 1 < PEER_STEP_BLOCKS else unext_ref[0]
        bufs[(i + 1) % 2][...] = _dot_nt(following, hb)
        gate_and_project(bufs[i % 2], PEER_STEP_BLOCKS * j + i, vt_ref.at[i])

    @pl.when(j == pl.num_programs(1) - 1)
    def _():
        d = acc_scr.shape[0]
        gt = gt_ref[...]
        gt = jnp.broadcast_to(gt, (gt.shape[0], rep, d)).reshape(tn, d)
        x2 = x1_ref[...] + gt * acc_scr[...].T
        if final:
            x2 = _rms(x2) * gf_ref[...]
        out_ref[...] = x2


def _peer_call(h2, x1, gt, lp, g_final, *, tn, eb, final):
    b, t, d = x1.shape
    n = b * t
    assert n % tn == 0 and (t % tn == 0 or tn % t == 0)
    nbt = max(1, tn // t)
    tiles_per_b = max(1, t // tn)
    n_blk = lp["p_u"].shape[0]
    assert lp["p_u"].shape[1] == eb
    nsb = PEER_STEP_BLOCKS
    assert n_blk % nsb == 0 and nsb % 2 == 0
    kern = functools.partial(_peer_kernel, tn=tn, eb=eb, rep=tn // nbt, final=final)
    full = lambda a: pl.BlockSpec(a.shape, lambda ti, e: (0,) * a.ndim)
    tok = pl.BlockSpec((tn, d), lambda ti, e: (ti, 0))
    sel = pltpu.VMEM((P_HEADS, N_KEYS, tn), F32)
    blk_scores = pltpu.VMEM((eb, tn), F32)
    out = pl.pallas_call(
        kern,
        grid=(n // tn, n_blk // nsb),
        in_specs=[tok, tok,
                  pl.BlockSpec((nbt, 1, d), lambda ti, e: (ti // tiles_per_b, 0, 0)),
                  full(lp["p_wq"]), full(lp["p_bq"]), full(lp["p_keys"]),
                  pl.BlockSpec((nsb, eb, d), lambda ti, e: (e, 0, 0)),
                  pl.BlockSpec((1, eb, d),
                               lambda ti, e: (jnp.minimum(nsb * (e + 1), n_blk - 1), 0, 0)),
                  pl.BlockSpec((nsb, d, eb), lambda ti, e: (e, 0, 0)),
                  full(g_final)],
        out_specs=tok,
        out_shape=jax.ShapeDtypeStruct((n, d), F32),
        scratch_shapes=[sel, sel, sel, sel, pltpu.VMEM((P_HEADS, 1, tn), F32),
                        blk_scores, blk_scores, pltpu.VMEM((d, tn), F32)],
        compiler_params=_params("arbitrary", "arbitrary"),
        name="peer",
    )(h2.reshape(n, d), x1.reshape(n, d), gt, lp["p_wq"], lp["p_bq"], lp["p_keys"],
      lp["p_u"], lp["p_u"], lp["p_vt"], g_final)
    return out.reshape(b, t, d)


def _layer(x, mod, lp, vfirst, past, q_offset, g_final, final, tiles):
    b, t, d = x.shape
    sh_t, sc_t, gt_t, sh_c, sc_c, gt_c = (m[:, None, :] for m in jnp.split(mod, 6, axis=-1))
    qi, k, v, ik, iw, rw, gates = _inproj_call(
        x, sc_t, sh_t, lp["g_norm1"], lp["w_in"], lp["idx_k_g"], lp["idx_k_b"], tiles["tm"])

    if past is None:
        k_all, v_all, ik_all = k, v, ik
        m0 = jnp.zeros((b, RW_HEADS, RW_HEAD_DIM, RW_HEAD_DIM), F32)
        shift0 = jnp.zeros((b, 1, RW_PAD), F32)
    else:
        k_past, v_past, ik_past, s0, rw_prev = past
        pl_ = k_past.shape[1]
        k_all = jnp.concatenate([k_past.reshape(b, pl_, KV_W), k], axis=1)
        v_all = jnp.concatenate([v_past.reshape(b, pl_, KV_W), v], axis=1)
        ik_all = jnp.concatenate([ik_past, ik], axis=1)
        m0 = jnp.swapaxes(s0, -1, -2)
        shift0 = jnp.pad(rw_prev, ((0, 0), (0, 0), (0, RW_PAD - RW_COLS)))
    tq = tiles["tq"]
    t_pad = -(-t // tq) * tq
    qpad = ((0, 0), (0, t_pad - t), (0, 0))
    o_a = _dsa_call(jnp.pad(qi, qpad), jnp.pad(iw, qpad), k_all, v_all, ik_all,
                    q_offset=q_offset, tq=tq, tk=tiles["tk"])[:, :t]

    r, lw, k2, v2, kkn, bb, g, bonus = _rwprep_call(rw, shift0, lp, vfirst, tiles["tm"])
    if vfirst is None:
        vfirst = v2
    o_scan, m_new = _scan_call(r, lw, k2, v2, kkn, bb, m0, tiles["c"])

    x1, h2 = _merge_call(x, o_a, o_scan, bonus, g, gates, gt_t, sc_c, sh_c, lp, tiles["tm"])
    x2 = _peer_call(h2, x1, gt_c, lp, g_final, tn=tiles["tn"], eb=tiles["eb"], final=final)
    state = (k.reshape(b, t, KV_HEADS, HEAD_DIM), v.reshape(b, t, KV_HEADS, HEAD_DIM), ik,
             jnp.swapaxes(m_new, -1, -2), rw[:, -1:, :RW_COLS])
    return x2, vfirst, state


def _tiles(t):
    return {"tm": min(t, 512), "tq": 2 * LANES if t % (2 * LANES) == 0 else LANES,
            "tk": 256, "c": min(t, CHUNK),
            "tn": 512 if t >= 512 else LANES, "eb": PEER_EB}


def kernel(x_prompt, x_sample, cache_k, cache_v, cache_kidx, state_wkv, state_shift, c_prompt, c_sample, w_ada, b_ada, g_norm1, w_in, idx_k_g, idx_k_b, rw_mu, rw_w0, rw_w_up, rw_a0, rw_a_up, rw_g_up, rw_k_k, rw_k_a, rw_r_k, rw_lnx_g, rw_lnx_b, rw_v0, rw_v_down, rw_v_up, w_oa, w_ob, w_out, g_norm2, peer_wq, peer_bq, peer_sub_keys, peer_u, peer_v, g_final):
    depth = w_in.shape[0]
    nbp = x_prompt.shape[0]
    past_len = cache_k.shape[2]
    bd = _head_block_diag()
    row = lambda a: a.reshape(1, -1)
    xp, xs = x_prompt, x_sample
    vf_p, vf_s = None, None
    new_p, new_s = [], []
    c_all = jnp.concatenate([c_prompt, c_sample], axis=0)
    gf = row(g_final)
    for l in range(depth):
        lp = {
            "g_norm1": row(g_norm1[l]), "w_in": _pack_w_in(w_in[l]),
            "idx_k_g": row(idx_k_g[l]), "idx_k_b": row(idx_k_b[l]),
            "mu": jnp.pad(row(rw_mu[l]), ((0, 0), (0, RW_PAD - RW_COLS))),
            "w0": row(rw_w0[l]), "a0": row(rw_a0[l]),
            "w_up": _pad_rows(rw_w_up[l], 0, LANES),
            "a_up": _pad_rows(rw_a_up[l], W_LORA, LANES),
            "g_up": _pad_rows(rw_g_up[l], 0, RW_PAD - 3 * RW_WIDTH - LANES),
            "k_k": row(rw_k_k[l]), "k_a": row(rw_k_a[l]), "r_k": row(rw_r_k[l]),
            "lnx_g": row(rw_lnx_g[l]), "lnx_b": row(rw_lnx_b[l]), "bd": bd,
            "w_oa": w_oa[l].astype(BF16), "w_ob": w_ob[l].astype(BF16),
            "w_out": w_out[l].astype(BF16), "g_norm2": row(g_norm2[l]),
            "p_wq": peer_wq[l].astype(BF16), "p_bq": row(peer_bq[l]),
            "p_keys": peer_sub_keys[l].reshape(2 * P_HEADS, N_KEYS, P_HALF).astype(BF16),
            "p_u": peer_u[l].reshape(-1, PEER_EB, peer_u.shape[-1]).astype(BF16),
            "p_vt": jnp.swapaxes(peer_v[l].reshape(-1, PEER_EB, peer_v.shape[-1]), 1, 2
                                 ).astype(BF16),
        }
        if l > 0:
            lp["v0"] = row(rw_v0[l - 1])
            lp["v_down"] = jnp.pad(rw_v_down[l - 1], ((0, 0), (0, LANES - V_LORA))).astype(BF16)
            lp["v_up"] = _pad_rows(rw_v_up[l - 1], 0, LANES)
        mod = _mod_call(c_all, w_ada[l], b_ada[l])
        final = l == depth - 1
        xp, vf_p, st_p = _layer(xp, mod[:nbp], lp, vf_p, None, 0, gf, final,
                                _tiles(xp.shape[1]))
        past = (cache_k[l], cache_v[l], cache_kidx[l], state_wkv[l], state_shift[l])
        xs, vf_s, st_s = _layer(xs, mod[nbp:], lp, vf_s, past, past_len, gf, final,
                                _tiles(xs.shape[1]))
        new_p.append(st_p)
        new_s.append(st_s)

    def stk(lst, i):
        return jnp.stack([e[i] for e in lst], axis=0)

    return (xp, xs,
            stk(new_p, 0), stk(new_p, 1), stk(new_p, 2), stk(new_p, 3), stk(new_p, 4),
            stk(new_s, 0), stk(new_s, 1), stk(new_s, 2), stk(new_s, 3), stk(new_s, 4))
```

```python
import functools

import numpy as np
import jax
import jax.numpy as jnp
from jax import lax
from jax.experimental import pallas as pl
from jax.experimental.pallas import tpu as pltpu

F32 = jnp.float32
BF16 = jnp.bfloat16
I32 = jnp.int32
HIGHEST = lax.Precision.HIGHEST

CHUNK = 64
N_HEADS = 8
HEAD_DIM = 64
KV_HEADS = 2
GROUP = N_HEADS // KV_HEADS
IDX_HEADS = 8
IDX_DIM = 64
DSA_TOPK = 256
ATTN_SCALE = HEAD_DIM ** -0.5
RW_HEADS = 8
RW_HEAD_DIM = 64
RW_WIDTH = RW_HEADS * RW_HEAD_DIM
W_LORA = 64
A_LORA = 64
V_LORA = 32
G_LORA = 160
RW_COLS = 3 * RW_WIDTH + W_LORA + A_LORA + G_LORA
N_KEYS = 128
P_HEADS = 8
P_HALF = 128
P_TOPK = 16
PEER_EB = 512
PEER_STEP_BLOCKS = 4
EPS = 1e-6
GN_EPS = 64e-5

LANES = 128
SUBLANES = 8
VMEM_LIMIT = 56 * 1024 * 1024

DSA_Q = N_HEADS * HEAD_DIM
IDX_Q = IDX_HEADS * IDX_DIM
KV_W = KV_HEADS * HEAD_DIM
QI_W = DSA_Q + IDX_Q
SMALL_W = 4 * LANES
RW_PAD = 15 * LANES
NEG = -1e30


def _params(*sem):
    return pltpu.CompilerParams(dimension_semantics=sem, vmem_limit_bytes=VMEM_LIMIT)


def _bdot(a, b):
    return jnp.dot(a.astype(BF16), b.astype(BF16), preferred_element_type=F32)


def _hdot(a, b):
    return jnp.dot(a, b, precision=HIGHEST, preferred_element_type=F32)


def _dot_nt(a, b, precision=None):
    return lax.dot_general(a, b, (((1,), (1,)), ((), ())), precision=precision,
                           preferred_element_type=F32)


def _dot_tn(a, b, precision=None):
    return lax.dot_general(a, b, (((0,), (0,)), ((), ())), precision=precision,
                           preferred_element_type=F32)


def _sigmoid(x):
    return 1.0 / (1.0 + jnp.exp(-x))


def _rms(x):
    return x * lax.rsqrt(jnp.mean(x * x, axis=-1, keepdims=True) + EPS)


def _mod_kernel(c_ref, w_ref, b_ref, o_ref):
    c = c_ref[...]
    o_ref[...] = _bdot(c * _sigmoid(c), w_ref[...]) + b_ref[...]


def _mod_call(c_all, w_ada, b_ada):
    nb, d = c_all.shape
    ncol = w_ada.shape[1] // d
    return pl.pallas_call(
        _mod_kernel,
        grid=(ncol,),
        in_specs=[pl.BlockSpec((nb, d), lambda j: (0, 0)),
                  pl.BlockSpec((d, d), lambda j: (0, j)),
                  pl.BlockSpec((1, d), lambda j: (0, j))],
        out_specs=pl.BlockSpec((nb, d), lambda j: (0, j)),
        out_shape=jax.ShapeDtypeStruct((nb, ncol * d), F32),
        compiler_params=_params("arbitrary"),
        name="adaln_mod",
    )(c_all, w_ada, b_ada.reshape(1, -1))


def _inproj_kernel(x_ref, sc_ref, sh_ref, g_ref, w_ref, ikg_ref, ikb_ref,
                   qi_ref, k_ref, v_ref, ik_ref, iw_ref, rw_ref, gates_ref):
    x = x_ref[0]
    h = (_rms(x) * g_ref[...]) * (1.0 + sc_ref[0]) + sh_ref[0]
    hb = h.astype(BF16)
    o0 = QI_W
    o1 = o0 + SMALL_W
    o2 = o1 + RW_PAD
    qi_ref[0] = jnp.dot(hb, w_ref[:, 0:o0], preferred_element_type=F32)
    small = jnp.dot(hb, w_ref[:, o0:o1], preferred_element_type=F32)
    k_ref[0] = small[:, 0:LANES]
    v_ref[0] = small[:, LANES:2 * LANES]
    ik = small[:, 2 * LANES:2 * LANES + IDX_DIM]
    ikc = ik - jnp.mean(ik, axis=-1, keepdims=True)
    ikn = ikc * lax.rsqrt(jnp.mean(ikc * ikc, axis=-1, keepdims=True) + EPS)
    ik_ref[0] = ikn * ikg_ref[...] + ikb_ref[...]
    iw_ref[0] = small[:, 3 * LANES:3 * LANES + IDX_HEADS]
    rw_ref[0] = jnp.dot(hb, w_ref[:, o1:o2], preferred_element_type=F32)
    gates_ref[0] = jnp.dot(hb, w_ref[:, o2:], preferred_element_type=F32)


def _pack_w_in(w_in):
    d = w_in.shape[0]
    offs = np.cumsum([0, DSA_Q, KV_W, KV_W, IDX_Q, IDX_DIM, IDX_HEADS, RW_COLS, 2 * d])
    q, k, v, iq, ik, iw, rw, gates = (w_in[:, offs[i]:offs[i + 1]] for i in range(8))
    z = lambda n: jnp.zeros((d, n), w_in.dtype)
    packed = jnp.concatenate(
        [q, iq, k, v, ik, z(LANES - IDX_DIM), iw, z(LANES - IDX_HEADS),
         rw, z(RW_PAD - RW_COLS), gates], axis=1)
    return packed.astype(BF16)


def _inproj_call(x, sc, sh, g1, w_packed, ikg, ikb, tm):
    b, t, d = x.shape
    nw = w_packed.shape[1]
    tok = lambda n: pl.BlockSpec((1, tm, n), lambda bi, i: (bi, i, 0))
    row = lambda n: pl.BlockSpec((1, n), lambda bi, i: (0, 0))
    per_b = pl.BlockSpec((1, 1, d), lambda bi, i: (bi, 0, 0))
    widths = (QI_W, LANES, LANES, IDX_DIM, IDX_HEADS, RW_PAD, 2 * d)
    return pl.pallas_call(
        _inproj_kernel,
        grid=(b, t // tm),
        in_specs=[tok(d), per_b, per_b, row(d),
                  pl.BlockSpec((d, nw), lambda bi, i: (0, 0)),
                  row(IDX_DIM), row(IDX_DIM)],
        out_specs=[tok(n) for n in widths],
        out_shape=[jax.ShapeDtypeStruct((b, t, n), F32) for n in widths],
        compiler_params=_params("arbitrary", "arbitrary"),
        name="norm_inproj",
    )(x, sc, sh, g1, w_packed, ikg, ikb)


def _dsa_kernel(qi_ref, iwt_ref, k_ref, vt_ref, ik_ref, o_ref,
                key_scr, bias_scr, iq_scr, qg_scr, *, tq, tk, l_valid, q_offset, topk):
    qb = pl.program_id(1)
    int_min = jnp.int32(-2 ** 31)
    q0 = q_offset + qb * tq
    last_chunk = (q0 + tq - 1) // CHUNK
    n_adm = jnp.minimum((last_chunk + 1) * CHUNK, l_valid)
    n_kt = (n_adm + tk - 1) // tk

    x = qi_ref[0]
    for h in range(IDX_HEADS):
        iq_scr[h] = x[:, DSA_Q + IDX_DIM * h:DSA_Q + IDX_DIM * (h + 1)].astype(BF16)
    for g in range(KV_HEADS):
        for r in range(GROUP):
            h = GROUP * g + r
            qg_scr[g, r * tq:(r + 1) * tq, :] = (
                x[:, HEAD_DIM * h:HEAD_DIM * (h + 1)] * ATTN_SCALE).astype(BF16)
    iwt = iwt_ref[0]
    q_chunk = (q0 + lax.broadcasted_iota(I32, (1, tq), 1)) // CHUNK
    row_iota = lax.broadcasted_iota(I32, (tk, tq), 0)

    def tile_base(kt):
        return pl.multiple_of(kt * tk, tk)

    def score_body(kt, carry):
        base = tile_base(kt)
        ikt = ik_ref[0, pl.ds(base, tk), :].astype(BF16)
        acc = jnp.zeros((tk, tq), F32)
        for h in range(IDX_HEADS):
            acc = acc + iwt[h:h + 1, :] * jnp.maximum(_dot_nt(ikt, iq_scr[h]), 0.0)
        acc = jnp.where(acc == 0.0, 0.0, acc)
        bits = lax.bitcast_convert_type(acc, I32)
        key = jnp.where(bits < 0, bits ^ jnp.int32(0x7FFFFFFF), bits)
        kpos = base + row_iota
        adm = (kpos < l_valid) & ((kpos // CHUNK) <= q_chunk)
        key_scr[pl.ds(base, tk), :] = jnp.where(adm, key, int_min)
        return carry

    lax.fori_loop(0, n_kt, score_body, 0)

    acc_rows = 4 * SUBLANES

    def count(pred_fn):
        def body(kt, c):
            base = tile_base(kt)
            m = jnp.where(pred_fn(key_scr[pl.ds(base, tk), :], base + row_iota), 1, 0)
            return c + jnp.sum(m.reshape(tk // acc_rows, acc_rows, tq), axis=0)
        c = lax.fori_loop(0, n_kt // 2, lambda i, c: body(2 * i + 1, body(2 * i, c)),
                          jnp.zeros((acc_rows, tq), I32))
        c = lax.fori_loop(2 * (n_kt // 2), n_kt, body, c)
        return jnp.sum(c, axis=0, keepdims=True)

    def bit_body(i, tb):
        cand_b = tb | lax.shift_left(jnp.int32(1), 31 - i)
        cand = cand_b ^ int_min
        cnt = count(lambda kk, idx: kk >= cand)
        return jnp.where(cnt >= topk, cand_b, tb)

    tau = lax.fori_loop(0, 32, bit_body, jnp.zeros((1, tq), I32)) ^ int_min
    cnt_ge = count(lambda kk, idx: kk >= tau)
    cnt_gt = count(lambda kk, idx: kk > tau)
    need = topk - cnt_gt
    excess = (tau > int_min) & (cnt_ge - cnt_gt > need)
    any_excess = jnp.max(jnp.where(excess, 1, 0)) > 0

    idx_bits = 13
    def tie_limit():
        def jbody(i, j):
            cand_j = j | lax.shift_left(jnp.int32(1), idx_bits - 1 - i)
            f = count(lambda kk, idx: (kk == tau) & (idx < cand_j))
            return jnp.where(f <= need, cand_j, j)
        return lax.fori_loop(0, idx_bits, jbody, jnp.zeros((1, tq), I32))

    j_lim = lax.cond(any_excess, tie_limit,
                     lambda: jnp.full((1, tq), 2 ** idx_bits - 1, I32))

    def bias_body(kt, carry):
        base = tile_base(kt)
        kk = key_scr[pl.ds(base, tk), :]
        sel = (kk > tau) | ((kk == tau) & ((base + row_iota) < j_lim))
        sel = sel & (kk != int_min)
        bias_scr[pl.ds(base, tk), :] = jnp.where(sel, 0.0, NEG)
        return carry

    lax.fori_loop(0, n_kt, bias_body, 0)

    def attn_body(kt, carry):
        base = tile_base(kt)
        bias = bias_scr[pl.ds(base, tk), :]
        k_all = k_ref[0, pl.ds(base, tk), :]
        vt_all = vt_ref[0, kt]
        new = []
        s_groups = [_dot_nt(k_all[:, HEAD_DIM * g:HEAD_DIM * (g + 1)].astype(BF16), qg_scr[g])
                    for g in range(KV_HEADS)]
        for g in range(KV_HEADS):
            s_all = s_groups[g]
            ps, stats = [], []
            for r in range(GROUP):
                m, l, acc = carry[GROUP * g + r]
                s = s_all[:, r * tq:(r + 1) * tq] + bias
                m_new = jnp.maximum(m, jnp.max(s, axis=0, keepdims=True))
                alpha = jnp.exp(m - m_new)
                p = jnp.exp(s - m_new)
                ps.append(p.astype(BF16))
                stats.append((m_new, l * alpha + jnp.sum(p, axis=0, keepdims=True), alpha, acc))
            pv = jnp.dot(vt_all[HEAD_DIM * g:HEAD_DIM * (g + 1), :].astype(BF16),
                         jnp.concatenate(ps, axis=1), preferred_element_type=F32)
            for r, (m_new, l_new, alpha, acc) in enumerate(stats):
                new.append((m_new, l_new, acc * alpha + pv[:, r * tq:(r + 1) * tq]))
        return tuple(new)

    init = tuple((jnp.full((1, tq), NEG, F32), jnp.zeros((1, tq), F32),
                  jnp.zeros((HEAD_DIM, tq), F32)) for _ in range(N_HEADS))
    fin = lax.fori_loop(0, n_kt, attn_body, init)
    o_ref[0] = jnp.concatenate([acc / l for _, l, acc in fin], axis=0).T


def _dsa_call(qi, iw, k_all, v_all, ik_all, *, q_offset, tq, tk):
    b, t, _ = qi.shape
    l_valid = k_all.shape[1]
    topk = min(DSA_TOPK, l_valid // 4)
    assert topk <= tk and t % tq == 0
    l_pad = -(-l_valid // tk) * tk
    assert l_pad < 2 ** 13 - 1
    pad = ((0, 0), (0, l_pad - l_valid), (0, 0))
    k_p, v_p, ik_p = (jnp.pad(a, pad) for a in (k_all, v_all, ik_all))
    nkt = l_pad // tk
    vt = jnp.swapaxes(v_p.reshape(b, nkt, tk, KV_W), 2, 3)
    iwt = jnp.swapaxes(iw, 1, 2)
    kern = functools.partial(_dsa_kernel, tq=tq, tk=tk, l_valid=l_valid,
                             q_offset=q_offset, topk=topk)
    return pl.pallas_call(
        kern,
        grid=(b, t // tq),
        in_specs=[pl.BlockSpec((1, tq, QI_W), lambda bi, i: (bi, i, 0)),
                  pl.BlockSpec((1, IDX_HEADS, tq), lambda bi, i: (bi, 0, i)),
                  pl.BlockSpec((1, l_pad, KV_W), lambda bi, i: (bi, 0, 0)),
                  pl.BlockSpec((1, nkt, KV_W, tk), lambda bi, i: (bi, 0, 0, 0)),
                  pl.BlockSpec((1, l_pad, IDX_DIM), lambda bi, i: (bi, 0, 0))],
        out_specs=pl.BlockSpec((1, tq, DSA_Q), lambda bi, i: (bi, i, 0)),
        out_shape=jax.ShapeDtypeStruct((b, t, DSA_Q), F32),
        scratch_shapes=[pltpu.VMEM((l_pad, tq), I32), pltpu.VMEM((l_pad, tq), F32),
                        pltpu.VMEM((IDX_HEADS, tq, IDX_DIM), BF16),
                        pltpu.VMEM((KV_HEADS, GROUP * tq, HEAD_DIM), BF16)],
        compiler_params=_params("arbitrary", "arbitrary"),
        name="dsa_attention",
    )(qi, iwt, k_p, vt, ik_p)


def _rwprep_kernel(*refs, has_vfirst):
    (rw_ref, prev8_ref, shift0_ref, mu_ref, w0_ref, a0_ref, wup_ref, aup_ref, gup_ref,
     kk_ref, ka_ref, rk_ref, bd_ref) = refs[:13]
    if has_vfirst:
        vfirst_ref, v0_ref, vdown_ref, vup_ref = refs[13:17]
        outs = refs[17:]
    else:
        outs = refs[13:]
    r_o, lw_o, k_o, v_o, kkn_o, b_o, g_o, bonus_o = outs
    i = pl.program_id(1)
    rw = rw_ref[0]
    prev = jnp.where(i == 0, shift0_ref[0], prev8_ref[0][SUBLANES - 1:SUBLANES, :])
    row = lax.broadcasted_iota(I32, rw.shape, 0)
    shifted = jnp.where(row == 0, prev, pltpu.roll(rw, 1, 0))
    mix = rw + mu_ref[...] * (shifted - rw)
    w3 = RW_WIDTH
    r = mix[:, 0:w3]
    kr = mix[:, w3:2 * w3]
    vr = mix[:, 2 * w3:3 * w3]
    wa = mix[:, 3 * w3:3 * w3 + LANES]
    gd = mix[:, 3 * w3 + LANES:]
    z = w0_ref[...] + _bdot(jnp.tanh(wa), wup_ref[...])
    nz = -z
    softplus = jnp.maximum(nz, 0.0) + jnp.log(1.0 + jnp.exp(-jnp.abs(nz)))
    lw = -jnp.exp(-softplus - 0.5)
    a = _sigmoid(a0_ref[...] + _bdot(wa, aup_ref[...]))
    g = _bdot(_sigmoid(gd), gup_ref[...])
    if has_vfirst:
        lora = _bdot(_bdot(vr, vdown_ref[...]), vup_ref[...])
        vr = vr + (vfirst_ref[0] - vr) * _sigmoid(v0_ref[...] + lora)
    bd = bd_ref[...]
    kkr = kr * kk_ref[...]
    kkn = kkr / jnp.maximum(jnp.sqrt(_hdot(kkr * kkr, bd)), 1e-12)
    k2 = kr * (1.0 + (a - 1.0) * ka_ref[...])
    r_o[0] = r
    lw_o[0] = lw
    k_o[0] = k2
    v_o[0] = vr
    kkn_o[0] = kkn
    b_o[0] = kkn * a
    g_o[0] = g
    bonus_o[0] = _hdot(r * k2 * rk_ref[...], bd) * vr


def _head_block_diag():
    h = np.arange(RW_WIDTH) // RW_HEAD_DIM
    return jnp.asarray((h[:, None] == h[None, :]).astype(np.float32))


def _pad_rows(w, lo, total):
    return jnp.pad(w, ((lo, total - lo - w.shape[0]), (0, 0))).astype(BF16)


def _rwprep_call(rw, shift0, lp, vfirst, tm):
    b, t, _ = rw.shape
    has_vfirst = vfirst is not None
    tok = lambda n: pl.BlockSpec((1, tm, n), lambda bi, i: (bi, i, 0))
    full = lambda a: pl.BlockSpec(a.shape, lambda bi, i: (0,) * a.ndim)
    consts = [lp["mu"], lp["w0"], lp["a0"], lp["w_up"], lp["a_up"], lp["g_up"],
              lp["k_k"], lp["k_a"], lp["r_k"], lp["bd"]]
    args = [rw, rw, shift0] + consts
    in_specs = [tok(RW_PAD),
                pl.BlockSpec((1, SUBLANES, RW_PAD),
                             lambda bi, i: (bi, jnp.maximum(i * (tm // SUBLANES) - 1, 0), 0)),
                pl.BlockSpec((1, 1, RW_PAD), lambda bi, i: (bi, 0, 0))]
    in_specs += [full(a) for a in consts]
    if has_vfirst:
        extra = [lp["v0"], lp["v_down"], lp["v_up"]]
        args += [vfirst] + extra
        in_specs += [tok(RW_WIDTH)] + [full(a) for a in extra]
    return pl.pallas_call(
        functools.partial(_rwprep_kernel, has_vfirst=has_vfirst),
        grid=(b, t // tm),
        in_specs=in_specs,
        out_specs=[tok(RW_WIDTH)] * 8,
        out_shape=[jax.ShapeDtypeStruct((b, t, RW_WIDTH), F32)] * 8,
        compiler_params=_params("arbitrary", "arbitrary"),
        name="rwkv_prep",
    )(*args)


def _scan_intra_kernel(r_ref, lw_ref, k_ref, v_ref, kk_ref, b_ref,
                       r2_ref, oi_ref, a_ref, d_ref, *, c, nc):
    row = lax.broadcasted_iota(I32, (c, c), 0)
    col = lax.broadcasted_iota(I32, (c, c), 1)
    incl = row >= col
    strict = row > col
    eye_c = jnp.where(row == col, 1.0, 0.0)
    ones_incl = jnp.where(incl, 1.0, 0.0)
    n = RW_HEAD_DIM
    rn = lax.broadcasted_iota(I32, (n, n), 0)
    cn = lax.broadcasted_iota(I32, (n, n), 1)
    nlev = int(np.log2(c))

    prep = []
    for ci in range(nc):
        rows = slice(ci * c, (ci + 1) * c)
        lw, k, b = lw_ref[0, rows, :], k_ref[0, rows, :], b_ref[0, rows, :]
        cum = _hdot(ones_incl, lw)
        total = cum[c - 1:c, :]
        g_inv = jnp.exp(-cum)
        g_rem = jnp.exp(total - cum)
        prep.append(dict(
            alpha=kk_ref[0, rows, :] * jnp.exp(cum - lw), beta=b * g_inv, kappa=k * g_inv,
            rho=r_ref[0, rows, :] * jnp.exp(cum), khat=k * g_rem, bhat=b * g_rem,
            g_tot=jnp.exp(total), v=v_ref[0, rows, :]))
    units = [(ci, h) for ci in range(nc) for h in range(RW_HEADS)]

    def head(ci, h, name):
        return prep[ci][name][:, n * h:n * (h + 1)]

    grams = [_dot_nt(
        jnp.concatenate([head(ci, h, "alpha"), head(ci, h, "rho")], axis=0).astype(BF16),
        jnp.concatenate([head(ci, h, "beta"), head(ci, h, "kappa")], axis=0).astype(BF16))
        for ci, h in units]
    l_ak = [jnp.where(strict, g[:c, c:], 0.0) for g in grams]
    l_rb = [jnp.where(incl, g[c:, :c], 0.0) for g in grams]
    l_rk = [jnp.where(incl, g[c:, c:], 0.0) for g in grams]
    ps = [-jnp.where(strict, g[:c, :c], 0.0) for g in grams]
    tinvs = [eye_c + p for p in ps]
    for _ in range(nlev - 1):
        ps = [_bdot(p, p) for p in ps]
        tinvs = [t + _bdot(t, p) for t, p in zip(tinvs, ps)]
    lvs = [_bdot(jnp.concatenate([ak, rk], axis=0), head(ci, h, "v"))
           for ak, rk, (ci, h) in zip(l_ak, l_rk, units)]
    wys = [_hdot(t, jnp.concatenate([head(ci, h, "alpha"), lv[:c]], axis=1))
           for t, lv, (ci, h) in zip(tinvs, lvs, units)]
    rbs = [_bdot(rb, wy) for rb, wy in zip(l_rb, wys)]
    bws = [_dot_tn(head(ci, h, "bhat"), wy, HIGHEST) for wy, (ci, h) in zip(wys, units)]
    kvs = [_dot_tn(head(ci, h, "khat"), head(ci, h, "v"), HIGHEST) for ci, h in units]
    for ci in range(nc):
        rows = slice(ci * c, (ci + 1) * c)
        mine = [u for u, (cj, _) in enumerate(units) if cj == ci]
        r2_ref[0, rows, :] = jnp.concatenate(
            [head(ci, h, "rho") - rbs[u][:, :n] for h, u in enumerate(mine)], axis=1)
        oi_ref[0, rows, :] = jnp.concatenate(
            [lvs[u][c:] - rbs[u][:, n:] for u in mine], axis=1)
        for h, u in enumerate(mine):
            dg = jnp.where(rn == cn,
                           jnp.broadcast_to(prep[ci]["g_tot"][:, n * h:n * (h + 1)], (n, n)), 0.0)
            a_ref[0, ci, h] = dg - bws[u][:, :n]
            d_ref[0, ci, h] = kvs[u] - bws[u][:, n:]


def _scan_inter_kernel(r2_ref, oi_ref, a_ref, d_ref, m0_ref, o_ref, mout_ref, m_scr, *, nb):
    ci = pl.program_id(1)

    @pl.when(ci == 0)
    def _():
        m_scr[...] = m0_ref[...]

    n = RW_HEAD_DIM
    for bi in range(nb):
        r2 = r2_ref[bi]
        outs = []
        for h in range(RW_HEADS):
            m0 = m_scr[bi, h]
            outs.append(_hdot(r2[:, n * h:n * (h + 1)], m0))
            m_scr[bi, h] = _hdot(a_ref[bi, 0, h], m0) + d_ref[bi, 0, h]
        o_ref[bi] = jnp.concatenate(outs, axis=1) + oi_ref[bi]

    @pl.when(ci == pl.num_programs(1) - 1)
    def _():
        mout_ref[...] = m_scr[...]


def _scan_call(r, lw, k, v, kk, bb, m0, c):
    b, t, w = r.shape
    nch = t // c
    nc = next(n for n in (4, 2, 1) if nch % n == 0)
    nb = next(n for n in (4, 2, 1) if b % n == 0)
    hd = RW_HEAD_DIM
    tok = pl.BlockSpec((1, nc * c, w), lambda bi, i: (bi, i, 0))
    mats = pl.BlockSpec((1, nc, RW_HEADS, hd, hd), lambda bi, i: (bi, i, 0, 0, 0))
    r2, oi, a_mat, d_mat = pl.pallas_call(
        functools.partial(_scan_intra_kernel, c=c, nc=nc),
        grid=(b, nch // nc),
        in_specs=[tok] * 6,
        out_specs=[tok, tok, mats, mats],
        out_shape=[jax.ShapeDtypeStruct((b, t, w), F32)] * 2
        + [jax.ShapeDtypeStruct((b, nch, RW_HEADS, hd, hd), F32)] * 2,
        compiler_params=_params("arbitrary", "arbitrary"),
        name="rwkv_chunk_terms",
    )(r, lw, k, v, kk, bb)
    tok_b = pl.BlockSpec((nb, c, w), lambda bi, i: (bi, i, 0))
    mat_b = pl.BlockSpec((nb, 1, RW_HEADS, hd, hd), lambda bi, i: (bi, i, 0, 0, 0))
    st = pl.BlockSpec((nb, RW_HEADS, hd, hd), lambda bi, i: (bi, 0, 0, 0))
    return pl.pallas_call(
        functools.partial(_scan_inter_kernel, nb=nb),
        grid=(b // nb, nch),
        in_specs=[tok_b, tok_b, mat_b, mat_b, st],
        out_specs=[tok_b, st],
        out_shape=[jax.ShapeDtypeStruct((b, t, w), F32),
                   jax.ShapeDtypeStruct(m0.shape, F32)],
        scratch_shapes=[pltpu.VMEM((nb, RW_HEADS, hd, hd), F32)],
        compiler_params=_params("arbitrary", "arbitrary"),
        name="rwkv_scan",
    )(r2, oi, a_mat, d_mat, m0)


def _merge_kernel(x_ref, oa_ref, os_ref, bonus_ref, g_ref, gates_ref, gt_ref, sc_ref, sh_ref,
                  woa_ref, wob_ref, wout_ref, lng_ref, lnb_ref, gn2_ref, bd_ref,
                  x1_ref, h2_ref):
    bd = bd_ref[...]
    inv_n = 1.0 / RW_HEAD_DIM
    o = os_ref[0]
    oc = o - _hdot(o, bd) * inv_n
    var = _hdot(oc * oc, bd) * inv_n
    y = oc * lax.rsqrt(var + GN_EPS) * lng_ref[...] + lnb_ref[...]
    ob = (y + bonus_ref[0]) * g_ref[0]
    gates = gates_ref[0]
    d = x_ref.shape[-1]
    merged = (_sigmoid(gates[:, :d]) * _bdot(oa_ref[0], woa_ref[...])
              + _sigmoid(gates[:, d:]) * _bdot(ob, wob_ref[...]))
    x1 = x_ref[0] + gt_ref[0] * _bdot(merged, wout_ref[...])
    x1_ref[0] = x1
    h2 = (_rms(x1) * gn2_ref[...]) * (1.0 + sc_ref[0]) + sh_ref[0]
    h2_ref[0] = h2.astype(BF16)


def _merge_call(x, oa, o_scan, bonus, g, gates, gt, sc, sh, lp, tm):
    b, t, d = x.shape
    tok = lambda n: pl.BlockSpec((1, tm, n), lambda bi, i: (bi, i, 0))
    per_b = pl.BlockSpec((1, 1, d), lambda bi, i: (bi, 0, 0))
    full = lambda a: pl.BlockSpec(a.shape, lambda bi, i: (0,) * a.ndim)
    consts = [lp["w_oa"], lp["w_ob"], lp["w_out"], lp["lnx_g"], lp["lnx_b"], lp["g_norm2"],
              lp["bd"]]
    return pl.pallas_call(
        _merge_kernel,
        grid=(b, t // tm),
        in_specs=[tok(d), tok(DSA_Q), tok(RW_WIDTH), tok(RW_WIDTH), tok(RW_WIDTH), tok(2 * d),
                  per_b, per_b, per_b] + [full(a) for a in consts],
        out_specs=[tok(d), tok(d)],
        out_shape=[jax.ShapeDtypeStruct((b, t, d), F32), jax.ShapeDtypeStruct((b, t, d), BF16)],
        compiler_params=_params("arbitrary", "arbitrary"),
        name="merge_out",
    )(x, oa, o_scan, bonus, g, gates, gt, sc, sh, *consts)


def _kth_largest_rows(x, kth):
    work = x
    cnt = jnp.zeros((1, x.shape[1]), F32)
    tau = jnp.full((1, x.shape[1]), -jnp.inf, F32)
    for _ in range(kth):
        mx = jnp.max(work, axis=0, keepdims=True)
        eq = work == mx
        tau = jnp.where(cnt < kth, mx, tau)
        cnt = cnt + jnp.sum(jnp.where(eq, 1.0, 0.0), axis=0, keepdims=True)
        work = jnp.where(eq, -jnp.inf, work)
    return tau


def _top_rows(x, kth):
    work = x
    rank = jnp.full(x.shape, float(kth), F32)
    tops = []
    for r in range(kth):
        mx = jnp.max(work, axis=0, keepdims=True)
        tops.append(mx)
        hit = work == mx
        rank = jnp.where(hit, float(r), rank)
        work = jnp.where(hit, -jnp.inf, work)
    return tops, rank


def _gelu(x):
    return 0.5 * x * (1.0 + lax.erf(x * (2.0 ** -0.5)))


def _peer_kernel(h2_ref, x1_ref, gt_ref, wq_ref, bq_ref, keys_ref, ublk_ref, unext_ref,
                 vt_ref, gf_ref, out_ref, lim_scr, e1_scr, rk2_scr, e2_scr, row_scr,
                 sx_scr, sy_scr, acc_scr, *, tn, eb, rep, final):
    j = pl.program_id(1)
    hb = h2_ref[...]

    @pl.when(j == 0)
    def _():
        sx_scr[...] = _dot_nt(ublk_ref[0], hb)
        q = (jnp.dot(hb, wq_ref[...], preferred_element_type=F32) + bq_ref[...]).astype(BF16)
        for h in range(P_HEADS):
            halves = []
            for c in range(2):
                hc = 2 * h + c
                s = _dot_nt(keys_ref[hc], q[:, P_HALF * hc:P_HALF * (hc + 1)])
                halves.append((s,) + _top_rows(s, P_TOPK))
            (s1, m1, _), (s2, m2, rank2) = halves
            m1s = jnp.concatenate(m1, axis=0)
            m2s = jnp.concatenate(m2, axis=0)
            hk = P_TOPK // 2
            m2lo = m2s[:hk]
            skip2 = lax.broadcasted_iota(I32, m2lo.shape, 0) < 2
            cand = jnp.concatenate(
                [m1s + m2[0], m1s[:hk] + m2[1], m2s[hk:] + m1[0]]
                + [jnp.where(skip2, -jnp.inf, m2lo + m1[r1]) for r1 in range(5)], axis=0)
            tau = _kth_largest_rows(cand, P_TOPK)
            z = jnp.sum(jnp.where(cand >= tau, jnp.exp(cand - (m1[0] + m2[0])), 0.0),
                        axis=0, keepdims=True)
            s1_top = jnp.where(s1 >= m1[P_TOPK - 1], s1, -jnp.inf)
            lim = jnp.zeros_like(s1)
            for r2 in range(P_TOPK):
                lim = lim + jnp.where(s1_top + m2[r2] >= tau, 1.0, 0.0)
            lim_scr[h] = lim
            rk2_scr[h] = rank2.astype(BF16)
            e1_scr[h] = jnp.exp(s1 - m1[0])
            e2_scr[h] = (jnp.exp(s2 - m2[0]) / z).astype(BF16)
        acc_scr[...] = jnp.zeros_like(acc_scr)

    n_i1 = eb // N_KEYS
    half = eb // 2

    def gate_and_project(sc_scr, blk, vt_blk):
        pk = 2 * SUBLANES
        for h in range(P_HEADS):
            for c in range(n_i1):
                i1 = blk * n_i1 + c
                for k, src in enumerate((lim_scr, e1_scr)):
                    row_scr[k, h * n_i1 + c] = jnp.broadcast_to(
                        src[h, pl.ds(i1, 1), :], (pk, tn)).astype(BF16)
        zero = jnp.zeros((N_KEYS, LANES), BF16)
        for hf in range(2):
            row_blocks = []
            for c in range(half // N_KEYS):
                ci = hf * (half // N_KEYS) + c
                col_blocks = []
                for tc in range(tn // LANES):
                    ln = slice(LANES * tc, LANES * (tc + 1))
                    gate = zero
                    for h in range(P_HEADS):
                        lim_t = jnp.concatenate(
                            [row_scr[0, h * n_i1 + ci, :, ln]] * (N_KEYS // pk), axis=0)
                        e1_t = jnp.concatenate(
                            [row_scr[1, h * n_i1 + ci, :, ln]] * (N_KEYS // pk), axis=0)
                        gate = gate + jnp.where(rk2_scr[h, :, ln] < lim_t,
                                                e1_t * e2_scr[h, :, ln], zero)
                    act = _gelu(sc_scr[N_KEYS * ci:N_KEYS * (ci + 1), ln]).astype(BF16)
                    col_blocks.append(gate * act)
                row_blocks.append(jnp.concatenate(col_blocks, axis=1))
            coef = jnp.concatenate(row_blocks, axis=0)
            acc_scr[...] += jnp.dot(vt_blk[:, half * hf:half * (hf + 1)], coef,
                                    preferred_element_type=F32)

    bufs = (sx_scr, sy_scr)
    for i in range(PEER_STEP_BLOCKS):
        following = ublk_ref[i + 1] if i + 1 < PEER_STEP_BLOCKS else unext_ref[0]
        bufs[(i + 1) % 2][...] = _dot_nt(following, hb)
        gate_and_project(bufs[i % 2], PEER_STEP_BLOCKS * j + i, vt_ref.at[i])

    @pl.when(j == pl.num_programs(1) - 1)
    def _():
        d = acc_scr.shape[0]
        gt = gt_ref[...]
        gt = jnp.broadcast_to(gt, (gt.shape[0], rep, d)).reshape(tn, d)
        x2 = x1_ref[...] + gt * acc_scr[...].T
        if final:
            x2 = _rms(x2) * gf_ref[...]
        out_ref[...] = x2


def _peer_call(h2, x1, gt, lp, g_final, *, tn, eb, final):
    b, t, d = x1.shape
    n = b * t
    assert n % tn == 0 and (t % tn == 0 or tn % t == 0)
    nbt = max(1, tn // t)
    tiles_per_b = max(1, t // tn)
    n_blk = lp["p_u"].shape[0]
    assert lp["p_u"].shape[1] == eb
    nsb = PEER_STEP_BLOCKS
    assert n_blk % nsb == 0 and nsb % 2 == 0
    kern = functools.partial(_peer_kernel, tn=tn, eb=eb, rep=tn // nbt, final=final)
    full = lambda a: pl.BlockSpec(a.shape, lambda ti, e: (0,) * a.ndim)
    tok = pl.BlockSpec((tn, d), lambda ti, e: (ti, 0))
    sel = pltpu.VMEM((P_HEADS, N_KEYS, tn), F32)
    sel_bf16 = pltpu.VMEM((P_HEADS, N_KEYS, tn), BF16)
    blk_scores = pltpu.VMEM((eb, tn), F32)
    out = pl.pallas_call(
        kern,
        grid=(n // tn, n_blk // nsb),
        in_specs=[tok, tok,
                  pl.BlockSpec((nbt, 1, d), lambda ti, e: (ti // tiles_per_b, 0, 0)),
                  full(lp["p_wq"]), full(lp["p_bq"]), full(lp["p_keys"]),
                  pl.BlockSpec((nsb, eb, d), lambda ti, e: (e, 0, 0)),
                  pl.BlockSpec((1, eb, d),
                               lambda ti, e: (jnp.minimum(nsb * (e + 1), n_blk - 1), 0, 0)),
                  pl.BlockSpec((nsb, d, eb), lambda ti, e: (e, 0, 0)),
                  full(g_final)],
        out_specs=tok,
        out_shape=jax.ShapeDtypeStruct((n, d), F32),
        scratch_shapes=[sel, sel, sel_bf16, sel_bf16,
                        pltpu.VMEM((2, P_HEADS * eb // N_KEYS, 2 * SUBLANES, tn), BF16),
                        blk_scores, blk_scores, pltpu.VMEM((d, tn), F32)],
        compiler_params=_params("arbitrary", "arbitrary"),
        name="peer",
    )(h2.reshape(n, d), x1.reshape(n, d), gt, lp["p_wq"], lp["p_bq"], lp["p_keys"],
      lp["p_u"], lp["p_u"], lp["p_vt"], g_final)
    return out.reshape(b, t, d)


def _layer(x, mod, lp, vfirst, past, q_offset, g_final, final, tiles):
    b, t, d = x.shape
    sh_t, sc_t, gt_t, sh_c, sc_c, gt_c = (m[:, None, :] for m in jnp.split(mod, 6, axis=-1))
    qi, k, v, ik, iw, rw, gates = _inproj_call(
        x, sc_t, sh_t, lp["g_norm1"], lp["w_in"], lp["idx_k_g"], lp["idx_k_b"], tiles["tm"])

    if past is None:
        k_all, v_all, ik_all = k, v, ik
        m0 = jnp.zeros((b, RW_HEADS, RW_HEAD_DIM, RW_HEAD_DIM), F32)
        shift0 = jnp.zeros((b, 1, RW_PAD), F32)
    else:
        k_past, v_past, ik_past, s0, rw_prev = past
        pl_ = k_past.shape[1]
        k_all = jnp.concatenate([k_past.reshape(b, pl_, KV_W), k], axis=1)
        v_all = jnp.concatenate([v_past.reshape(b, pl_, KV_W), v], axis=1)
        ik_all = jnp.concatenate([ik_past, ik], axis=1)
        m0 = jnp.swapaxes(s0, -1, -2)
        shift0 = jnp.pad(rw_prev, ((0, 0), (0, 0), (0, RW_PAD - RW_COLS)))
    tq = tiles["tq"]
    t_pad = -(-t // tq) * tq
    qpad = ((0, 0), (0, t_pad - t), (0, 0))
    o_a = _dsa_call(jnp.pad(qi, qpad), jnp.pad(iw, qpad), k_all, v_all, ik_all,
                    q_offset=q_offset, tq=tq, tk=tiles["tk"])[:, :t]

    r, lw, k2, v2, kkn, bb, g, bonus = _rwprep_call(rw, shift0, lp, vfirst, tiles["tm"])
    if vfirst is None:
        vfirst = v2
    o_scan, m_new = _scan_call(r, lw, k2, v2, kkn, bb, m0, tiles["c"])

    x1, h2 = _merge_call(x, o_a, o_scan, bonus, g, gates, gt_t, sc_c, sh_c, lp, tiles["tm"])
    x2 = _peer_call(h2, x1, gt_c, lp, g_final, tn=tiles["tn"], eb=tiles["eb"], final=final)
    state = (k.reshape(b, t, KV_HEADS, HEAD_DIM), v.reshape(b, t, KV_HEADS, HEAD_DIM), ik,
             jnp.swapaxes(m_new, -1, -2), rw[:, -1:, :RW_COLS])
    return x2, vfirst, state


def _tiles(t):
    return {"tm": min(t, 512), "tq": 2 * LANES if t % (2 * LANES) == 0 else LANES,
            "tk": 256, "c": min(t, CHUNK),
            "tn": 512 if t >= 512 else LANES, "eb": PEER_EB}


def kernel(x_prompt, x_sample, cache_k, cache_v, cache_kidx, state_wkv, state_shift, c_prompt, c_sample, w_ada, b_ada, g_norm1, w_in, idx_k_g, idx_k_b, rw_mu, rw_w0, rw_w_up, rw_a0, rw_a_up, rw_g_up, rw_k_k, rw_k_a, rw_r_k, rw_lnx_g, rw_lnx_b, rw_v0, rw_v_down, rw_v_up, w_oa, w_ob, w_out, g_norm2, peer_wq, peer_bq, peer_sub_keys, peer_u, peer_v, g_final):
    depth = w_in.shape[0]
    nbp = x_prompt.shape[0]
    past_len = cache_k.shape[2]
    bd = _head_block_diag()
    row = lambda a: a.reshape(1, -1)
    xp, xs = x_prompt, x_sample
    vf_p, vf_s = None, None
    new_p, new_s = [], []
    c_all = jnp.concatenate([c_prompt, c_sample], axis=0)
    gf = row(g_final)
    for l in range(depth):
        lp = {
            "g_norm1": row(g_norm1[l]), "w_in": _pack_w_in(w_in[l]),
            "idx_k_g": row(idx_k_g[l]), "idx_k_b": row(idx_k_b[l]),
            "mu": jnp.pad(row(rw_mu[l]), ((0, 0), (0, RW_PAD - RW_COLS))),
            "w0": row(rw_w0[l]), "a0": row(rw_a0[l]),
            "w_up": _pad_rows(rw_w_up[l], 0, LANES),
            "a_up": _pad_rows(rw_a_up[l], W_LORA, LANES),
            "g_up": _pad_rows(rw_g_up[l], 0, RW_PAD - 3 * RW_WIDTH - LANES),
            "k_k": row(rw_k_k[l]), "k_a": row(rw_k_a[l]), "r_k": row(rw_r_k[l]),
            "lnx_g": row(rw_lnx_g[l]), "lnx_b": row(rw_lnx_b[l]), "bd": bd,
            "w_oa": w_oa[l].astype(BF16), "w_ob": w_ob[l].astype(BF16),
            "w_out": w_out[l].astype(BF16), "g_norm2": row(g_norm2[l]),
            "p_wq": peer_wq[l].astype(BF16), "p_bq": row(peer_bq[l]),
            "p_keys": peer_sub_keys[l].reshape(2 * P_HEADS, N_KEYS, P_HALF).astype(BF16),
            "p_u": peer_u[l].reshape(-1, PEER_EB, peer_u.shape[-1]).astype(BF16),
            "p_vt": jnp.swapaxes(peer_v[l].reshape(-1, PEER_EB, peer_v.shape[-1]), 1, 2
                                 ).astype(BF16),
        }
        if l > 0:
            lp["v0"] = row(rw_v0[l - 1])
            lp["v_down"] = jnp.pad(rw_v_down[l - 1], ((0, 0), (0, LANES - V_LORA))).astype(BF16)
            lp["v_up"] = _pad_rows(rw_v_up[l - 1], 0, LANES)
        mod = _mod_call(c_all, w_ada[l], b_ada[l])
        final = l == depth - 1
        xp, vf_p, st_p = _layer(xp, mod[:nbp], lp, vf_p, None, 0, gf, final,
                                _tiles(xp.shape[1]))
        past = (cache_k[l], cache_v[l], cache_kidx[l], state_wkv[l], state_shift[l])
        xs, vf_s, st_s = _layer(xs, mod[nbp:], lp, vf_s, past, past_len, gf, final,
                                _tiles(xs.shape[1]))
        new_p.append(st_p)
        new_s.append(st_s)

    def stk(lst, i):
        return jnp.stack([e[i] for e in lst], axis=0)

    return (xp, xs,
            stk(new_p, 0), stk(new_p, 1), stk(new_p, 2), stk(new_p, 3), stk(new_p, 4),
            stk(new_s, 0), stk(new_s, 1), stk(new_s, 2), stk(new_s, 3), stk(new_s, 4))
```

```python
import functools

import numpy as np
import jax
import jax.numpy as jnp
from jax import lax
from jax.experimental import pallas as pl
from jax.experimental.pallas import tpu as pltpu

F32 = jnp.float32
BF16 = jnp.bfloat16
I32 = jnp.int32
HIGHEST = lax.Precision.HIGHEST

CHUNK = 64
N_HEADS = 8
HEAD_DIM = 64
KV_HEADS = 2
GROUP = N_HEADS // KV_HEADS
IDX_HEADS = 8
IDX_DIM = 64
DSA_TOPK = 256
ATTN_SCALE = HEAD_DIM ** -0.5
RW_HEADS = 8
RW_HEAD_DIM = 64
RW_WIDTH = RW_HEADS * RW_HEAD_DIM
W_LORA = 64
A_LORA = 64
V_LORA = 32
G_LORA = 160
RW_COLS = 3 * RW_WIDTH + W_LORA + A_LORA + G_LORA
N_KEYS = 128
P_HEADS = 8
P_HALF = 128
P_TOPK = 16
PEER_EB = 512
PEER_STEP_BLOCKS = 4
EPS = 1e-6
GN_EPS = 64e-5

LANES = 128
SUBLANES = 8
VMEM_LIMIT = 56 * 1024 * 1024

DSA_Q = N_HEADS * HEAD_DIM
IDX_Q = IDX_HEADS * IDX_DIM
KV_W = KV_HEADS * HEAD_DIM
QI_W = DSA_Q + IDX_Q
SMALL_W = 4 * LANES
RW_PAD = 15 * LANES
NEG = -1e30


def _params(*sem):
    return pltpu.CompilerParams(dimension_semantics=sem, vmem_limit_bytes=VMEM_LIMIT)


def _bdot(a, b):
    return jnp.dot(a.astype(BF16), b.astype(BF16), preferred_element_type=F32)


def _hdot(a, b):
    return jnp.dot(a, b, precision=HIGHEST, preferred_element_type=F32)


def _dot_nt(a, b, precision=None):
    return lax.dot_general(a, b, (((1,), (1,)), ((), ())), precision=precision,
                           preferred_element_type=F32)


def _dot_tn(a, b, precision=None):
    return lax.dot_general(a, b, (((0,), (0,)), ((), ())), precision=precision,
                           preferred_element_type=F32)


def _sigmoid(x):
    return 1.0 / (1.0 + jnp.exp(-x))


def _rms(x):
    return x * lax.rsqrt(jnp.mean(x * x, axis=-1, keepdims=True) + EPS)


def _mod_kernel(c_ref, w_ref, b_ref, o_ref):
    c = c_ref[...]
    o_ref[...] = _bdot(c * _sigmoid(c), w_ref[...]) + b_ref[...]


def _mod_call(c_all, w_ada, b_ada):
    nb, d = c_all.shape
    ncol = w_ada.shape[1] // d
    return pl.pallas_call(
        _mod_kernel,
        grid=(ncol,),
        in_specs=[pl.BlockSpec((nb, d), lambda j: (0, 0)),
                  pl.BlockSpec((d, d), lambda j: (0, j)),
                  pl.BlockSpec((1, d), lambda j: (0, j))],
        out_specs=pl.BlockSpec((nb, d), lambda j: (0, j)),
        out_shape=jax.ShapeDtypeStruct((nb, ncol * d), F32),
        compiler_params=_params("arbitrary"),
        name="adaln_mod",
    )(c_all, w_ada, b_ada.reshape(1, -1))


def _inproj_kernel(x_ref, sc_ref, sh_ref, g_ref, w_ref, ikg_ref, ikb_ref,
                   qi_ref, k_ref, v_ref, ik_ref, iw_ref, rw_ref, gates_ref):
    x = x_ref[0]
    h = (_rms(x) * g_ref[...]) * (1.0 + sc_ref[0]) + sh_ref[0]
    hb = h.astype(BF16)
    o0 = QI_W
    o1 = o0 + SMALL_W
    o2 = o1 + RW_PAD
    qi_ref[0] = jnp.dot(hb, w_ref[:, 0:o0], preferred_element_type=F32).astype(BF16)
    small = jnp.dot(hb, w_ref[:, o0:o1], preferred_element_type=F32)
    k_ref[0] = small[:, 0:LANES]
    v_ref[0] = small[:, LANES:2 * LANES]
    ik = small[:, 2 * LANES:2 * LANES + IDX_DIM]
    ikc = ik - jnp.mean(ik, axis=-1, keepdims=True)
    ikn = ikc * lax.rsqrt(jnp.mean(ikc * ikc, axis=-1, keepdims=True) + EPS)
    ik_ref[0] = ikn * ikg_ref[...] + ikb_ref[...]
    iw_ref[0] = small[:, 3 * LANES:3 * LANES + IDX_HEADS]
    rw_ref[0] = jnp.dot(hb, w_ref[:, o1:o2], preferred_element_type=F32)
    gates_ref[0] = jnp.dot(hb, w_ref[:, o2:], preferred_element_type=F32)


def _pack_w_in(w_in):
    d = w_in.shape[0]
    offs = np.cumsum([0, DSA_Q, KV_W, KV_W, IDX_Q, IDX_DIM, IDX_HEADS, RW_COLS, 2 * d])
    q, k, v, iq, ik, iw, rw, gates = (w_in[:, offs[i]:offs[i + 1]] for i in range(8))
    z = lambda n: jnp.zeros((d, n), w_in.dtype)
    packed = jnp.concatenate(
        [q, iq, k, v, ik, z(LANES - IDX_DIM), iw, z(LANES - IDX_HEADS),
         rw, z(RW_PAD - RW_COLS), gates], axis=1)
    return packed.astype(BF16)


def _inproj_call(x, sc, sh, g1, w_packed, ikg, ikb, tm):
    b, t, d = x.shape
    nw = w_packed.shape[1]
    tok = lambda n: pl.BlockSpec((1, tm, n), lambda bi, i: (bi, i, 0))
    row = lambda n: pl.BlockSpec((1, n), lambda bi, i: (0, 0))
    per_b = pl.BlockSpec((1, 1, d), lambda bi, i: (bi, 0, 0))
    widths = (QI_W, LANES, LANES, IDX_DIM, IDX_HEADS, RW_PAD, 2 * d)
    return pl.pallas_call(
        _inproj_kernel,
        grid=(b, t // tm),
        in_specs=[tok(d), per_b, per_b, row(d),
                  pl.BlockSpec((d, nw), lambda bi, i: (0, 0)),
                  row(IDX_DIM), row(IDX_DIM)],
        out_specs=[tok(n) for n in widths],
        out_shape=[jax.ShapeDtypeStruct((b, t, n), BF16 if i == 0 else F32)
                   for i, n in enumerate(widths)],
        compiler_params=_params("arbitrary", "arbitrary"),
        name="norm_inproj",
    )(x, sc, sh, g1, w_packed, ikg, ikb)


def _dsa_kernel(qi_ref, iwt_ref, k_ref, vt_ref, ik_ref, o_ref,
                key_scr, bias_scr, iq_scr, qg_scr, *, tq, tk, l_valid, q_offset, topk):
    qb = pl.program_id(1)
    int_min = jnp.int32(-2 ** 31)
    q0 = q_offset + qb * tq
    last_chunk = (q0 + tq - 1) // CHUNK
    n_adm = jnp.minimum((last_chunk + 1) * CHUNK, l_valid)
    n_kt = (n_adm + tk - 1) // tk

    x = qi_ref[0].astype(F32)
    for h in range(IDX_HEADS):
        iq_scr[h] = x[:, DSA_Q + IDX_DIM * h:DSA_Q + IDX_DIM * (h + 1)].astype(BF16)
    for g in range(KV_HEADS):
        for r in range(GROUP):
            h = GROUP * g + r
            qg_scr[g, r * tq:(r + 1) * tq, :] = (
                x[:, HEAD_DIM * h:HEAD_DIM * (h + 1)] * ATTN_SCALE).astype(BF16)
    iwt = iwt_ref[0]
    q_chunk = (q0 + lax.broadcasted_iota(I32, (1, tq), 1)) // CHUNK
    row_iota = lax.broadcasted_iota(I32, (tk, tq), 0)

    def tile_base(kt):
        return pl.multiple_of(kt * tk, tk)

    def score_body(kt, carry):
        base = tile_base(kt)
        ikt = ik_ref[0, pl.ds(base, tk), :].astype(BF16)
        acc = jnp.zeros((tk, tq), F32)
        for h in range(IDX_HEADS):
            acc = acc + iwt[h:h + 1, :] * jnp.maximum(_dot_nt(ikt, iq_scr[h]), 0.0)
        acc = jnp.where(acc == 0.0, 0.0, acc)
        bits = lax.bitcast_convert_type(acc, I32)
        key = jnp.where(bits < 0, bits ^ jnp.int32(0x7FFFFFFF), bits)
        kpos = base + row_iota
        adm = (kpos < l_valid) & ((kpos // CHUNK) <= q_chunk)
        key_scr[pl.ds(base, tk), :] = jnp.where(adm, key, int_min)
        return carry

    lax.fori_loop(0, n_kt, score_body, 0)

    acc_rows = 4 * SUBLANES

    def count(pred_fn):
        def body(kt, c):
            base = tile_base(kt)
            m = jnp.where(pred_fn(key_scr[pl.ds(base, tk), :], base + row_iota), 1, 0)
            return c + jnp.sum(m.reshape(tk // acc_rows, acc_rows, tq), axis=0)
        c = lax.fori_loop(0, n_kt // 2, lambda i, c: body(2 * i + 1, body(2 * i, c)),
                          jnp.zeros((acc_rows, tq), I32))
        c = lax.fori_loop(2 * (n_kt // 2), n_kt, body, c)
        return jnp.sum(c, axis=0, keepdims=True)

    def bit_body(i, tb):
        cand_b = tb | lax.shift_left(jnp.int32(1), 31 - i)
        cand = cand_b ^ int_min
        cnt = count(lambda kk, idx: kk >= cand)
        return jnp.where(cnt >= topk, cand_b, tb)

    tau = lax.fori_loop(0, 32, bit_body, jnp.zeros((1, tq), I32)) ^ int_min
    cnt_ge = count(lambda kk, idx: kk >= tau)
    cnt_gt = count(lambda kk, idx: kk > tau)
    need = topk - cnt_gt
    excess = (tau > int_min) & (cnt_ge - cnt_gt > need)
    any_excess = jnp.max(jnp.where(excess, 1, 0)) > 0

    idx_bits = 13
    def tie_limit():
        def jbody(i, j):
            cand_j = j | lax.shift_left(jnp.int32(1), idx_bits - 1 - i)
            f = count(lambda kk, idx: (kk == tau) & (idx < cand_j))
            return jnp.where(f <= need, cand_j, j)
        return lax.fori_loop(0, idx_bits, jbody, jnp.zeros((1, tq), I32))

    j_lim = lax.cond(any_excess, tie_limit,
                     lambda: jnp.full((1, tq), 2 ** idx_bits - 1, I32))

    def bias_body(kt, carry):
        base = tile_base(kt)
        kk = key_scr[pl.ds(base, tk), :]
        sel = (kk > tau) | ((kk == tau) & ((base + row_iota) < j_lim))
        sel = sel & (kk != int_min)
        bias_scr[pl.ds(base, tk), :] = jnp.where(sel, 0.0, NEG)
        return carry

    lax.fori_loop(0, n_kt, bias_body, 0)

    def attn_body(kt, carry):
        base = tile_base(kt)
        bias = bias_scr[pl.ds(base, tk), :]
        k_all = k_ref[0, pl.ds(base, tk), :]
        vt_all = vt_ref[0, kt]
        new = []
        s_groups = [_dot_nt(k_all[:, HEAD_DIM * g:HEAD_DIM * (g + 1)].astype(BF16), qg_scr[g])
                    for g in range(KV_HEADS)]
        for g in range(KV_HEADS):
            s_all = s_groups[g]
            ps, stats = [], []
            for r in range(GROUP):
                m, l, acc = carry[GROUP * g + r]
                s = s_all[:, r * tq:(r + 1) * tq] + bias
                m_new = jnp.maximum(m, jnp.max(s, axis=0, keepdims=True))
                alpha = jnp.exp(m - m_new)
                p = jnp.exp(s - m_new)
                ps.append(p.astype(BF16))
                stats.append((m_new, l * alpha + jnp.sum(p, axis=0, keepdims=True), alpha, acc))
            pv = jnp.dot(vt_all[HEAD_DIM * g:HEAD_DIM * (g + 1), :].astype(BF16),
                         jnp.concatenate(ps, axis=1), preferred_element_type=F32)
            for r, (m_new, l_new, alpha, acc) in enumerate(stats):
                new.append((m_new, l_new, acc * alpha + pv[:, r * tq:(r + 1) * tq]))
        return tuple(new)

    init = tuple((jnp.full((1, tq), NEG, F32), jnp.zeros((1, tq), F32),
                  jnp.zeros((HEAD_DIM, tq), F32)) for _ in range(N_HEADS))
    fin = lax.fori_loop(0, n_kt, attn_body, init)
    o_ref[0] = jnp.concatenate([acc / l for _, l, acc in fin], axis=0).T.astype(BF16)


def _dsa_call(qi, iw, k_all, v_all, ik_all, *, q_offset, tq, tk):
    b, t, _ = qi.shape
    l_valid = k_all.shape[1]
    topk = min(DSA_TOPK, l_valid // 4)
    assert topk <= tk and t % tq == 0
    l_pad = -(-l_valid // tk) * tk
    assert l_pad < 2 ** 13 - 1
    pad = ((0, 0), (0, l_pad - l_valid), (0, 0))
    k_p, v_p, ik_p = (jnp.pad(a, pad) for a in (k_all, v_all, ik_all))
    nkt = l_pad // tk
    vt = jnp.swapaxes(v_p.reshape(b, nkt, tk, KV_W), 2, 3)
    iwt = jnp.swapaxes(iw, 1, 2)
    kern = functools.partial(_dsa_kernel, tq=tq, tk=tk, l_valid=l_valid,
                             q_offset=q_offset, topk=topk)
    return pl.pallas_call(
        kern,
        grid=(b, t // tq),
        in_specs=[pl.BlockSpec((1, tq, QI_W), lambda bi, i: (bi, i, 0)),
                  pl.BlockSpec((1, IDX_HEADS, tq), lambda bi, i: (bi, 0, i)),
                  pl.BlockSpec((1, l_pad, KV_W), lambda bi, i: (bi, 0, 0)),
                  pl.BlockSpec((1, nkt, KV_W, tk), lambda bi, i: (bi, 0, 0, 0)),
                  pl.BlockSpec((1, l_pad, IDX_DIM), lambda bi, i: (bi, 0, 0))],
        out_specs=pl.BlockSpec((1, tq, DSA_Q), lambda bi, i: (bi, i, 0)),
        out_shape=jax.ShapeDtypeStruct((b, t, DSA_Q), BF16),
        scratch_shapes=[pltpu.VMEM((l_pad, tq), I32), pltpu.VMEM((l_pad, tq), F32),
                        pltpu.VMEM((IDX_HEADS, tq, IDX_DIM), BF16),
                        pltpu.VMEM((KV_HEADS, GROUP * tq, HEAD_DIM), BF16)],
        compiler_params=_params("arbitrary", "arbitrary"),
        name="dsa_attention",
    )(qi, iwt, k_p, vt, ik_p)


def _rwprep_kernel(*refs, has_vfirst):
    (rw_ref, prev8_ref, shift0_ref, mu_ref, w0_ref, a0_ref, wup_ref, aup_ref, gup_ref,
     kk_ref, ka_ref, rk_ref, bd_ref) = refs[:13]
    if has_vfirst:
        vfirst_ref, v0_ref, vdown_ref, vup_ref = refs[13:17]
        outs = refs[17:]
    else:
        outs = refs[13:]
    r_o, lw_o, k_o, v_o, kkn_o, b_o, g_o, bonus_o = outs
    i = pl.program_id(1)
    rw = rw_ref[0]
    prev = jnp.where(i == 0, shift0_ref[0], prev8_ref[0][SUBLANES - 1:SUBLANES, :])
    row = lax.broadcasted_iota(I32, rw.shape, 0)
    shifted = jnp.where(row == 0, prev, pltpu.roll(rw, 1, 0))
    mix = rw + mu_ref[...] * (shifted - rw)
    w3 = RW_WIDTH
    r = mix[:, 0:w3]
    kr = mix[:, w3:2 * w3]
    vr = mix[:, 2 * w3:3 * w3]
    wa = mix[:, 3 * w3:3 * w3 + LANES]
    gd = mix[:, 3 * w3 + LANES:]
    z = w0_ref[...] + _bdot(jnp.tanh(wa), wup_ref[...])
    nz = -z
    softplus = jnp.maximum(nz, 0.0) + jnp.log(1.0 + jnp.exp(-jnp.abs(nz)))
    lw = -jnp.exp(-softplus - 0.5)
    a = _sigmoid(a0_ref[...] + _bdot(wa, aup_ref[...]))
    g = _bdot(_sigmoid(gd), gup_ref[...])
    if has_vfirst:
        lora = _bdot(_bdot(vr, vdown_ref[...]), vup_ref[...])
        vr = vr + (vfirst_ref[0] - vr) * _sigmoid(v0_ref[...] + lora)
    bd = bd_ref[...]
    kkr = kr * kk_ref[...]
    kkn = kkr / jnp.maximum(jnp.sqrt(_hdot(kkr * kkr, bd)), 1e-12)
    k2 = kr * (1.0 + (a - 1.0) * ka_ref[...])
    r_o[0] = r
    lw_o[0] = lw
    k_o[0] = k2
    v_o[0] = vr
    kkn_o[0] = kkn
    b_o[0] = kkn * a
    g_o[0] = g
    bonus_o[0] = _hdot(r * k2 * rk_ref[...], bd) * vr


def _head_block_diag():
    h = np.arange(RW_WIDTH) // RW_HEAD_DIM
    return jnp.asarray((h[:, None] == h[None, :]).astype(np.float32))


def _pad_rows(w, lo, total):
    return jnp.pad(w, ((lo, total - lo - w.shape[0]), (0, 0))).astype(BF16)


def _rwprep_call(rw, shift0, lp, vfirst, tm):
    b, t, _ = rw.shape
    has_vfirst = vfirst is not None
    tok = lambda n: pl.BlockSpec((1, tm, n), lambda bi, i: (bi, i, 0))
    full = lambda a: pl.BlockSpec(a.shape, lambda bi, i: (0,) * a.ndim)
    consts = [lp["mu"], lp["w0"], lp["a0"], lp["w_up"], lp["a_up"], lp["g_up"],
              lp["k_k"], lp["k_a"], lp["r_k"], lp["bd"]]
    args = [rw, rw, shift0] + consts
    in_specs = [tok(RW_PAD),
                pl.BlockSpec((1, SUBLANES, RW_PAD),
                             lambda bi, i: (bi, jnp.maximum(i * (tm // SUBLANES) - 1, 0), 0)),
                pl.BlockSpec((1, 1, RW_PAD), lambda bi, i: (bi, 0, 0))]
    in_specs += [full(a) for a in consts]
    if has_vfirst:
        extra = [lp["v0"], lp["v_down"], lp["v_up"]]
        args += [vfirst] + extra
        in_specs += [tok(RW_WIDTH)] + [full(a) for a in extra]
    return pl.pallas_call(
        functools.partial(_rwprep_kernel, has_vfirst=has_vfirst),
        grid=(b, t // tm),
        in_specs=in_specs,
        out_specs=[tok(RW_WIDTH)] * 8,
        out_shape=[jax.ShapeDtypeStruct((b, t, RW_WIDTH), F32)] * 8,
        compiler_params=_params("arbitrary", "arbitrary"),
        name="rwkv_prep",
    )(*args)


def _scan_intra_kernel(r_ref, lw_ref, k_ref, v_ref, kk_ref, b_ref,
                       r2_ref, oi_ref, a_ref, d_ref, *, c, nc):
    row = lax.broadcasted_iota(I32, (c, c), 0)
    col = lax.broadcasted_iota(I32, (c, c), 1)
    incl = row >= col
    strict = row > col
    eye_c = jnp.where(row == col, 1.0, 0.0)
    ones_incl = jnp.where(incl, 1.0, 0.0)
    n = RW_HEAD_DIM
    rn = lax.broadcasted_iota(I32, (n, n), 0)
    cn = lax.broadcasted_iota(I32, (n, n), 1)
    nlev = int(np.log2(c))

    prep = []
    for ci in range(nc):
        rows = slice(ci * c, (ci + 1) * c)
        lw, k, b = lw_ref[0, rows, :], k_ref[0, rows, :], b_ref[0, rows, :]
        cum = _hdot(ones_incl, lw)
        total = cum[c - 1:c, :]
        g_inv = jnp.exp(-cum)
        g_rem = jnp.exp(total - cum)
        prep.append(dict(
            alpha=kk_ref[0, rows, :] * jnp.exp(cum - lw), beta=b * g_inv, kappa=k * g_inv,
            rho=r_ref[0, rows, :] * jnp.exp(cum), khat=k * g_rem, bhat=b * g_rem,
            g_tot=jnp.exp(total), v=v_ref[0, rows, :]))
    units = [(ci, h) for ci in range(nc) for h in range(RW_HEADS)]

    def head(ci, h, name):
        return prep[ci][name][:, n * h:n * (h + 1)]

    grams = [_dot_nt(
        jnp.concatenate([head(ci, h, "alpha"), head(ci, h, "rho")], axis=0).astype(BF16),
        jnp.concatenate([head(ci, h, "beta"), head(ci, h, "kappa")], axis=0).astype(BF16))
        for ci, h in units]
    l_ak = [jnp.where(strict, g[:c, c:], 0.0) for g in grams]
    l_rb = [jnp.where(incl, g[c:, :c], 0.0) for g in grams]
    l_rk = [jnp.where(incl, g[c:, c:], 0.0) for g in grams]
    ps = [-jnp.where(strict, g[:c, :c], 0.0) for g in grams]
    tinvs = [eye_c + p for p in ps]
    for _ in range(nlev - 1):
        ps = [_bdot(p, p) for p in ps]
        tinvs = [t + _bdot(t, p) for t, p in zip(tinvs, ps)]
    lvs = [_bdot(jnp.concatenate([ak, rk], axis=0), head(ci, h, "v"))
           for ak, rk, (ci, h) in zip(l_ak, l_rk, units)]
    wys = [_hdot(t, jnp.concatenate([head(ci, h, "alpha"), lv[:c]], axis=1))
           for t, lv, (ci, h) in zip(tinvs, lvs, units)]
    rbs = [_bdot(rb, wy) for rb, wy in zip(l_rb, wys)]
    bws = [_dot_tn(head(ci, h, "bhat"), wy, HIGHEST) for wy, (ci, h) in zip(wys, units)]
    kvs = [_dot_tn(head(ci, h, "khat"), head(ci, h, "v"), HIGHEST) for ci, h in units]
    for ci in range(nc):
        rows = slice(ci * c, (ci + 1) * c)
        mine = [u for u, (cj, _) in enumerate(units) if cj == ci]
        r2_ref[0, rows, :] = jnp.concatenate(
            [head(ci, h, "rho") - rbs[u][:, :n] for h, u in enumerate(mine)], axis=1)
        oi_ref[0, rows, :] = jnp.concatenate(
            [lvs[u][c:] - rbs[u][:, n:] for u in mine], axis=1)
        for h, u in enumerate(mine):
            dg = jnp.where(rn == cn,
                           jnp.broadcast_to(prep[ci]["g_tot"][:, n * h:n * (h + 1)], (n, n)), 0.0)
            a_ref[0, ci, h] = dg - bws[u][:, :n]
            d_ref[0, ci, h] = kvs[u] - bws[u][:, n:]


def _scan_inter_kernel(r2_ref, oi_ref, a_ref, d_ref, m0_ref, o_ref, mout_ref, m_scr, *, nb):
    ci = pl.program_id(1)

    @pl.when(ci == 0)
    def _():
        m_scr[...] = m0_ref[...]

    n = RW_HEAD_DIM
    for bi in range(nb):
        r2 = r2_ref[bi]
        outs = []
        for h in range(RW_HEADS):
            m0 = m_scr[bi, h]
            outs.append(_hdot(r2[:, n * h:n * (h + 1)], m0))
            m_scr[bi, h] = _hdot(a_ref[bi, 0, h], m0) + d_ref[bi, 0, h]
        o_ref[bi] = jnp.concatenate(outs, axis=1) + oi_ref[bi]

    @pl.when(ci == pl.num_programs(1) - 1)
    def _():
        mout_ref[...] = m_scr[...]


def _scan_call(r, lw, k, v, kk, bb, m0, c):
    b, t, w = r.shape
    nch = t // c
    nc = next(n for n in (4, 2, 1) if nch % n == 0)
    nb = next(n for n in (4, 2, 1) if b % n == 0)
    hd = RW_HEAD_DIM
    tok = pl.BlockSpec((1, nc * c, w), lambda bi, i: (bi, i, 0))
    mats = pl.BlockSpec((1, nc, RW_HEADS, hd, hd), lambda bi, i: (bi, i, 0, 0, 0))
    r2, oi, a_mat, d_mat = pl.pallas_call(
        functools.partial(_scan_intra_kernel, c=c, nc=nc),
        grid=(b, nch // nc),
        in_specs=[tok] * 6,
        out_specs=[tok, tok, mats, mats],
        out_shape=[jax.ShapeDtypeStruct((b, t, w), F32)] * 2
        + [jax.ShapeDtypeStruct((b, nch, RW_HEADS, hd, hd), F32)] * 2,
        compiler_params=_params("arbitrary", "arbitrary"),
        name="rwkv_chunk_terms",
    )(r, lw, k, v, kk, bb)
    tok_b = pl.BlockSpec((nb, c, w), lambda bi, i: (bi, i, 0))
    mat_b = pl.BlockSpec((nb, 1, RW_HEADS, hd, hd), lambda bi, i: (bi, i, 0, 0, 0))
    st = pl.BlockSpec((nb, RW_HEADS, hd, hd), lambda bi, i: (bi, 0, 0, 0))
    return pl.pallas_call(
        functools.partial(_scan_inter_kernel, nb=nb),
        grid=(b // nb, nch),
        in_specs=[tok_b, tok_b, mat_b, mat_b, st],
        out_specs=[tok_b, st],
        out_shape=[jax.ShapeDtypeStruct((b, t, w), F32),
                   jax.ShapeDtypeStruct(m0.shape, F32)],
        scratch_shapes=[pltpu.VMEM((nb, RW_HEADS, hd, hd), F32)],
        compiler_params=_params("arbitrary", "arbitrary"),
        name="rwkv_scan",
    )(r2, oi, a_mat, d_mat, m0)


def _merge_kernel(x_ref, oa_ref, os_ref, bonus_ref, g_ref, gates_ref, gt_ref, sc_ref, sh_ref,
                  woa_ref, wob_ref, wout_ref, lng_ref, lnb_ref, gn2_ref, bd_ref,
                  x1_ref, h2_ref):
    bd = bd_ref[...]
    inv_n = 1.0 / RW_HEAD_DIM
    o = os_ref[0]
    oc = o - _hdot(o, bd) * inv_n
    var = _hdot(oc * oc, bd) * inv_n
    y = oc * lax.rsqrt(var + GN_EPS) * lng_ref[...] + lnb_ref[...]
    ob = (y + bonus_ref[0]) * g_ref[0]
    gates = gates_ref[0]
    d = x_ref.shape[-1]
    merged = (_sigmoid(gates[:, :d]) * _bdot(oa_ref[0], woa_ref[...])
              + _sigmoid(gates[:, d:]) * _bdot(ob, wob_ref[...]))
    x1 = x_ref[0] + gt_ref[0] * _bdot(merged, wout_ref[...])
    x1_ref[0] = x1
    h2 = (_rms(x1) * gn2_ref[...]) * (1.0 + sc_ref[0]) + sh_ref[0]
    h2_ref[0] = h2.astype(BF16)


def _merge_call(x, oa, o_scan, bonus, g, gates, gt, sc, sh, lp, tm):
    b, t, d = x.shape
    tok = lambda n: pl.BlockSpec((1, tm, n), lambda bi, i: (bi, i, 0))
    per_b = pl.BlockSpec((1, 1, d), lambda bi, i: (bi, 0, 0))
    full = lambda a: pl.BlockSpec(a.shape, lambda bi, i: (0,) * a.ndim)
    consts = [lp["w_oa"], lp["w_ob"], lp["w_out"], lp["lnx_g"], lp["lnx_b"], lp["g_norm2"],
              lp["bd"]]
    return pl.pallas_call(
        _merge_kernel,
        grid=(b, t // tm),
        in_specs=[tok(d), tok(DSA_Q), tok(RW_WIDTH), tok(RW_WIDTH), tok(RW_WIDTH), tok(2 * d),
                  per_b, per_b, per_b] + [full(a) for a in consts],
        out_specs=[tok(d), tok(d)],
        out_shape=[jax.ShapeDtypeStruct((b, t, d), F32), jax.ShapeDtypeStruct((b, t, d), BF16)],
        compiler_params=_params("arbitrary", "arbitrary"),
        name="merge_out",
    )(x, oa, o_scan, bonus, g, gates, gt, sc, sh, *consts)


def _kth_largest_rows(x, kth):
    work = x
    cnt = jnp.zeros((1, x.shape[1]), F32)
    tau = jnp.full((1, x.shape[1]), -jnp.inf, F32)
    for _ in range(kth):
        mx = jnp.max(work, axis=0, keepdims=True)
        eq = work == mx
        tau = jnp.where(cnt < kth, mx, tau)
        cnt = cnt + jnp.sum(jnp.where(eq, 1.0, 0.0), axis=0, keepdims=True)
        work = jnp.where(eq, -jnp.inf, work)
    return tau


def _top_rows(x, kth):
    work = x
    rank = jnp.full(x.shape, float(kth), F32)
    tops = []
    for r in range(kth):
        mx = jnp.max(work, axis=0, keepdims=True)
        tops.append(mx)
        hit = work == mx
        rank = jnp.where(hit, float(r), rank)
        work = jnp.where(hit, -jnp.inf, work)
    return tops, rank


def _gelu(x):
    return 0.5 * x * (1.0 + lax.erf(x * (2.0 ** -0.5)))


def _peer_kernel(h2_ref, x1_ref, gt_ref, wq_ref, bq_ref, keys_ref, ublk_ref, unext_ref,
                 vt_ref, gf_ref, out_ref, lim_scr, e1_scr, rk2_scr, e2_scr, row_scr,
                 sx_scr, sy_scr, acc_scr, *, tn, eb, rep, final):
    j = pl.program_id(1)
    hb = h2_ref[...]

    @pl.when(j == 0)
    def _():
        sx_scr[...] = _dot_nt(ublk_ref[0], hb)
        q = (jnp.dot(hb, wq_ref[...], preferred_element_type=F32) + bq_ref[...]).astype(BF16)
        for h in range(P_HEADS):
            halves = []
            for c in range(2):
                hc = 2 * h + c
                s = _dot_nt(keys_ref[hc], q[:, P_HALF * hc:P_HALF * (hc + 1)])
                halves.append((s,) + _top_rows(s, P_TOPK))
            (s1, m1, _), (s2, m2, rank2) = halves
            m1s = jnp.concatenate(m1, axis=0)
            m2s = jnp.concatenate(m2, axis=0)
            hk = P_TOPK // 2
            m2lo = m2s[:hk]
            skip2 = lax.broadcasted_iota(I32, m2lo.shape, 0) < 2
            cand = jnp.concatenate(
                [m1s + m2[0], m1s[:hk] + m2[1], m2s[hk:] + m1[0]]
                + [jnp.where(skip2, -jnp.inf, m2lo + m1[r1]) for r1 in range(5)], axis=0)
            tau = _kth_largest_rows(cand, P_TOPK)
            z = jnp.sum(jnp.where(cand >= tau, jnp.exp(cand - (m1[0] + m2[0])), 0.0),
                        axis=0, keepdims=True)
            s1_top = jnp.where(s1 >= m1[P_TOPK - 1], s1, -jnp.inf)
            lim = jnp.zeros_like(s1)
            for r2 in range(hk):
                lim = lim + jnp.where(s1_top + m2[r2] >= tau, 1.0, 0.0)
            lim_hi = jnp.sum(jnp.where(m2s[hk:] + m1[0] >= tau, 1.0, 0.0), axis=0, keepdims=True)
            lim_scr[h] = lim + jnp.where(s1 == m1[0], lim_hi, 0.0)
            rk2_scr[h] = rank2.astype(BF16)
            e1_scr[h] = jnp.exp(s1 - m1[0])
            e2_scr[h] = (jnp.exp(s2 - m2[0]) / z).astype(BF16)
        acc_scr[...] = jnp.zeros_like(acc_scr)

    n_i1 = eb // N_KEYS
    half = eb // 2

    def gate_and_project(sc_scr, blk, vt_blk):
        pk = 2 * SUBLANES
        for h in range(P_HEADS):
            for c in range(n_i1):
                i1 = blk * n_i1 + c
                for k, src in enumerate((lim_scr, e1_scr)):
                    row_scr[k, h * n_i1 + c] = jnp.broadcast_to(
                        src[h, pl.ds(i1, 1), :], (pk, tn)).astype(BF16)
        zero = jnp.zeros((N_KEYS, LANES), BF16)
        for hf in range(2):
            row_blocks = []
            for c in range(half // N_KEYS):
                ci = hf * (half // N_KEYS) + c
                col_blocks = []
                for tc in range(tn // LANES):
                    ln = slice(LANES * tc, LANES * (tc + 1))
                    gate = zero
                    for h in range(P_HEADS):
                        lim_t = jnp.concatenate(
                            [row_scr[0, h * n_i1 + ci, :, ln]] * (N_KEYS // pk), axis=0)
                        e1_t = jnp.concatenate(
                            [row_scr[1, h * n_i1 + ci, :, ln]] * (N_KEYS // pk), axis=0)
                        gate = gate + jnp.where(rk2_scr[h, :, ln] < lim_t,
                                                e1_t * e2_scr[h, :, ln], zero)
                    act = _gelu(sc_scr[N_KEYS * ci:N_KEYS * (ci + 1), ln]).astype(BF16)
                    col_blocks.append(gate * act)
                row_blocks.append(jnp.concatenate(col_blocks, axis=1))
            coef = jnp.concatenate(row_blocks, axis=0)
            acc_scr[...] += jnp.dot(vt_blk[:, half * hf:half * (hf + 1)], coef,
                                    preferred_element_type=F32)

    bufs = (sx_scr, sy_scr)
    for i in range(PEER_STEP_BLOCKS):
        following = ublk_ref[i + 1] if i + 1 < PEER_STEP_BLOCKS else unext_ref[0]
        bufs[(i + 1) % 2][...] = _dot_nt(following, hb)
        gate_and_project(bufs[i % 2], PEER_STEP_BLOCKS * j + i, vt_ref.at[i])

    @pl.when(j == pl.num_programs(1) - 1)
    def _():
        d = acc_scr.shape[0]
        gt = gt_ref[...]
        gt = jnp.broadcast_to(gt, (gt.shape[0], rep, d)).reshape(tn, d)
        x2 = x1_ref[...] + gt * acc_scr[...].T
        if final:
            x2 = _rms(x2) * gf_ref[...]
        out_ref[...] = x2


def _peer_call(h2, x1, gt, lp, g_final, *, tn, eb, final):
    b, t, d = x1.shape
    n = b * t
    assert n % tn == 0 and (t % tn == 0 or tn % t == 0)
    nbt = max(1, tn // t)
    tiles_per_b = max(1, t // tn)
    n_blk = lp["p_u"].shape[0]
    assert lp["p_u"].shape[1] == eb
    nsb = PEER_STEP_BLOCKS
    assert n_blk % nsb == 0 and nsb % 2 == 0
    kern = functools.partial(_peer_kernel, tn=tn, eb=eb, rep=tn // nbt, final=final)
    full = lambda a: pl.BlockSpec(a.shape, lambda ti, e: (0,) * a.ndim)
    tok = pl.BlockSpec((tn, d), lambda ti, e: (ti, 0))
    sel = pltpu.VMEM((P_HEADS, N_KEYS, tn), F32)
    sel_bf16 = pltpu.VMEM((P_HEADS, N_KEYS, tn), BF16)
    blk_scores = pltpu.VMEM((eb, tn), F32)
    out = pl.pallas_call(
        kern,
        grid=(n // tn, n_blk // nsb),
        in_specs=[tok, tok,
                  pl.BlockSpec((nbt, 1, d), lambda ti, e: (ti // tiles_per_b, 0, 0)),
                  full(lp["p_wq"]), full(lp["p_bq"]), full(lp["p_keys"]),
                  pl.BlockSpec((nsb, eb, d), lambda ti, e: (e, 0, 0)),
                  pl.BlockSpec((1, eb, d),
                               lambda ti, e: (jnp.minimum(nsb * (e + 1), n_blk - 1), 0, 0)),
                  pl.BlockSpec((nsb, d, eb), lambda ti, e: (e, 0, 0)),
                  full(g_final)],
        out_specs=tok,
        out_shape=jax.ShapeDtypeStruct((n, d), F32),
        scratch_shapes=[sel, sel, sel_bf16, sel_bf16,
                        pltpu.VMEM((2, P_HEADS * eb // N_KEYS, 2 * SUBLANES, tn), BF16),
                        blk_scores, blk_scores, pltpu.VMEM((d, tn), F32)],
        compiler_params=_params("arbitrary", "arbitrary"),
        name="peer",
    )(h2.reshape(n, d), x1.reshape(n, d), gt, lp["p_wq"], lp["p_bq"], lp["p_keys"],
      lp["p_u"], lp["p_u"], lp["p_vt"], g_final)
    return out.reshape(b, t, d)


def _layer(x, mod, lp, vfirst, past, q_offset, g_final, final, tiles):
    b, t, d = x.shape
    sh_t, sc_t, gt_t, sh_c, sc_c, gt_c = (m[:, None, :] for m in jnp.split(mod, 6, axis=-1))
    qi, k, v, ik, iw, rw, gates = _inproj_call(
        x, sc_t, sh_t, lp["g_norm1"], lp["w_in"], lp["idx_k_g"], lp["idx_k_b"], tiles["tm"])

    if past is None:
        k_all, v_all, ik_all = k, v, ik
        m0 = jnp.zeros((b, RW_HEADS, RW_HEAD_DIM, RW_HEAD_DIM), F32)
        shift0 = jnp.zeros((b, 1, RW_PAD), F32)
    else:
        k_past, v_past, ik_past, s0, rw_prev = past
        pl_ = k_past.shape[1]
        k_all = jnp.concatenate([k_past.reshape(b, pl_, KV_W), k], axis=1)
        v_all = jnp.concatenate([v_past.reshape(b, pl_, KV_W), v], axis=1)
        ik_all = jnp.concatenate([ik_past, ik], axis=1)
        m0 = jnp.swapaxes(s0, -1, -2)
        shift0 = jnp.pad(rw_prev, ((0, 0), (0, 0), (0, RW_PAD - RW_COLS)))
    tq = tiles["tq"]
    t_pad = -(-t // tq) * tq
    qpad = ((0, 0), (0, t_pad - t), (0, 0))
    o_a = _dsa_call(jnp.pad(qi, qpad), jnp.pad(iw, qpad), k_all, v_all, ik_all,
                    q_offset=q_offset, tq=tq, tk=tiles["tk"])[:, :t]

    r, lw, k2, v2, kkn, bb, g, bonus = _rwprep_call(rw, shift0, lp, vfirst, tiles["tm"])
    if vfirst is None:
        vfirst = v2
    o_scan, m_new = _scan_call(r, lw, k2, v2, kkn, bb, m0, tiles["c"])

    x1, h2 = _merge_call(x, o_a, o_scan, bonus, g, gates, gt_t, sc_c, sh_c, lp, tiles["tm"])
    x2 = _peer_call(h2, x1, gt_c, lp, g_final, tn=tiles["tn"], eb=tiles["eb"], final=final)
    state = (k.reshape(b, t, KV_HEADS, HEAD_DIM), v.reshape(b, t, KV_HEADS, HEAD_DIM), ik,
             jnp.swapaxes(m_new, -1, -2), rw[:, -1:, :RW_COLS])
    return x2, vfirst, state


def _tiles(t):
    return {"tm": min(t, 512), "tq": 2 * LANES if t % (2 * LANES) == 0 else LANES,
            "tk": 256, "c": min(t, CHUNK),
            "tn": 512 if t >= 512 else LANES, "eb": PEER_EB}


def kernel(x_prompt, x_sample, cache_k, cache_v, cache_kidx, state_wkv, state_shift, c_prompt, c_sample, w_ada, b_ada, g_norm1, w_in, idx_k_g, idx_k_b, rw_mu, rw_w0, rw_w_up, rw_a0, rw_a_up, rw_g_up, rw_k_k, rw_k_a, rw_r_k, rw_lnx_g, rw_lnx_b, rw_v0, rw_v_down, rw_v_up, w_oa, w_ob, w_out, g_norm2, peer_wq, peer_bq, peer_sub_keys, peer_u, peer_v, g_final):
    depth = w_in.shape[0]
    nbp = x_prompt.shape[0]
    past_len = cache_k.shape[2]
    bd = _head_block_diag()
    row = lambda a: a.reshape(1, -1)
    xp, xs = x_prompt, x_sample
    vf_p, vf_s = None, None
    new_p, new_s = [], []
    c_all = jnp.concatenate([c_prompt, c_sample], axis=0)
    gf = row(g_final)
    for l in range(depth):
        lp = {
            "g_norm1": row(g_norm1[l]), "w_in": _pack_w_in(w_in[l]),
            "idx_k_g": row(idx_k_g[l]), "idx_k_b": row(idx_k_b[l]),
            "mu": jnp.pad(row(rw_mu[l]), ((0, 0), (0, RW_PAD - RW_COLS))),
            "w0": row(rw_w0[l]), "a0": row(rw_a0[l]),
            "w_up": _pad_rows(rw_w_up[l], 0, LANES),
            "a_up": _pad_rows(rw_a_up[l], W_LORA, LANES),
            "g_up": _pad_rows(rw_g_up[l], 0, RW_PAD - 3 * RW_WIDTH - LANES),
            "k_k": row(rw_k_k[l]), "k_a": row(rw_k_a[l]), "r_k": row(rw_r_k[l]),
            "lnx_g": row(rw_lnx_g[l]), "lnx_b": row(rw_lnx_b[l]), "bd": bd,
            "w_oa": w_oa[l].astype(BF16), "w_ob": w_ob[l].astype(BF16),
            "w_out": w_out[l].astype(BF16), "g_norm2": row(g_norm2[l]),
            "p_wq": peer_wq[l].astype(BF16), "p_bq": row(peer_bq[l]),
            "p_keys": peer_sub_keys[l].reshape(2 * P_HEADS, N_KEYS, P_HALF).astype(BF16),
            "p_u": peer_u[l].reshape(-1, PEER_EB, peer_u.shape[-1]).astype(BF16),
            "p_vt": jnp.swapaxes(peer_v[l].reshape(-1, PEER_EB, peer_v.shape[-1]), 1, 2
                                 ).astype(BF16),
        }
        if l > 0:
            lp["v0"] = row(rw_v0[l - 1])
            lp["v_down"] = jnp.pad(rw_v_down[l - 1], ((0, 0), (0, LANES - V_LORA))).astype(BF16)
            lp["v_up"] = _pad_rows(rw_v_up[l - 1], 0, LANES)
        mod = _mod_call(c_all, w_ada[l], b_ada[l])
        final = l == depth - 1
        xp, vf_p, st_p = _layer(xp, mod[:nbp], lp, vf_p, None, 0, gf, final,
                                _tiles(xp.shape[1]))
        past = (cache_k[l], cache_v[l], cache_kidx[l], state_wkv[l], state_shift[l])
        xs, vf_s, st_s = _layer(xs, mod[nbp:], lp, vf_s, past, past_len, gf, final,
                                _tiles(xs.shape[1]))
        new_p.append(st_p)
        new_s.append(st_s)

    def stk(lst, i):
        return jnp.stack([e[i] for e in lst], axis=0)

    return (xp, xs,
            stk(new_p, 0), stk(new_p, 1), stk(new_p, 2), stk(new_p, 3), stk(new_p, 4),
            stk(new_s, 0), stk(new_s, 1), stk(new_s, 2), stk(new_s, 3), stk(new_s, 4))
```

```python
import functools

import numpy as np
import jax
import jax.numpy as jnp
from jax import lax
from jax.experimental import pallas as pl
from jax.experimental.pallas import tpu as pltpu

F32 = jnp.float32
BF16 = jnp.bfloat16
I32 = jnp.int32
HIGHEST = lax.Precision.HIGHEST

CHUNK = 64
N_HEADS = 8
HEAD_DIM = 64
KV_HEADS = 2
GROUP = N_HEADS // KV_HEADS
IDX_HEADS = 8
IDX_DIM = 64
DSA_TOPK = 256
ATTN_SCALE = HEAD_DIM ** -0.5
RW_HEADS = 8
RW_HEAD_DIM = 64
RW_WIDTH = RW_HEADS * RW_HEAD_DIM
W_LORA = 64
A_LORA = 64
V_LORA = 32
G_LORA = 160
RW_COLS = 3 * RW_WIDTH + W_LORA + A_LORA + G_LORA
N_KEYS = 128
P_HEADS = 8
P_HALF = 128
P_TOPK = 16
PEER_EB = 512
PEER_STEP_BLOCKS = 4
EPS = 1e-6
GN_EPS = 64e-5

LANES = 128
SUBLANES = 8
VMEM_LIMIT = 56 * 1024 * 1024

DSA_Q = N_HEADS * HEAD_DIM
IDX_Q = IDX_HEADS * IDX_DIM
KV_W = KV_HEADS * HEAD_DIM
QI_W = DSA_Q + IDX_Q
SMALL_W = 4 * LANES
RW_PAD = 15 * LANES
NEG = -1e30


def _params(*sem):
    return pltpu.CompilerParams(dimension_semantics=sem, vmem_limit_bytes=VMEM_LIMIT)


def _bdot(a, b):
    return jnp.dot(a.astype(BF16), b.astype(BF16), preferred_element_type=F32)


def _hdot(a, b):
    return jnp.dot(a, b, precision=HIGHEST, preferred_element_type=F32)


def _dot_nt(a, b, precision=None):
    return lax.dot_general(a, b, (((1,), (1,)), ((), ())), precision=precision,
                           preferred_element_type=F32)


def _dot_tn(a, b, precision=None):
    return lax.dot_general(a, b, (((0,), (0,)), ((), ())), precision=precision,
                           preferred_element_type=F32)


def _sigmoid(x):
    return 1.0 / (1.0 + jnp.exp(-x))


def _rms(x):
    return x * lax.rsqrt(jnp.mean(x * x, axis=-1, keepdims=True) + EPS)


def _mod_kernel(c_ref, w_ref, b_ref, o_ref):
    c = c_ref[...]
    o_ref[...] = _bdot(c * _sigmoid(c), w_ref[...]) + b_ref[...]


def _mod_call(c_all, w_ada, b_ada):
    nb, d = c_all.shape
    ncol = w_ada.shape[1] // d
    return pl.pallas_call(
        _mod_kernel,
        grid=(ncol,),
        in_specs=[pl.BlockSpec((nb, d), lambda j: (0, 0)),
                  pl.BlockSpec((d, d), lambda j: (0, j)),
                  pl.BlockSpec((1, d), lambda j: (0, j))],
        out_specs=pl.BlockSpec((nb, d), lambda j: (0, j)),
        out_shape=jax.ShapeDtypeStruct((nb, ncol * d), F32),
        compiler_params=_params("arbitrary"),
        name="adaln_mod",
    )(c_all, w_ada, b_ada.reshape(1, -1))


def _inproj_kernel(x_ref, sc_ref, sh_ref, g_ref, w_ref, ikg_ref, ikb_ref,
                   qi_ref, k_ref, v_ref, ik_ref, iw_ref, rw_ref, gates_ref):
    x = x_ref[0]
    h = (_rms(x) * g_ref[...]) * (1.0 + sc_ref[0]) + sh_ref[0]
    hb = h.astype(BF16)
    o0 = QI_W
    o1 = o0 + SMALL_W
    o2 = o1 + RW_PAD
    qi_ref[0] = jnp.dot(hb, w_ref[:, 0:o0], preferred_element_type=F32).astype(BF16)
    small = jnp.dot(hb, w_ref[:, o0:o1], preferred_element_type=F32)
    k_ref[0] = small[:, 0:LANES]
    v_ref[0] = small[:, LANES:2 * LANES]
    ik = small[:, 2 * LANES:2 * LANES + IDX_DIM]
    ikc = ik - jnp.mean(ik, axis=-1, keepdims=True)
    ikn = ikc * lax.rsqrt(jnp.mean(ikc * ikc, axis=-1, keepdims=True) + EPS)
    ik_ref[0] = ikn * ikg_ref[...] + ikb_ref[...]
    iw_ref[0] = small[:, 3 * LANES:3 * LANES + IDX_HEADS]
    rw_ref[0] = jnp.dot(hb, w_ref[:, o1:o2], preferred_element_type=F32)
    gates_ref[0] = jnp.dot(hb, w_ref[:, o2:], preferred_element_type=F32)


def _pack_w_in(w_in):
    d = w_in.shape[0]
    offs = np.cumsum([0, DSA_Q, KV_W, KV_W, IDX_Q, IDX_DIM, IDX_HEADS, RW_COLS, 2 * d])
    q, k, v, iq, ik, iw, rw, gates = (w_in[:, offs[i]:offs[i + 1]] for i in range(8))
    z = lambda n: jnp.zeros((d, n), w_in.dtype)
    packed = jnp.concatenate(
        [q, iq, k, v, ik, z(LANES - IDX_DIM), iw, z(LANES - IDX_HEADS),
         rw, z(RW_PAD - RW_COLS), gates], axis=1)
    return packed.astype(BF16)


def _inproj_call(x, sc, sh, g1, w_packed, ikg, ikb, tm):
    b, t, d = x.shape
    nw = w_packed.shape[1]
    tok = lambda n: pl.BlockSpec((1, tm, n), lambda bi, i: (bi, i, 0))
    row = lambda n: pl.BlockSpec((1, n), lambda bi, i: (0, 0))
    per_b = pl.BlockSpec((1, 1, d), lambda bi, i: (bi, 0, 0))
    widths = (QI_W, LANES, LANES, IDX_DIM, IDX_HEADS, RW_PAD, 2 * d)
    return pl.pallas_call(
        _inproj_kernel,
        grid=(b, t // tm),
        in_specs=[tok(d), per_b, per_b, row(d),
                  pl.BlockSpec((d, nw), lambda bi, i: (0, 0)),
                  row(IDX_DIM), row(IDX_DIM)],
        out_specs=[tok(n) for n in widths],
        out_shape=[jax.ShapeDtypeStruct((b, t, n), BF16 if i == 0 else F32)
                   for i, n in enumerate(widths)],
        compiler_params=_params("arbitrary", "arbitrary"),
        name="norm_inproj",
    )(x, sc, sh, g1, w_packed, ikg, ikb)


def _dsa_kernel(qi_ref, iwt_ref, k_ref, vt_ref, ik_ref, o_ref,
                key_scr, bias_scr, iq_scr, qg_scr, *, tq, tk, l_valid, q_offset, topk):
    qb = pl.program_id(1)
    int_min = jnp.int32(-2 ** 31)
    q0 = q_offset + qb * tq
    last_chunk = (q0 + tq - 1) // CHUNK
    n_adm = jnp.minimum((last_chunk + 1) * CHUNK, l_valid)
    n_kt = (n_adm + tk - 1) // tk

    x = qi_ref[0].astype(F32)
    for h in range(IDX_HEADS):
        iq_scr[h] = x[:, DSA_Q + IDX_DIM * h:DSA_Q + IDX_DIM * (h + 1)].astype(BF16)
    for g in range(KV_HEADS):
        for r in range(GROUP):
            h = GROUP * g + r
            qg_scr[g, r * tq:(r + 1) * tq, :] = (
                x[:, HEAD_DIM * h:HEAD_DIM * (h + 1)] * ATTN_SCALE).astype(BF16)
    iwt = iwt_ref[0]
    q_chunk = (q0 + lax.broadcasted_iota(I32, (1, tq), 1)) // CHUNK
    row_iota = lax.broadcasted_iota(I32, (tk, tq), 0)

    def tile_base(kt):
        return pl.multiple_of(kt * tk, tk)

    def score_body(kt, carry):
        base = tile_base(kt)
        ikt = ik_ref[0, pl.ds(base, tk), :].astype(BF16)
        acc = jnp.zeros((tk, tq), F32)
        for h in range(IDX_HEADS):
            acc = acc + iwt[h:h + 1, :] * jnp.maximum(_dot_nt(ikt, iq_scr[h]), 0.0)
        acc = jnp.where(acc == 0.0, 0.0, acc)
        bits = lax.bitcast_convert_type(acc, I32)
        key = jnp.where(bits < 0, bits ^ jnp.int32(0x7FFFFFFF), bits)
        kpos = base + row_iota
        adm = (kpos < l_valid) & ((kpos // CHUNK) <= q_chunk)
        key_scr[pl.ds(base, tk), :] = jnp.where(adm, key, int_min)
        return carry

    lax.fori_loop(0, n_kt, score_body, 0)

    acc_rows = 4 * SUBLANES

    def count(pred_fn):
        def body(kt, c):
            base = tile_base(kt)
            m = jnp.where(pred_fn(key_scr[pl.ds(base, tk), :], base + row_iota), 1, 0)
            return c + jnp.sum(m.reshape(tk // acc_rows, acc_rows, tq), axis=0)
        c = lax.fori_loop(0, n_kt // 2, lambda i, c: body(2 * i + 1, body(2 * i, c)),
                          jnp.zeros((acc_rows, tq), I32))
        c = lax.fori_loop(2 * (n_kt // 2), n_kt, body, c)
        return jnp.sum(c, axis=0, keepdims=True)

    def bit_body(i, tb):
        cand_b = tb | lax.shift_left(jnp.int32(1), 31 - i)
        cand = cand_b ^ int_min
        cnt = count(lambda kk, idx: kk >= cand)
        return jnp.where(cnt >= topk, cand_b, tb)

    tau = lax.fori_loop(0, 32, bit_body, jnp.zeros((1, tq), I32)) ^ int_min
    cnt_ge = count(lambda kk, idx: kk >= tau)
    cnt_gt = count(lambda kk, idx: kk > tau)
    need = topk - cnt_gt
    excess = (tau > int_min) & (cnt_ge - cnt_gt > need)
    any_excess = jnp.max(jnp.where(excess, 1, 0)) > 0

    idx_bits = 13
    def tie_limit():
        def jbody(i, j):
            cand_j = j | lax.shift_left(jnp.int32(1), idx_bits - 1 - i)
            f = count(lambda kk, idx: (kk == tau) & (idx < cand_j))
            return jnp.where(f <= need, cand_j, j)
        return lax.fori_loop(0, idx_bits, jbody, jnp.zeros((1, tq), I32))

    j_lim = lax.cond(any_excess, tie_limit,
                     lambda: jnp.full((1, tq), 2 ** idx_bits - 1, I32))

    def bias_body(kt, carry):
        base = tile_base(kt)
        kk = key_scr[pl.ds(base, tk), :]
        sel = (kk > tau) | ((kk == tau) & ((base + row_iota) < j_lim))
        sel = sel & (kk != int_min)
        bias_scr[pl.ds(base, tk), :] = jnp.where(sel, 0.0, NEG)
        return carry

    lax.fori_loop(0, n_kt, bias_body, 0)

    def attn_body(kt, carry):
        base = tile_base(kt)
        bias = bias_scr[pl.ds(base, tk), :]
        k_all = k_ref[0, pl.ds(base, tk), :]
        vt_all = vt_ref[0, kt]
        new = []
        s_groups = [_dot_nt(k_all[:, HEAD_DIM * g:HEAD_DIM * (g + 1)].astype(BF16), qg_scr[g])
                    for g in range(KV_HEADS)]
        for g in range(KV_HEADS):
            s_all = s_groups[g]
            ps, stats = [], []
            for r in range(GROUP):
                m, l, acc = carry[GROUP * g + r]
                s = s_all[:, r * tq:(r + 1) * tq] + bias
                m_new = jnp.maximum(m, jnp.max(s, axis=0, keepdims=True))
                alpha = jnp.exp(m - m_new)
                p = jnp.exp(s - m_new)
                ps.append(p.astype(BF16))
                stats.append((m_new, l * alpha + jnp.sum(p, axis=0, keepdims=True), alpha, acc))
            pv = jnp.dot(vt_all[HEAD_DIM * g:HEAD_DIM * (g + 1), :].astype(BF16),
                         jnp.concatenate(ps, axis=1), preferred_element_type=F32)
            for r, (m_new, l_new, alpha, acc) in enumerate(stats):
                new.append((m_new, l_new, acc * alpha + pv[:, r * tq:(r + 1) * tq]))
        return tuple(new)

    init = tuple((jnp.full((1, tq), NEG, F32), jnp.zeros((1, tq), F32),
                  jnp.zeros((HEAD_DIM, tq), F32)) for _ in range(N_HEADS))
    fin = lax.fori_loop(0, n_kt, attn_body, init)
    o_ref[0] = jnp.concatenate([acc / l for _, l, acc in fin], axis=0).T.astype(BF16)


def _dsa_call(qi, iw, k_all, v_all, ik_all, *, q_offset, tq, tk):
    b, t, _ = qi.shape
    l_valid = k_all.shape[1]
    topk = min(DSA_TOPK, l_valid // 4)
    assert topk <= tk and t % tq == 0
    l_pad = -(-l_valid // tk) * tk
    assert l_pad < 2 ** 13 - 1
    pad = ((0, 0), (0, l_pad - l_valid), (0, 0))
    k_p, v_p, ik_p = (jnp.pad(a, pad) for a in (k_all, v_all, ik_all))
    nkt = l_pad // tk
    vt = jnp.swapaxes(v_p.reshape(b, nkt, tk, KV_W), 2, 3)
    iwt = jnp.swapaxes(iw, 1, 2)
    kern = functools.partial(_dsa_kernel, tq=tq, tk=tk, l_valid=l_valid,
                             q_offset=q_offset, topk=topk)
    return pl.pallas_call(
        kern,
        grid=(b, t // tq),
        in_specs=[pl.BlockSpec((1, tq, QI_W), lambda bi, i: (bi, i, 0)),
                  pl.BlockSpec((1, IDX_HEADS, tq), lambda bi, i: (bi, 0, i)),
                  pl.BlockSpec((1, l_pad, KV_W), lambda bi, i: (bi, 0, 0)),
                  pl.BlockSpec((1, nkt, KV_W, tk), lambda bi, i: (bi, 0, 0, 0)),
                  pl.BlockSpec((1, l_pad, IDX_DIM), lambda bi, i: (bi, 0, 0))],
        out_specs=pl.BlockSpec((1, tq, DSA_Q), lambda bi, i: (bi, i, 0)),
        out_shape=jax.ShapeDtypeStruct((b, t, DSA_Q), BF16),
        scratch_shapes=[pltpu.VMEM((l_pad, tq), I32), pltpu.VMEM((l_pad, tq), F32),
                        pltpu.VMEM((IDX_HEADS, tq, IDX_DIM), BF16),
                        pltpu.VMEM((KV_HEADS, GROUP * tq, HEAD_DIM), BF16)],
        compiler_params=_params("arbitrary", "arbitrary"),
        name="dsa_attention",
    )(qi, iwt, k_p, vt, ik_p)


def _rwprep_kernel(*refs, has_vfirst):
    (rw_ref, prev8_ref, shift0_ref, mu_ref, w0_ref, a0_ref, wup_ref, aup_ref, gup_ref,
     kk_ref, ka_ref, rk_ref, bd_ref) = refs[:13]
    if has_vfirst:
        vfirst_ref, v0_ref, vdown_ref, vup_ref = refs[13:17]
        outs = refs[17:]
    else:
        outs = refs[13:]
    r_o, lw_o, k_o, v_o, kkn_o, b_o, g_o, bonus_o = outs
    i = pl.program_id(1)
    rw = rw_ref[0]
    prev = jnp.where(i == 0, shift0_ref[0], prev8_ref[0][SUBLANES - 1:SUBLANES, :])
    row = lax.broadcasted_iota(I32, rw.shape, 0)
    shifted = jnp.where(row == 0, prev, pltpu.roll(rw, 1, 0))
    mix = rw + mu_ref[...] * (shifted - rw)
    w3 = RW_WIDTH
    r = mix[:, 0:w3]
    kr = mix[:, w3:2 * w3]
    vr = mix[:, 2 * w3:3 * w3]
    wa = mix[:, 3 * w3:3 * w3 + LANES]
    gd = mix[:, 3 * w3 + LANES:]
    z = w0_ref[...] + _bdot(jnp.tanh(wa), wup_ref[...])
    nz = -z
    softplus = jnp.maximum(nz, 0.0) + jnp.log(1.0 + jnp.exp(-jnp.abs(nz)))
    lw = -jnp.exp(-softplus - 0.5)
    a = _sigmoid(a0_ref[...] + _bdot(wa, aup_ref[...]))
    g = _bdot(_sigmoid(gd), gup_ref[...])
    if has_vfirst:
        lora = _bdot(_bdot(vr, vdown_ref[...]), vup_ref[...])
        vr = vr + (vfirst_ref[0] - vr) * _sigmoid(v0_ref[...] + lora)
    bd = bd_ref[...]
    kkr = kr * kk_ref[...]
    kkn = kkr / jnp.maximum(jnp.sqrt(_hdot(kkr * kkr, bd)), 1e-12)
    k2 = kr * (1.0 + (a - 1.0) * ka_ref[...])
    r_o[0] = r
    lw_o[0] = lw
    k_o[0] = k2
    v_o[0] = vr
    kkn_o[0] = kkn
    b_o[0] = kkn * a
    g_o[0] = g
    bonus_o[0] = _hdot(r * k2 * rk_ref[...], bd) * vr


def _head_block_diag():
    h = np.arange(RW_WIDTH) // RW_HEAD_DIM
    return jnp.asarray((h[:, None] == h[None, :]).astype(np.float32))


def _pad_rows(w, lo, total):
    return jnp.pad(w, ((lo, total - lo - w.shape[0]), (0, 0))).astype(BF16)


def _rwprep_call(rw, shift0, lp, vfirst, tm):
    b, t, _ = rw.shape
    has_vfirst = vfirst is not None
    tok = lambda n: pl.BlockSpec((1, tm, n), lambda bi, i: (bi, i, 0))
    full = lambda a: pl.BlockSpec(a.shape, lambda bi, i: (0,) * a.ndim)
    consts = [lp["mu"], lp["w0"], lp["a0"], lp["w_up"], lp["a_up"], lp["g_up"],
              lp["k_k"], lp["k_a"], lp["r_k"], lp["bd"]]
    args = [rw, rw, shift0] + consts
    in_specs = [tok(RW_PAD),
                pl.BlockSpec((1, SUBLANES, RW_PAD),
                             lambda bi, i: (bi, jnp.maximum(i * (tm // SUBLANES) - 1, 0), 0)),
                pl.BlockSpec((1, 1, RW_PAD), lambda bi, i: (bi, 0, 0))]
    in_specs += [full(a) for a in consts]
    if has_vfirst:
        extra = [lp["v0"], lp["v_down"], lp["v_up"]]
        args += [vfirst] + extra
        in_specs += [tok(RW_WIDTH)] + [full(a) for a in extra]
    return pl.pallas_call(
        functools.partial(_rwprep_kernel, has_vfirst=has_vfirst),
        grid=(b, t // tm),
        in_specs=in_specs,
        out_specs=[tok(RW_WIDTH)] * 8,
        out_shape=[jax.ShapeDtypeStruct((b, t, RW_WIDTH), F32)] * 8,
        compiler_params=_params("arbitrary", "arbitrary"),
        name="rwkv_prep",
    )(*args)


def _scan_intra_kernel(r_ref, lw_ref, k_ref, v_ref, kk_ref, b_ref,
                       r2_ref, oi_ref, a_ref, d_ref, *, c, nc):
    row = lax.broadcasted_iota(I32, (c, c), 0)
    col = lax.broadcasted_iota(I32, (c, c), 1)
    incl = row >= col
    strict = row > col
    eye_c = jnp.where(row == col, 1.0, 0.0)
    ones_incl = jnp.where(incl, 1.0, 0.0)
    n = RW_HEAD_DIM
    rn = lax.broadcasted_iota(I32, (n, n), 0)
    cn = lax.broadcasted_iota(I32, (n, n), 1)
    nlev = int(np.log2(c))

    prep = []
    for ci in range(nc):
        rows = slice(ci * c, (ci + 1) * c)
        lw, k, b = lw_ref[0, rows, :], k_ref[0, rows, :], b_ref[0, rows, :]
        cum = _hdot(ones_incl, lw)
        total = cum[c - 1:c, :]
        g_inv = jnp.exp(-cum)
        g_rem = jnp.exp(total - cum)
        prep.append(dict(
            alpha=kk_ref[0, rows, :] * jnp.exp(cum - lw), beta=b * g_inv, kappa=k * g_inv,
            rho=r_ref[0, rows, :] * jnp.exp(cum), khat=k * g_rem, bhat=b * g_rem,
            g_tot=jnp.exp(total), v=v_ref[0, rows, :]))
    units = [(ci, h) for ci in range(nc) for h in range(RW_HEADS)]

    def head(ci, h, name):
        return prep[ci][name][:, n * h:n * (h + 1)]

    grams = [_dot_nt(
        jnp.concatenate([head(ci, h, "alpha"), head(ci, h, "rho")], axis=0).astype(BF16),
        jnp.concatenate([head(ci, h, "beta"), head(ci, h, "kappa")], axis=0).astype(BF16))
        for ci, h in units]
    l_ak = [jnp.where(strict, g[:c, c:], 0.0) for g in grams]
    l_rb = [jnp.where(incl, g[c:, :c], 0.0) for g in grams]
    l_rk = [jnp.where(incl, g[c:, c:], 0.0) for g in grams]
    ps = [-jnp.where(strict, g[:c, :c], 0.0) for g in grams]
    tinvs = [eye_c + p for p in ps]
    for _ in range(nlev - 1):
        ps = [_bdot(p, p) for p in ps]
        tinvs = [t + _bdot(t, p) for t, p in zip(tinvs, ps)]
    lvs = [_bdot(jnp.concatenate([ak, rk], axis=0), head(ci, h, "v"))
           for ak, rk, (ci, h) in zip(l_ak, l_rk, units)]
    wys = [_hdot(t, jnp.concatenate([head(ci, h, "alpha"), lv[:c]], axis=1))
           for t, lv, (ci, h) in zip(tinvs, lvs, units)]
    rbs = [_bdot(rb, wy) for rb, wy in zip(l_rb, wys)]
    bws = [_dot_tn(head(ci, h, "bhat"), wy, HIGHEST) for wy, (ci, h) in zip(wys, units)]
    kvs = [_dot_tn(head(ci, h, "khat"), head(ci, h, "v"), HIGHEST) for ci, h in units]
    for ci in range(nc):
        rows = slice(ci * c, (ci + 1) * c)
        mine = [u for u, (cj, _) in enumerate(units) if cj == ci]
        r2_ref[0, rows, :] = jnp.concatenate(
            [head(ci, h, "rho") - rbs[u][:, :n] for h, u in enumerate(mine)], axis=1)
        oi_ref[0, rows, :] = jnp.concatenate(
            [lvs[u][c:] - rbs[u][:, n:] for u in mine], axis=1)
        for h, u in enumerate(mine):
            dg = jnp.where(rn == cn,
                           jnp.broadcast_to(prep[ci]["g_tot"][:, n * h:n * (h + 1)], (n, n)), 0.0)
            a_ref[0, ci, h] = dg - bws[u][:, :n]
            d_ref[0, ci, h] = kvs[u] - bws[u][:, n:]


def _scan_inter_kernel(r2_ref, oi_ref, a_ref, d_ref, m0_ref, o_ref, mout_ref, m_scr, *, nb):
    ci = pl.program_id(1)

    @pl.when(ci == 0)
    def _():
        m_scr[...] = m0_ref[...]

    n = RW_HEAD_DIM
    for bi in range(nb):
        r2 = r2_ref[bi]
        outs = []
        for h in range(RW_HEADS):
            m0 = m_scr[bi, h]
            outs.append(_hdot(r2[:, n * h:n * (h + 1)], m0))
            m_scr[bi, h] = _hdot(a_ref[bi, 0, h], m0) + d_ref[bi, 0, h]
        o_ref[bi] = jnp.concatenate(outs, axis=1) + oi_ref[bi]

    @pl.when(ci == pl.num_programs(1) - 1)
    def _():
        mout_ref[...] = m_scr[...]


def _scan_call(r, lw, k, v, kk, bb, m0, c):
    b, t, w = r.shape
    nch = t // c
    nc = next(n for n in (4, 2, 1) if nch % n == 0)
    nb = next(n for n in (4, 2, 1) if b % n == 0)
    hd = RW_HEAD_DIM
    tok = pl.BlockSpec((1, nc * c, w), lambda bi, i: (bi, i, 0))
    mats = pl.BlockSpec((1, nc, RW_HEADS, hd, hd), lambda bi, i: (bi, i, 0, 0, 0))
    r2, oi, a_mat, d_mat = pl.pallas_call(
        functools.partial(_scan_intra_kernel, c=c, nc=nc),
        grid=(b, nch // nc),
        in_specs=[tok] * 6,
        out_specs=[tok, tok, mats, mats],
        out_shape=[jax.ShapeDtypeStruct((b, t, w), F32)] * 2
        + [jax.ShapeDtypeStruct((b, nch, RW_HEADS, hd, hd), F32)] * 2,
        compiler_params=_params("arbitrary", "arbitrary"),
        name="rwkv_chunk_terms",
    )(r, lw, k, v, kk, bb)
    tok_b = pl.BlockSpec((nb, c, w), lambda bi, i: (bi, i, 0))
    mat_b = pl.BlockSpec((nb, 1, RW_HEADS, hd, hd), lambda bi, i: (bi, i, 0, 0, 0))
    st = pl.BlockSpec((nb, RW_HEADS, hd, hd), lambda bi, i: (bi, 0, 0, 0))
    return pl.pallas_call(
        functools.partial(_scan_inter_kernel, nb=nb),
        grid=(b // nb, nch),
        in_specs=[tok_b, tok_b, mat_b, mat_b, st],
        out_specs=[tok_b, st],
        out_shape=[jax.ShapeDtypeStruct((b, t, w), F32),
                   jax.ShapeDtypeStruct(m0.shape, F32)],
        scratch_shapes=[pltpu.VMEM((nb, RW_HEADS, hd, hd), F32)],
        compiler_params=_params("arbitrary", "arbitrary"),
        name="rwkv_scan",
    )(r2, oi, a_mat, d_mat, m0)


def _merge_kernel(x_ref, oa_ref, os_ref, bonus_ref, g_ref, gates_ref, gt_ref, sc_ref, sh_ref,
                  woa_ref, wob_ref, wout_ref, lng_ref, lnb_ref, gn2_ref, bd_ref,
                  x1_ref, h2_ref):
    bd = bd_ref[...]
    inv_n = 1.0 / RW_HEAD_DIM
    o = os_ref[0]
    oc = o - _hdot(o, bd) * inv_n
    var = _hdot(oc * oc, bd) * inv_n
    y = oc * lax.rsqrt(var + GN_EPS) * lng_ref[...] + lnb_ref[...]
    ob = (y + bonus_ref[0]) * g_ref[0]
    gates = gates_ref[0]
    d = x_ref.shape[-1]
    merged = (_sigmoid(gates[:, :d]) * _bdot(oa_ref[0], woa_ref[...])
              + _sigmoid(gates[:, d:]) * _bdot(ob, wob_ref[...]))
    x1 = x_ref[0] + gt_ref[0] * _bdot(merged, wout_ref[...])
    x1_ref[0] = x1
    h2 = (_rms(x1) * gn2_ref[...]) * (1.0 + sc_ref[0]) + sh_ref[0]
    h2_ref[0] = h2.astype(BF16)


def _merge_call(x, oa, o_scan, bonus, g, gates, gt, sc, sh, lp, tm):
    b, t, d = x.shape
    tok = lambda n: pl.BlockSpec((1, tm, n), lambda bi, i: (bi, i, 0))
    per_b = pl.BlockSpec((1, 1, d), lambda bi, i: (bi, 0, 0))
    full = lambda a: pl.BlockSpec(a.shape, lambda bi, i: (0,) * a.ndim)
    consts = [lp["w_oa"], lp["w_ob"], lp["w_out"], lp["lnx_g"], lp["lnx_b"], lp["g_norm2"],
              lp["bd"]]
    return pl.pallas_call(
        _merge_kernel,
        grid=(b, t // tm),
        in_specs=[tok(d), tok(DSA_Q), tok(RW_WIDTH), tok(RW_WIDTH), tok(RW_WIDTH), tok(2 * d),
                  per_b, per_b, per_b] + [full(a) for a in consts],
        out_specs=[tok(d), tok(d)],
        out_shape=[jax.ShapeDtypeStruct((b, t, d), F32), jax.ShapeDtypeStruct((b, t, d), BF16)],
        compiler_params=_params("arbitrary", "arbitrary"),
        name="merge_out",
    )(x, oa, o_scan, bonus, g, gates, gt, sc, sh, *consts)


def _top_rows(x, kth):
    work = x
    rank = jnp.full(x.shape, float(kth), F32)
    tops = []
    for r in range(kth):
        mx = jnp.max(work, axis=0, keepdims=True)
        tops.append(mx)
        hit = work == mx
        rank = jnp.where(hit, float(r), rank)
        work = jnp.where(hit, -jnp.inf, work)
    return tops, rank


def _gelu(x):
    return 0.5 * x * (1.0 + lax.erf(x * (2.0 ** -0.5)))


def _peer_kernel(h2_ref, x1_ref, gt_ref, wq_ref, bq_ref, keys_ref, ublk_ref, unext_ref,
                 vt_ref, gf_ref, out_ref, lim_scr, e1_scr, rk2_scr, e2_scr, row_scr,
                 sx_scr, sy_scr, acc_scr, *, tn, eb, rep, final):
    j = pl.program_id(1)
    hb = h2_ref[...]

    @pl.when(j == 0)
    def _():
        sx_scr[...] = _dot_nt(ublk_ref[0], hb)
        q = (jnp.dot(hb, wq_ref[...], preferred_element_type=F32) + bq_ref[...]).astype(BF16)
        for h in range(P_HEADS):
            halves = []
            for c in range(2):
                hc = 2 * h + c
                s = _dot_nt(keys_ref[hc], q[:, P_HALF * hc:P_HALF * (hc + 1)])
                halves.append((s,) + _top_rows(s, P_TOPK))
            (s1, m1, _), (s2, m2, rank2) = halves
            m1s = jnp.concatenate(m1, axis=0)
            m2s = jnp.concatenate(m2, axis=0)
            hk = P_TOPK // 2
            m2lo = m2s[:hk]
            skip2 = lax.broadcasted_iota(I32, m2lo.shape, 0) < 2
            cand = jnp.concatenate(
                [m1s + m2[0], m1s[:hk] + m2[1], m2s[hk:] + m1[0]]
                + [jnp.where(skip2, -jnp.inf, m2lo + m1[r1]) for r1 in range(5)], axis=0)
            tau = _top_rows(cand, P_TOPK)[0][P_TOPK - 1]
            z = jnp.sum(jnp.where(cand >= tau, jnp.exp(cand - (m1[0] + m2[0])), 0.0),
                        axis=0, keepdims=True)
            s1_top = jnp.where(s1 >= m1[P_TOPK - 1], s1, -jnp.inf)
            lim = jnp.zeros_like(s1)
            for r2 in range(hk):
                lim = lim + jnp.where(s1_top + m2[r2] >= tau, 1.0, 0.0)
            lim_hi = jnp.sum(jnp.where(m2s[hk:] + m1[0] >= tau, 1.0, 0.0), axis=0, keepdims=True)
            lim_scr[h] = lim + jnp.where(s1 == m1[0], lim_hi, 0.0)
            rk2_scr[h] = rank2.astype(BF16)
            e1_scr[h] = jnp.exp(s1 - m1[0])
            e2_scr[h] = (jnp.exp(s2 - m2[0]) / z).astype(BF16)
        acc_scr[...] = jnp.zeros_like(acc_scr)

    n_i1 = eb // N_KEYS
    half = eb // 2

    def gate_and_project(sc_scr, blk, vt_blk):
        pk = 2 * SUBLANES
        for h in range(P_HEADS):
            for c in range(n_i1):
                i1 = blk * n_i1 + c
                for k, src in enumerate((lim_scr, e1_scr)):
                    row_scr[k, h * n_i1 + c] = jnp.broadcast_to(
                        src[h, pl.ds(i1, 1), :], (pk, tn)).astype(BF16)
        zero = jnp.zeros((N_KEYS, LANES), BF16)
        for hf in range(2):
            row_blocks = []
            for c in range(half // N_KEYS):
                ci = hf * (half // N_KEYS) + c
                col_blocks = []
                for tc in range(tn // LANES):
                    ln = slice(LANES * tc, LANES * (tc + 1))
                    gate = zero
                    for h in range(P_HEADS):
                        lim_t = jnp.concatenate(
                            [row_scr[0, h * n_i1 + ci, :, ln]] * (N_KEYS // pk), axis=0)
                        e1_t = jnp.concatenate(
                            [row_scr[1, h * n_i1 + ci, :, ln]] * (N_KEYS // pk), axis=0)
                        gate = gate + jnp.where(rk2_scr[h, :, ln] < lim_t,
                                                e1_t * e2_scr[h, :, ln], zero)
                    act = _gelu(sc_scr[N_KEYS * ci:N_KEYS * (ci + 1), ln]).astype(BF16)
                    col_blocks.append(gate * act)
                row_blocks.append(jnp.concatenate(col_blocks, axis=1))
            coef = jnp.concatenate(row_blocks, axis=0)
            acc_scr[...] += jnp.dot(vt_blk[:, half * hf:half * (hf + 1)], coef,
                                    preferred_element_type=F32)

    bufs = (sx_scr, sy_scr)
    for i in range(PEER_STEP_BLOCKS):
        following = ublk_ref[i + 1] if i + 1 < PEER_STEP_BLOCKS else unext_ref[0]
        bufs[(i + 1) % 2][...] = _dot_nt(following, hb)
        gate_and_project(bufs[i % 2], PEER_STEP_BLOCKS * j + i, vt_ref.at[i])

    @pl.when(j == pl.num_programs(1) - 1)
    def _():
        d = acc_scr.shape[0]
        gt = gt_ref[...]
        gt = jnp.broadcast_to(gt, (gt.shape[0], rep, d)).reshape(tn, d)
        x2 = x1_ref[...] + gt * acc_scr[...].T
        if final:
            x2 = _rms(x2) * gf_ref[...]
        out_ref[...] = x2


def _peer_call(h2, x1, gt, lp, g_final, *, tn, eb, final):
    b, t, d = x1.shape
    n = b * t
    assert n % tn == 0 and (t % tn == 0 or tn % t == 0)
    nbt = max(1, tn // t)
    tiles_per_b = max(1, t // tn)
    n_blk = lp["p_u"].shape[0]
    assert lp["p_u"].shape[1] == eb
    nsb = PEER_STEP_BLOCKS
    assert n_blk % nsb == 0 and nsb % 2 == 0
    kern = functools.partial(_peer_kernel, tn=tn, eb=eb, rep=tn // nbt, final=final)
    full = lambda a: pl.BlockSpec(a.shape, lambda ti, e: (0,) * a.ndim)
    tok = pl.BlockSpec((tn, d), lambda ti, e: (ti, 0))
    sel = pltpu.VMEM((P_HEADS, N_KEYS, tn), F32)
    sel_bf16 = pltpu.VMEM((P_HEADS, N_KEYS, tn), BF16)
    blk_scores = pltpu.VMEM((eb, tn), F32)
    out = pl.pallas_call(
        kern,
        grid=(n // tn, n_blk // nsb),
        in_specs=[tok, tok,
                  pl.BlockSpec((nbt, 1, d), lambda ti, e: (ti // tiles_per_b, 0, 0)),
                  full(lp["p_wq"]), full(lp["p_bq"]), full(lp["p_keys"]),
                  pl.BlockSpec((nsb, eb, d), lambda ti, e: (e, 0, 0)),
                  pl.BlockSpec((1, eb, d),
                               lambda ti, e: (jnp.minimum(nsb * (e + 1), n_blk - 1), 0, 0)),
                  pl.BlockSpec((nsb, d, eb), lambda ti, e: (e, 0, 0)),
                  full(g_final)],
        out_specs=tok,
        out_shape=jax.ShapeDtypeStruct((n, d), F32),
        scratch_shapes=[sel, sel, sel_bf16, sel_bf16,
                        pltpu.VMEM((2, P_HEADS * eb // N_KEYS, 2 * SUBLANES, tn), BF16),
                        blk_scores, blk_scores, pltpu.VMEM((d, tn), F32)],
        compiler_params=_params("arbitrary", "arbitrary"),
        name="peer",
    )(h2.reshape(n, d), x1.reshape(n, d), gt, lp["p_wq"], lp["p_bq"], lp["p_keys"],
      lp["p_u"], lp["p_u"], lp["p_vt"], g_final)
    return out.reshape(b, t, d)


def _layer(x, mod, lp, vfirst, past, q_offset, g_final, final, tiles):
    b, t, d = x.shape
    sh_t, sc_t, gt_t, sh_c, sc_c, gt_c = (m[:, None, :] for m in jnp.split(mod, 6, axis=-1))
    qi, k, v, ik, iw, rw, gates = _inproj_call(
        x, sc_t, sh_t, lp["g_norm1"], lp["w_in"], lp["idx_k_g"], lp["idx_k_b"], tiles["tm"])

    if past is None:
        k_all, v_all, ik_all = k, v, ik
        m0 = jnp.zeros((b, RW_HEADS, RW_HEAD_DIM, RW_HEAD_DIM), F32)
        shift0 = jnp.zeros((b, 1, RW_PAD), F32)
    else:
        k_past, v_past, ik_past, s0, rw_prev = past
        pl_ = k_past.shape[1]
        k_all = jnp.concatenate([k_past.reshape(b, pl_, KV_W), k], axis=1)
        v_all = jnp.concatenate([v_past.reshape(b, pl_, KV_W), v], axis=1)
        ik_all = jnp.concatenate([ik_past, ik], axis=1)
        m0 = jnp.swapaxes(s0, -1, -2)
        shift0 = jnp.pad(rw_prev, ((0, 0), (0, 0), (0, RW_PAD - RW_COLS)))
    tq = tiles["tq"]
    t_pad = -(-t // tq) * tq
    qpad = ((0, 0), (0, t_pad - t), (0, 0))
    o_a = _dsa_call(jnp.pad(qi, qpad), jnp.pad(iw, qpad), k_all, v_all, ik_all,
                    q_offset=q_offset, tq=tq, tk=tiles["tk"])[:, :t]

    r, lw, k2, v2, kkn, bb, g, bonus = _rwprep_call(rw, shift0, lp, vfirst, tiles["tm"])
    if vfirst is None:
        vfirst = v2
    o_scan, m_new = _scan_call(r, lw, k2, v2, kkn, bb, m0, tiles["c"])

    x1, h2 = _merge_call(x, o_a, o_scan, bonus, g, gates, gt_t, sc_c, sh_c, lp, tiles["tm"])
    x2 = _peer_call(h2, x1, gt_c, lp, g_final, tn=tiles["tn"], eb=tiles["eb"], final=final)
    state = (k.reshape(b, t, KV_HEADS, HEAD_DIM), v.reshape(b, t, KV_HEADS, HEAD_DIM), ik,
             jnp.swapaxes(m_new, -1, -2), rw[:, -1:, :RW_COLS])
    return x2, vfirst, state


def _tiles(t):
    return {"tm": min(t, 512), "tq": 2 * LANES if t % (2 * LANES) == 0 else LANES,
            "tk": 256, "c": min(t, CHUNK),
            "tn": 512 if t >= 512 else LANES, "eb": PEER_EB}


def kernel(x_prompt, x_sample, cache_k, cache_v, cache_kidx, state_wkv, state_shift, c_prompt, c_sample, w_ada, b_ada, g_norm1, w_in, idx_k_g, idx_k_b, rw_mu, rw_w0, rw_w_up, rw_a0, rw_a_up, rw_g_up, rw_k_k, rw_k_a, rw_r_k, rw_lnx_g, rw_lnx_b, rw_v0, rw_v_down, rw_v_up, w_oa, w_ob, w_out, g_norm2, peer_wq, peer_bq, peer_sub_keys, peer_u, peer_v, g_final):
    depth = w_in.shape[0]
    nbp = x_prompt.shape[0]
    past_len = cache_k.shape[2]
    bd = _head_block_diag()
    row = lambda a: a.reshape(1, -1)
    xp, xs = x_prompt, x_sample
    vf_p, vf_s = None, None
    new_p, new_s = [], []
    c_all = jnp.concatenate([c_prompt, c_sample], axis=0)
    gf = row(g_final)
    for l in range(depth):
        lp = {
            "g_norm1": row(g_norm1[l]), "w_in": _pack_w_in(w_in[l]),
            "idx_k_g": row(idx_k_g[l]), "idx_k_b": row(idx_k_b[l]),
            "mu": jnp.pad(row(rw_mu[l]), ((0, 0), (0, RW_PAD - RW_COLS))),
            "w0": row(rw_w0[l]), "a0": row(rw_a0[l]),
            "w_up": _pad_rows(rw_w_up[l], 0, LANES),
            "a_up": _pad_rows(rw_a_up[l], W_LORA, LANES),
            "g_up": _pad_rows(rw_g_up[l], 0, RW_PAD - 3 * RW_WIDTH - LANES),
            "k_k": row(rw_k_k[l]), "k_a": row(rw_k_a[l]), "r_k": row(rw_r_k[l]),
            "lnx_g": row(rw_lnx_g[l]), "lnx_b": row(rw_lnx_b[l]), "bd": bd,
            "w_oa": w_oa[l].astype(BF16), "w_ob": w_ob[l].astype(BF16),
            "w_out": w_out[l].astype(BF16), "g_norm2": row(g_norm2[l]),
            "p_wq": peer_wq[l].astype(BF16), "p_bq": row(peer_bq[l]),
            "p_keys": peer_sub_keys[l].reshape(2 * P_HEADS, N_KEYS, P_HALF).astype(BF16),
            "p_u": peer_u[l].reshape(-1, PEER_EB, peer_u.shape[-1]).astype(BF16),
            "p_vt": jnp.swapaxes(peer_v[l].reshape(-1, PEER_EB, peer_v.shape[-1]), 1, 2
                                 ).astype(BF16),
        }
        if l > 0:
            lp["v0"] = row(rw_v0[l - 1])
            lp["v_down"] = jnp.pad(rw_v_down[l - 1], ((0, 0), (0, LANES - V_LORA))).astype(BF16)
            lp["v_up"] = _pad_rows(rw_v_up[l - 1], 0, LANES)
        mod = _mod_call(c_all, w_ada[l], b_ada[l])
        final = l == depth - 1
        xp, vf_p, st_p = _layer(xp, mod[:nbp], lp, vf_p, None, 0, gf, final,
                                _tiles(xp.shape[1]))
        past = (cache_k[l], cache_v[l], cache_kidx[l], state_wkv[l], state_shift[l])
        xs, vf_s, st_s = _layer(xs, mod[nbp:], lp, vf_s, past, past_len, gf, final,
                                _tiles(xs.shape[1]))
        new_p.append(st_p)
        new_s.append(st_s)

    def stk(lst, i):
        return jnp.stack([e[i] for e in lst], axis=0)

    return (xp, xs,
            stk(new_p, 0), stk(new_p, 1), stk(new_p, 2), stk(new_p, 3), stk(new_p, 4),
            stk(new_s, 0), stk(new_s, 1), stk(new_s, 2), stk(new_s, 3), stk(new_s, 4))
```
